```python
import math
import jax, jax.numpy as jnp
from jax import lax
import numpy as np

D_MODEL = 1024
BATCH = 32
SEQ = 2048
DEPTH = 4

N_MIXERS = 4
ALPHA = (2 * DEPTH) ** 0.25
BETA = (8 * DEPTH) ** -0.25
LN_EPS = 1e-5

ML_HEADS = 4
ML_DV = D_MODEL // ML_HEADS
ML_DK = ML_DV // 2
ML_CHUNK = 64
ML_IN = 2 * ML_HEADS * ML_DK + 2 * D_MODEL + 4 * ML_HEADS

GLA_HEADS = 4
GLA_DK = D_MODEL // 2 // GLA_HEADS
GLA_DV = D_MODEL // GLA_HEADS
GLA_RANK = 16
GLA_TAU = 16.0
GLA_CHUNK = 32
GLA_IN = 2 * GLA_HEADS * GLA_DK + 2 * D_MODEL + 2 * GLA_RANK

LRU_WIDTH = D_MODEL
LRU_BLOCKS = 4
LRU_BW = LRU_WIDTH // LRU_BLOCKS
CONV_WIDTH = 4
LRU_C = 8.0
LRU_IN = 2 * LRU_WIDTH

MLA_HEADS = 8
MLA_NOPE = 128
MLA_ROPE = 64
MLA_DV = 128
MLA_Q_RANK = 384
MLA_KV_RANK = 256
MLA_IN = MLA_Q_RANK + MLA_KV_RANK + MLA_ROPE
ROPE_THETA = 10000.0
Q_BLOCK = 128

N_EXPERTS = 16
EXPERT_FF = 1024
CAPACITY_FACTOR = 2

N_MLSTM = (DEPTH - 0 + N_MIXERS - 1) // N_MIXERS
N_GLA = (DEPTH - 1 + N_MIXERS - 1) // N_MIXERS
N_LRU = (DEPTH - 2 + N_MIXERS - 1) // N_MIXERS
N_MLA = (DEPTH - 3 + N_MIXERS - 1) // N_MIXERS

kernel_name = 'hybrid_mlstm_gla_rglru_mla_ecmoe_encoder'


def layer_norm(x, g, b):
    xf = x.astype(jnp.float32)
    mu = jnp.mean(xf, axis=-1, keepdims=True)
    var = jnp.mean(jnp.square(xf - mu), axis=-1, keepdims=True)
    return ((xf - mu) * lax.rsqrt(var + LN_EPS) * g + b).astype(x.dtype)


def rms_norm(x, g):
    xf = x.astype(jnp.float32)
    return (xf * lax.rsqrt(jnp.mean(jnp.square(xf), axis=-1, keepdims=True) + LN_EPS) * g).astype(x.dtype)


def head_norm(h, g):
    hf = h.astype(jnp.float32)
    mu = jnp.mean(hf, axis=-1, keepdims=True)
    var = jnp.mean(jnp.square(hf - mu), axis=-1, keepdims=True)
    hn = (hf - mu) * lax.rsqrt(var + LN_EPS)
    return hn.reshape(h.shape[0], h.shape[1], -1) * g.astype(jnp.float32)


def to_chunks(t, L):
    B, S = t.shape[0], t.shape[1]
    t = t.reshape((B, S // L, L) + t.shape[2:])
    return jnp.moveaxis(t, (1, 2), (0, 3))


def from_chunks(t):
    t = jnp.moveaxis(t, (0, 3), (1, 2))
    return t.reshape((t.shape[0], t.shape[1] * t.shape[2]) + t.shape[3:])


def flip_seq(t):
    return jnp.flip(t, axis=1)


def mlstm_chunked(q, k, v, ig, fg):
    B, S, H, DK = q.shape
    DV = v.shape[-1]
    L = ML_CHUNK
    mask = jnp.tril(jnp.ones((L, L), dtype=bool))

    def step(carry, inp):
        C, n, m = carry
        qb, kb, vb, ib, fb = inp
        b = jnp.cumsum(jax.nn.log_sigmoid(fb), axis=-1)
        g = b[..., -1]
        logD = jnp.where(mask, b[..., :, None] - b[..., None, :] + ib[..., None, :], -jnp.inf)
        inter = b + m[..., None]
        m_t = jnp.maximum(inter, jnp.max(logD, axis=-1))
        s = jnp.einsum('bhtd,bhsd->bhts', qb, kb) * jnp.exp(logD - m_t[..., None])
        w_inter = jnp.exp(inter - m_t)
        num = jnp.einsum('bhts,bhsv->bhtv', s, vb) + w_inter[..., None] * jnp.einsum('bhtd,bhdv->bhtv', qb, C)
        den = jnp.sum(s, axis=-1) + w_inter * jnp.einsum('bhtd,bhd->bht', qb, n)
        h = num / jnp.maximum(jnp.abs(den), jnp.exp(-m_t))[..., None]
        lw = g[..., None] - b + ib
        m_new = jnp.maximum(g + m, jnp.max(lw, axis=-1))
        ws = jnp.exp(lw - m_new[..., None])
        wc = jnp.exp(g + m - m_new)
        kw = kb * ws[..., None]
        C = wc[..., None, None] * C + jnp.einsum('bhsd,bhsv->bhdv', kw, vb)
        n = wc[..., None] * n + jnp.sum(kw, axis=-2)
        return (C, n, m_new), h

    init = (jnp.zeros((B, H, DK, DV), jnp.float32), jnp.zeros((B, H, DK), jnp.float32),
            jnp.zeros((B, H), jnp.float32))
    xs = (to_chunks(q, L), to_chunks(k, L), to_chunks(v, L), to_chunks(ig, L), to_chunks(fg, L))
    _, h = lax.scan(step, init, xs)
    return from_chunks(h)


def mlstm_mixer(x, w_in, gate_b, norm_g, w_out):
    B, S, _ = x.shape
    f32 = jnp.float32
    qk = ML_HEADS * ML_DK
    z = x @ w_in
    q, k, v, o, gates = jnp.split(z, [qk, 2 * qk, 2 * qk + D_MODEL, 2 * qk + 2 * D_MODEL], axis=-1)
    q = q.reshape(B, S, ML_HEADS, ML_DK).astype(f32) * (ML_DK ** -0.5)
    k = k.reshape(B, S, ML_HEADS, ML_DK).astype(f32)
    v = v.reshape(B, S, ML_HEADS, ML_DV).astype(f32)
    gates = gates.reshape(B, S, 2, 2, ML_HEADS).astype(f32) + gate_b.astype(f32)
    h_fwd = mlstm_chunked(q, k, v, gates[:, :, 0, 0], gates[:, :, 0, 1])
    h_bwd = flip_seq(mlstm_chunked(flip_seq(q), flip_seq(k), flip_seq(v),
                                   flip_seq(gates[:, :, 1, 0]), flip_seq(gates[:, :, 1, 1])))
    h = head_norm(h_fwd + h_bwd, norm_g).astype(x.dtype)
    return (jax.nn.sigmoid(o) * h) @ w_out


def gla_chunked(q, k, v, loga):
    B, S, H, DK = q.shape
    DV = v.shape[-1]
    L = GLA_CHUNK
    mask = jnp.tril(jnp.ones((L, L), dtype=bool))[:, :, None]

    def step(state, inp):
        qb, kb, vb, ab = inp
        b = jnp.cumsum(ab, axis=-2)
        dec = jnp.exp(jnp.where(mask, b[:, :, :, None, :] - b[:, :, None, :, :], -jnp.inf))
        A = jnp.einsum('bhtd,bhtsd,bhsd->bhts', qb, dec, kb)
        o = jnp.einsum('bhts,bhsv->bhtv', A, vb) + jnp.einsum('bhtd,bhdv->bhtv', qb * jnp.exp(b), state)
        g = b[:, :, -1, :]
        state = jnp.exp(g)[..., None] * state + jnp.einsum('bhsd,bhsv->bhdv', kb * jnp.exp(g[:, :, None, :] - b), vb)
        return state, o

    init = jnp.zeros((B, H, DK, DV), jnp.float32)
    xs = (to_chunks(q, L), to_chunks(k, L), to_chunks(v, L), to_chunks(loga, L))
    _, o = lax.scan(step, init, xs)
    return from_chunks(o)


def gla_mixer(x, w_in, gate_w, gate_b, norm_g, w_out):
    B, S, _ = x.shape
    f32 = jnp.float32
    qk = GLA_HEADS * GLA_DK
    z = x @ w_in
    q, k, v, r, glr = jnp.split(z, [qk, 2 * qk, 2 * qk + D_MODEL, 2 * qk + 2 * D_MODEL], axis=-1)
    q = q.reshape(B, S, GLA_HEADS, GLA_DK).astype(f32) * (GLA_DK ** -0.5)
    k = k.reshape(B, S, GLA_HEADS, GLA_DK).astype(f32)
    v = v.reshape(B, S, GLA_HEADS, GLA_DV).astype(f32)
    glr = glr.reshape(B, S, 2, GLA_RANK).astype(f32)
    loga = jax.nn.log_sigmoid(jnp.einsum('bsgr,grk->bsgk', glr, gate_w.astype(f32)) + gate_b.astype(f32)) / GLA_TAU
    loga = loga.reshape(B, S, 2, GLA_HEADS, GLA_DK)
    o_fwd = gla_chunked(q, k, v, loga[:, :, 0])
    o_bwd = flip_seq(gla_chunked(flip_seq(q), flip_seq(k), flip_seq(v), flip_seq(loga[:, :, 1])))
    o = head_norm(o_fwd + o_bwd, norm_g).astype(x.dtype)
    return (jax.nn.silu(r) * o) @ w_out


def rglru_direction(u, wa, ba, wx, bx, lam):
    B, S, W = u.shape
    ub = u.reshape(B, S, LRU_BLOCKS, LRU_BW)
    r = jax.nn.sigmoid(jnp.einsum('bsni,nij->bsnj', ub, wa).reshape(B, S, W) + ba)
    i = jax.nn.sigmoid(jnp.einsum('bsni,nij->bsnj', ub, wx).reshape(B, S, W) + bx)
    log_a = LRU_C * r * jax.nn.log_sigmoid(lam)
    a = jnp.exp(log_a)
    gated = jnp.sqrt(-jnp.expm1(2.0 * log_a)) * (i * u)

    def step(h, inp):
        a_t, g_t = inp
        h = a_t * h + g_t
        return h, h

    _, hs = lax.scan(step, jnp.zeros((B, W), jnp.float32), (jnp.swapaxes(a, 0, 1), jnp.swapaxes(gated, 0, 1)))
    return jnp.swapaxes(hs, 0, 1)


def rglru_mixer(x, w_in, conv_w, conv_b, gate_a_w, gate_a_b, gate_x_w, gate_x_b, lam, w_out):
    f32 = jnp.float32
    z = x @ w_in
    gate_branch, u = jnp.split(z, [LRU_WIDTH], axis=-1)
    u = lax.conv_general_dilated(u, conv_w[:, None, :].astype(u.dtype), window_strides=(1,),
                                 padding=[(CONV_WIDTH // 2, CONV_WIDTH - 1 - CONV_WIDTH // 2)],
                                 dimension_numbers=('NWC', 'WIO', 'NWC'),
                                 feature_group_count=LRU_WIDTH) + conv_b
    u = u.astype(f32)
    h_fwd = rglru_direction(u, gate_a_w[0].astype(f32), gate_a_b[0].astype(f32), gate_x_w[0].astype(f32),
                            gate_x_b[0].astype(f32), lam[0].astype(f32))
    h_bwd = flip_seq(rglru_direction(flip_seq(u), gate_a_w[1].astype(f32), gate_a_b[1].astype(f32),
                                     gate_x_w[1].astype(f32), gate_x_b[1].astype(f32), lam[1].astype(f32)))
    h = (h_fwd + h_bwd).astype(x.dtype)
    return (jax.nn.gelu(gate_branch) * h) @ w_out


def apply_rope(t, cos, sin):
    half = t.shape[-1] // 2
    t1, t2 = t[..., :half], t[..., half:]
    return jnp.concatenate([t1 * cos - t2 * sin, t2 * cos + t1 * sin], axis=-1)


def mla_mixer(x, positions, w_in, q_norm_g, kv_norm_g, w_uq, w_ukv, w_out):
    B, S, _ = x.shape
    f32 = jnp.float32
    z = x @ w_in
    cq, ckv, k_rope = jnp.split(z, [MLA_Q_RANK, MLA_Q_RANK + MLA_KV_RANK], axis=-1)
    q = (rms_norm(cq, q_norm_g) @ w_uq).reshape(B, S, MLA_HEADS, MLA_NOPE + MLA_ROPE)
    kv = (rms_norm(ckv, kv_norm_g) @ w_ukv).reshape(B, S, MLA_HEADS, MLA_NOPE + MLA_DV)
    q_nope, q_rope = jnp.split(q, [MLA_NOPE], axis=-1)
    k_nope, v = jnp.split(kv, [MLA_NOPE], axis=-1)
    half = MLA_ROPE // 2
    freq = ROPE_THETA ** (-jnp.arange(half, dtype=f32) / half)
    ang = positions.astype(f32)[..., None] * freq
    cos = jnp.cos(ang).astype(x.dtype)
    sin = jnp.sin(ang).astype(x.dtype)
    q_rope = apply_rope(q_rope, cos[:, :, None, :], sin[:, :, None, :])
    k_rope = apply_rope(k_rope, cos, sin)
    scale = (MLA_NOPE + MLA_ROPE) ** -0.5
    nq = S // Q_BLOCK
    qn_blocks = jnp.moveaxis(q_nope.reshape(B, nq, Q_BLOCK, MLA_HEADS, MLA_NOPE), 1, 0)
    qr_blocks = jnp.moveaxis(q_rope.reshape(B, nq, Q_BLOCK, MLA_HEADS, MLA_ROPE), 1, 0)

    def attend(blk):
        qn, qr = blk
        s = (jnp.einsum('bqhd,bkhd->bhqk', qn, k_nope) + jnp.einsum('bqhd,bkd->bhqk', qr, k_rope)).astype(f32) * scale
        p = jax.nn.softmax(s, axis=-1).astype(v.dtype)
        return jnp.einsum('bhqk,bkhv->bqhv', p, v)

    out = lax.map(attend, (qn_blocks, qr_blocks))
    out = jnp.moveaxis(out, 0, 1).reshape(B, S, MLA_HEADS * MLA_DV)
    return out @ w_out


def ec_moe(x, w_router, w_gate, w_up, w_down):
    B, N, _ = x.shape
    cap = CAPACITY_FACTOR * N // N_EXPERTS
    aff = jax.nn.softmax(jnp.einsum('bnd,de->bne', x, w_router).astype(jnp.float32), axis=-1)
    gates, idx = lax.top_k(jnp.swapaxes(aff, 1, 2), cap)
    bidx = jnp.arange(B)[:, None, None]
    xin = x[bidx, idx]
    h = jax.nn.silu(jnp.einsum('becd,edf->becf', xin, w_gate)) * jnp.einsum('becd,edf->becf', xin, w_up)
    out = jnp.einsum('becf,efd->becd', h, w_down) * gates[..., None].astype(x.dtype)
    return jnp.zeros_like(x).at[bidx, idx].add(out.astype(x.dtype))


def setup_inputs(seed: int = 0) -> dict:
    key = jax.random.key(seed)
    keys = iter(jax.random.split(key, 64))
    f32 = jnp.float32

    def nrm(shape, scale):
        return jax.random.normal(next(keys), shape, f32) * scale

    def gain(shape):
        return 1.0 + nrm(shape, 0.02)

    D = D_MODEL
    x = jax.random.normal(next(keys), (BATCH, SEQ, D), f32)
    positions = (jnp.arange(SEQ, dtype=jnp.int32)[None, :]
                 + jax.random.randint(next(keys), (BATCH, 1), 0, SEQ, dtype=jnp.int32))
    mlstm_w_in = nrm((N_MLSTM, D, ML_IN), D ** -0.5)
    ig_b = nrm((N_MLSTM, 2, 1, ML_HEADS), 0.1)
    fg_b = 3.0 + 3.0 * jax.random.uniform(next(keys), (N_MLSTM, 2, 1, ML_HEADS), f32)
    mlstm_gate_b = jnp.concatenate([ig_b, fg_b], axis=2)
    mlstm_norm_g = gain((N_MLSTM, D))
    mlstm_w_out = nrm((N_MLSTM, D, D), BETA * D ** -0.5)
    gla_w_in = nrm((N_GLA, D, GLA_IN), D ** -0.5)
    gla_gate_w = nrm((N_GLA, 2, GLA_RANK, GLA_HEADS * GLA_DK), GLA_RANK ** -0.5)
    gla_gate_b = nrm((N_GLA, 2, GLA_HEADS * GLA_DK), 0.1)
    gla_norm_g = gain((N_GLA, D))
    gla_w_out = nrm((N_GLA, D, D), BETA * D ** -0.5)
    lru_w_in = nrm((N_LRU, D, LRU_IN), D ** -0.5)
    lru_conv_w = nrm((N_LRU, CONV_WIDTH, LRU_WIDTH), CONV_WIDTH ** -0.5)
    lru_conv_b = nrm((N_LRU, LRU_WIDTH), 0.02)
    lru_gate_a_w = nrm((N_LRU, 2, LRU_BLOCKS, LRU_BW, LRU_BW), LRU_BW ** -0.5)
    lru_gate_a_b = nrm((N_LRU, 2, LRU_WIDTH), 0.1)
    lru_gate_x_w = nrm((N_LRU, 2, LRU_BLOCKS, LRU_BW, LRU_BW), LRU_BW ** -0.5)
    lru_gate_x_b = nrm((N_LRU, 2, LRU_WIDTH), 0.1)
    a0 = jax.random.uniform(next(keys), (N_LRU, 2, LRU_WIDTH), f32, minval=0.9, maxval=0.999)
    s0 = a0 ** (1.0 / LRU_C)
    lru_lambda = jnp.log(s0) - jnp.log1p(-s0)
    lru_w_out = nrm((N_LRU, LRU_WIDTH, D), BETA * LRU_WIDTH ** -0.5)
    mla_w_in = nrm((N_MLA, D, MLA_IN), D ** -0.5)
    mla_q_norm_g = gain((N_MLA, MLA_Q_RANK))
    mla_kv_norm_g = gain((N_MLA, MLA_KV_RANK))
    mla_w_uq = nrm((N_MLA, MLA_Q_RANK, MLA_HEADS * (MLA_NOPE + MLA_ROPE)), MLA_Q_RANK ** -0.5)
    mla_w_ukv = nrm((N_MLA, MLA_KV_RANK, MLA_HEADS * (MLA_NOPE + MLA_DV)), MLA_KV_RANK ** -0.5)
    mla_w_out = nrm((N_MLA, MLA_HEADS * MLA_DV, D), BETA * (MLA_HEADS * MLA_DV) ** -0.5)
    moe_router = nrm((DEPTH, D, N_EXPERTS), D ** -0.5)
    moe_w_gate = nrm((DEPTH, N_EXPERTS, D, EXPERT_FF), D ** -0.5)
    moe_w_up = nrm((DEPTH, N_EXPERTS, D, EXPERT_FF), D ** -0.5)
    moe_w_down = nrm((DEPTH, N_EXPERTS, EXPERT_FF, D), BETA * EXPERT_FF ** -0.5)
    ln_g = gain((DEPTH, 2, D))
    ln_b = nrm((DEPTH, 2, D), 0.02)
    return {
        'x': x, 'positions': positions,
        'mlstm_w_in': mlstm_w_in, 'mlstm_gate_b': mlstm_gate_b, 'mlstm_norm_g': mlstm_norm_g, 'mlstm_w_out': mlstm_w_out,
        'gla_w_in': gla_w_in, 'gla_gate_w': gla_gate_w, 'gla_gate_b': gla_gate_b, 'gla_norm_g': gla_norm_g, 'gla_w_out': gla_w_out,
        'lru_w_in': lru_w_in, 'lru_conv_w': lru_conv_w, 'lru_conv_b': lru_conv_b,
        'lru_gate_a_w': lru_gate_a_w, 'lru_gate_a_b': lru_gate_a_b, 'lru_gate_x_w': lru_gate_x_w, 'lru_gate_x_b': lru_gate_x_b,
        'lru_lambda': lru_lambda, 'lru_w_out': lru_w_out,
        'mla_w_in': mla_w_in, 'mla_q_norm_g': mla_q_norm_g, 'mla_kv_norm_g': mla_kv_norm_g,
        'mla_w_uq': mla_w_uq, 'mla_w_ukv': mla_w_ukv, 'mla_w_out': mla_w_out,
        'moe_router': moe_router, 'moe_w_gate': moe_w_gate, 'moe_w_up': moe_w_up, 'moe_w_down': moe_w_down,
        'ln_g': ln_g, 'ln_b': ln_b,
    }


def reference(x, positions,
              mlstm_w_in, mlstm_gate_b, mlstm_norm_g, mlstm_w_out,
              gla_w_in, gla_gate_w, gla_gate_b, gla_norm_g, gla_w_out,
              lru_w_in, lru_conv_w, lru_conv_b, lru_gate_a_w, lru_gate_a_b, lru_gate_x_w, lru_gate_x_b,
              lru_lambda, lru_w_out,
              mla_w_in, mla_q_norm_g, mla_kv_norm_g, mla_w_uq, mla_w_ukv, mla_w_out,
              moe_router, moe_w_gate, moe_w_up, moe_w_down,
              ln_g, ln_b):
    for i in range(DEPTH):
        m = i % N_MIXERS
        j = i // N_MIXERS
        if m == 0:
            h = mlstm_mixer(x, mlstm_w_in[j], mlstm_gate_b[j], mlstm_norm_g[j], mlstm_w_out[j])
        elif m == 1:
            h = gla_mixer(x, gla_w_in[j], gla_gate_w[j], gla_gate_b[j], gla_norm_g[j], gla_w_out[j])
        elif m == 2:
            h = rglru_mixer(x, lru_w_in[j], lru_conv_w[j], lru_conv_b[j], lru_gate_a_w[j], lru_gate_a_b[j],
                            lru_gate_x_w[j], lru_gate_x_b[j], lru_lambda[j], lru_w_out[j])
        else:
            h = mla_mixer(x, positions, mla_w_in[j], mla_q_norm_g[j], mla_kv_norm_g[j],
                          mla_w_uq[j], mla_w_ukv[j], mla_w_out[j])
        x = layer_norm(ALPHA * x + h, ln_g[i, 0], ln_b[i, 0])
        y = ec_moe(x, moe_router[i], moe_w_gate[i], moe_w_up[i], moe_w_down[i])
        x = layer_norm(ALPHA * x + y, ln_g[i, 1], ln_b[i, 1])
    return x
```

```python
import functools
import math

import jax
import jax.numpy as jnp
from jax import lax
from jax.experimental import pallas as pl
from jax.experimental.pallas import tpu as pltpu

F32 = jnp.float32
BF16 = jnp.bfloat16
HIGHEST = lax.Precision.HIGHEST

D_MODEL = 1024
DEPTH = 4
N_MIXERS = 4
ALPHA = (2 * DEPTH) ** 0.25
LN_EPS = 1e-5

ML_HEADS = 4
ML_DV = D_MODEL // ML_HEADS
ML_DK = ML_DV // 2

GLA_HEADS = 4
GLA_DK = D_MODEL // 2 // GLA_HEADS
GLA_DV = D_MODEL // GLA_HEADS
GLA_RANK = 16
GLA_TAU = 16.0

LRU_WIDTH = D_MODEL
LRU_BLOCKS = 4
LRU_BW = LRU_WIDTH // LRU_BLOCKS
CONV_WIDTH = 4
LRU_C = 8.0

MLA_HEADS = 8
MLA_NOPE = 128
MLA_ROPE = 64
MLA_DV = 128
MLA_Q_RANK = 384
MLA_KV_RANK = 256
ROPE_THETA = 10000.0

N_EXPERTS = 16
CAPACITY_FACTOR = 2

V7X_VMEM_BYTES = 64 * 1024 * 1024
VMEM_LIMIT = V7X_VMEM_BYTES - 8 * 1024 * 1024
LANES = 128
SUBLANES = 8

ROW_TILE = 512
SEQ_TILE = 512
ML_CHUNK = 256
GLA_CHUNK = 64
Q_TILE = 512
N_CHUNK = 512


def _params(*sem):
    return pltpu.CompilerParams(dimension_semantics=sem, vmem_limit_bytes=VMEM_LIMIT)


def _log_sigmoid(x):
    return jnp.minimum(x, 0.0) - jnp.log(1.0 + jnp.exp(-jnp.abs(x)))


def _sigmoid(x):
    return 1.0 / (1.0 + jnp.exp(-x))


def _layer_norm(v, g, b):
    mu = jnp.mean(v, axis=-1, keepdims=True)
    d = v - mu
    var = jnp.mean(d * d, axis=-1, keepdims=True)
    return d * lax.rsqrt(var + LN_EPS) * g + b


def _dot(a, b):
    return jnp.dot(a, b, preferred_element_type=F32)


def _dot_nt(a, b):
    return lax.dot_general(a, b, (((1,), (1,)), ((), ())), preferred_element_type=F32)


def _dot_tn(a, b):
    return lax.dot_general(a, b, (((0,), (0,)), ((), ())), preferred_element_type=F32)


def _dense_kernel(*refs, n_w, n_t, has_bias):
    x_ref = refs[0]
    pos = 1
    w_refs = refs[pos:pos + n_w]
    pos += n_w
    b_refs = []
    for hb in has_bias:
        if hb:
            b_refs.append(refs[pos])
            pos += 1
        else:
            b_refs.append(None)
    t_refs = refs[pos:pos + 2 * n_t]
    pos += 2 * n_t
    o_refs = refs[pos:pos + n_w]
    pos += n_w
    ot_refs = refs[pos:pos + n_t]

    xb = x_ref[...].astype(BF16)
    for w_ref, b_ref, o_ref in zip(w_refs, b_refs, o_refs):
        n = w_ref.shape[1]
        for j0 in range(0, n, N_CHUNK):
            j1 = min(n, j0 + N_CHUNK)
            acc = _dot(xb, w_ref[:, j0:j1])
            if b_ref is not None:
                acc = acc + b_ref[:, j0:j1]
            o_ref[:, j0:j1] = acc.astype(o_ref.dtype)
    for i in range(n_t):
        wt_ref, bt_ref = t_refs[2 * i], t_refs[2 * i + 1]
        ot_refs[i][...] = _dot_nt(wt_ref[...], xb) + bt_ref[...]


def _dense(x, ws, dtypes, biases=None, transposed=()):
    T, K = x.shape
    tm = min(ROW_TILE, T)
    if biases is None:
        biases = [None] * len(ws)
    has_bias = tuple(b is not None for b in biases)
    args = [x] + list(ws) + [b for b in biases if b is not None]
    in_specs = [pl.BlockSpec((tm, K), lambda i: (i, 0))]
    in_specs += [pl.BlockSpec(w.shape, lambda i: (0, 0)) for w in ws]
    in_specs += [pl.BlockSpec(b.shape, lambda i: (0, 0)) for b in biases if b is not None]
    for wt, bt in transposed:
        args += [wt, bt]
        in_specs += [pl.BlockSpec(wt.shape, lambda i: (0, 0)), pl.BlockSpec(bt.shape, lambda i: (0, 0))]
    out_shape = [jax.ShapeDtypeStruct((T, w.shape[1]), dt) for w, dt in zip(ws, dtypes)]
    out_specs = [pl.BlockSpec((tm, w.shape[1]), lambda i: (i, 0)) for w in ws]
    for wt, _ in transposed:
        out_shape.append(jax.ShapeDtypeStruct((wt.shape[0], T), F32))
        out_specs.append(pl.BlockSpec((wt.shape[0], tm), lambda i: (0, i)))
    kern = functools.partial(_dense_kernel, n_w=len(ws), n_t=len(transposed), has_bias=has_bias)
    return pl.pallas_call(
        kern, grid=(T // tm,), in_specs=in_specs, out_specs=out_specs, out_shape=out_shape,
        compiler_params=_params("parallel"), name="dense")(*args)


def _outproj_ln_kernel(a_ref, w_ref, x_ref, g_ref, b_ref, o_ref, ob_ref):
    y = _dot(a_ref[...], w_ref[...])
    v = _layer_norm(ALPHA * x_ref[...] + y, g_ref[...], b_ref[...])
    o_ref[...] = v
    ob_ref[...] = v.astype(BF16)


def _outproj_ln(a, w, x, g, b):
    T, K = a.shape
    D = w.shape[1]
    tm = min(ROW_TILE, T)
    row = lambda i: (i, 0)
    fix = lambda i: (0, 0)
    return pl.pallas_call(
        _outproj_ln_kernel, grid=(T // tm,),
        in_specs=[pl.BlockSpec((tm, K), row), pl.BlockSpec((K, D), fix), pl.BlockSpec((tm, D), row),
                  pl.BlockSpec((1, D), fix), pl.BlockSpec((1, D), fix)],
        out_specs=[pl.BlockSpec((tm, D), row), pl.BlockSpec((tm, D), row)],
        out_shape=[jax.ShapeDtypeStruct((T, D), F32), jax.ShapeDtypeStruct((T, D), BF16)],
        compiler_params=_params("parallel"), name="outproj_ln")(a, w, x, g.reshape(1, D), b.reshape(1, D))


def _residual_ln_kernel(x_ref, y_ref, g_ref, b_ref, o_ref):
    o_ref[...] = _layer_norm(ALPHA * x_ref[...] + y_ref[...], g_ref[...], b_ref[...])


def _residual_ln(x, y, g, b):
    T, D = x.shape
    tm = min(ROW_TILE, T)
    row = lambda i: (i, 0)
    fix = lambda i: (0, 0)
    return pl.pallas_call(
        _residual_ln_kernel, grid=(T // tm,),
        in_specs=[pl.BlockSpec((tm, D), row), pl.BlockSpec((tm, D), row),
                  pl.BlockSpec((1, D), fix), pl.BlockSpec((1, D), fix)],
        out_specs=pl.BlockSpec((tm, D), row),
        out_shape=jax.ShapeDtypeStruct((T, D), F32),
        compiler_params=_params("parallel"), name="residual_ln")(x, y, g.reshape(1, D), b.reshape(1, D))


def _head_norm(h, n_heads, dv):
    outs = []
    for hh in range(n_heads):
        v = h[:, hh * dv:(hh + 1) * dv]
        mu = jnp.mean(v, axis=-1, keepdims=True)
        d = v - mu
        var = jnp.mean(d * d, axis=-1, keepdims=True)
        outs.append(d * lax.rsqrt(var + LN_EPS))
    return jnp.concatenate(outs, axis=1)


def _mlstm_kernel(*refs, reverse, ts, chunk):
    if reverse:
        qkv_ref, gr_ref, hf_ref, o_ref, ng_ref, out_ref, c_ref, n_ref, m_ref = refs
    else:
        qkv_ref, gr_ref, out_ref, c_ref, n_ref, m_ref = refs
    H, DK, DV, L = ML_HEADS, ML_DK, ML_DV, chunk
    scale = DK ** -0.5

    @pl.when(pl.program_id(1) == 0)
    def _():
        c_ref[...] = jnp.zeros_like(c_ref)
        n_ref[...] = jnp.zeros_like(n_ref)
        m_ref[...] = jnp.zeros_like(m_ref)

    rows = lax.broadcasted_iota(jnp.int32, (L, L), 0)
    cols = lax.broadcasted_iota(jnp.int32, (L, L), 1)
    mask = (cols >= rows) if reverse else (cols <= rows)
    tri = jnp.where((rows >= cols) if reverse else (rows <= cols), 1.0, 0.0).astype(F32)
    eye = rows == cols
    d0 = 8 if reverse else 0
    last = 0 if reverse else L - 1
    n_chunks = ts // L
    order = range(n_chunks - 1, -1, -1) if reverse else range(n_chunks)
    neg_inf = -jnp.inf

    for c in order:
        r0 = c * L
        g8 = gr_ref[d0:d0 + 8, r0:r0 + L]
        lf8 = _log_sigmoid(g8)
        b8 = jnp.dot(lf8, tri, precision=HIGHEST, preferred_element_type=F32)
        u8 = g8[0:4, :] - b8[4:8, :]
        for h in range(H):
            qb = qkv_ref[r0:r0 + L, h * DK:(h + 1) * DK]
            kb = qkv_ref[r0:r0 + L, H * DK + h * DK:H * DK + (h + 1) * DK]
            vb = qkv_ref[r0:r0 + L, 2 * H * DK + h * DV:2 * H * DK + (h + 1) * DV]
            u_r = u8[h:h + 1, :]
            b_r = b8[4 + h:5 + h, :]
            m_prev = m_ref[h:h + 1, 0:1]
            u_c = jnp.sum(jnp.where(eye, u_r, 0.0), axis=1, keepdims=True)
            b_c = jnp.sum(jnp.where(eye, b_r, 0.0), axis=1, keepdims=True)
            um = jnp.where(mask, u_r, neg_inf)
            a_c = jnp.maximum(m_prev, jnp.max(um, axis=1, keepdims=True))
            dmat = jnp.exp(um - a_c)
            s = _dot_nt(qb, kb) * (scale * dmat)
            w_int = jnp.exp(m_prev - a_c) * scale
            c_old = c_ref[h]
            num = _dot(s.astype(BF16), vb) + w_int * _dot(qb, c_old.astype(BF16))
            qn = jnp.sum(qb.astype(F32) * n_ref[h], axis=1, keepdims=True)
            den = jnp.sum(s, axis=1, keepdims=True) + w_int * qn
            hh = num / jnp.maximum(jnp.abs(den), jnp.exp(-(a_c + b_c)))
            a_last = jnp.maximum(m_prev, jnp.max(u_r, axis=1, keepdims=True))
            g_tot = b_r[:, last:last + 1]
            ws_c = jnp.exp(u_c - a_last)
            wc = jnp.exp(m_prev - a_last)
            kw = kb.astype(F32) * ws_c
            c_ref[h] = wc * c_old + _dot_tn(kw.astype(BF16), vb)
            n_ref[h] = wc * n_ref[h] + jnp.sum(kw, axis=0, keepdims=True)
            m_ref[h:h + 1, :] = jnp.broadcast_to(g_tot + a_last, (1, LANES))
            if reverse:
                hs = hf_ref[r0:r0 + L, h * DV:(h + 1) * DV] + hh
                mu = jnp.mean(hs, axis=-1, keepdims=True)
                dd = hs - mu
                var = jnp.mean(dd * dd, axis=-1, keepdims=True)
                hn = dd * lax.rsqrt(var + LN_EPS) * ng_ref[:, h * DV:(h + 1) * DV]
                og = _sigmoid(o_ref[r0:r0 + L, h * DV:(h + 1) * DV])
                out_ref[r0:r0 + L, h * DV:(h + 1) * DV] = (og * hn).astype(out_ref.dtype)
            else:
                out_ref[r0:r0 + L, h * DV:(h + 1) * DV] = hh


def _mlstm_pass(qkv, gr, B, S, reverse, hf=None, o=None, ng=None):
    T = B * S
    ts = min(SEQ_TILE, S)
    chunk = min(ML_CHUNK, ts)
    nb = S // ts
    if reverse:
        blk = lambda b, j: (b * nb + nb - 1 - j, 0)
        blk_t = lambda b, j: (0, b * nb + nb - 1 - j)
    else:
        blk = lambda b, j: (b * nb + j, 0)
        blk_t = lambda b, j: (0, b * nb + j)
    wq = qkv.shape[1]
    in_specs = [pl.BlockSpec((ts, wq), blk), pl.BlockSpec((16, ts), blk_t)]
    args = [qkv, gr]
    if reverse:
        in_specs += [pl.BlockSpec((ts, D_MODEL), blk), pl.BlockSpec((ts, D_MODEL), blk),
                     pl.BlockSpec((1, D_MODEL), lambda b, j: (0, 0))]
        args += [hf, o, ng.reshape(1, D_MODEL)]
    out_dtype = BF16 if reverse else F32
    kern = functools.partial(_mlstm_kernel, reverse=reverse, ts=ts, chunk=chunk)
    return pl.pallas_call(
        kern, grid=(B, nb), in_specs=in_specs,
        out_specs=pl.BlockSpec((ts, D_MODEL), blk),
        out_shape=jax.ShapeDtypeStruct((T, D_MODEL), out_dtype),
        scratch_shapes=[pltpu.VMEM((ML_HEADS, ML_DK, ML_DV), F32),
                        pltpu.VMEM((ML_HEADS, 1, ML_DK), F32),
                        pltpu.VMEM((SUBLANES, LANES), F32)],
        compiler_params=_params("parallel", "arbitrary"),
        name="mlstm_bwd" if reverse else "mlstm_fwd")(*args)


def _mlstm_layer(x, xb, B, S, w_in, gate_b, norm_g, w_out, ln_g, ln_b):
    del xb
    qk = ML_HEADS * ML_DK
    w = w_in.astype(BF16)
    w_qkv = w[:, :2 * qk + D_MODEL]
    w_o = w[:, 2 * qk + D_MODEL:2 * qk + 2 * D_MODEL]
    w_g_t = w[:, 2 * qk + 2 * D_MODEL:].T
    b_g = gate_b.astype(F32).reshape(16, 1)
    qkv, o, gr = _dense(x, [w_qkv, w_o], [BF16, F32], transposed=[(w_g_t, b_g)])
    hf = _mlstm_pass(qkv, gr, B, S, False)
    a = _mlstm_pass(qkv, gr, B, S, True, hf, o, norm_g.astype(F32))
    return _outproj_ln(a, w_out.astype(BF16), x, ln_g, ln_b)


def _cumsum_rows(x, n, reverse):
    row = lax.broadcasted_iota(jnp.int32, x.shape, 0)
    sh = 1
    while sh < n:
        if reverse:
            x = x + jnp.where(row < n - sh, pltpu.roll(x, n - sh, axis=0), 0.0)
        else:
            x = x + jnp.where(row >= sh, pltpu.roll(x, sh, axis=0), 0.0)
        sh *= 2
    return x


def _gla_kernel(*refs, reverse, ts, chunk):
    if reverse:
        qkv_ref, glr_ref, gw_ref, gb_ref, of_ref, r_ref, ng_ref, out_ref, st_ref, la_ref = refs
    else:
        qkv_ref, glr_ref, gw_ref, gb_ref, out_ref, st_ref, la_ref = refs
    H, DK, DV, L = GLA_HEADS, GLA_DK, GLA_DV, chunk
    scale = DK ** -0.5
    d = 1 if reverse else 0

    @pl.when(pl.program_id(1) == 0)
    def _():
        st_ref[...] = jnp.zeros_like(st_ref)

    glr = glr_ref[:, d * GLA_RANK:(d + 1) * GLA_RANK]
    pre = jnp.dot(glr, gw_ref[d], precision=HIGHEST, preferred_element_type=F32) + gb_ref[d:d + 1, :]
    la_ref[...] = _log_sigmoid(pre) * (1.0 / GLA_TAU)

    rows = lax.broadcasted_iota(jnp.int32, (L, L), 0)
    cols = lax.broadcasted_iota(jnp.int32, (L, L), 1)
    mask = (cols >= rows) if reverse else (cols <= rows)
    last = 0 if reverse else L - 1
    mid = L // 2
    n_chunks = ts // L

    def body(i, carry):
        c = (n_chunks - 1 - i) if reverse else i
        r0 = pl.multiple_of(c * L, L)
        bsum = _cumsum_rows(la_ref[pl.ds(r0, L), :], L, reverse)
        for h in range(H):
            b = bsum[:, h * DK:(h + 1) * DK]
            qf = qkv_ref[pl.ds(r0, L), h * DK:(h + 1) * DK].astype(F32)
            kf = qkv_ref[pl.ds(r0, L), H * DK + h * DK:H * DK + (h + 1) * DK].astype(F32)
            vb = qkv_ref[pl.ds(r0, L), 2 * H * DK + h * DV:2 * H * DK + (h + 1) * DV]
            beta = b[mid:mid + 1, :]
            g = b[last:last + 1, :]
            qt = (qf * jnp.exp(b - beta)).astype(BF16)
            kt = (kf * jnp.exp(beta - b)).astype(BF16)
            amat = jnp.where(mask, _dot_nt(qt, kt) * scale, 0.0)
            qh = (qf * (jnp.exp(b) * scale)).astype(BF16)
            st = st_ref[h]
            o = _dot(amat.astype(BF16), vb) + _dot_nt(qh, st.astype(BF16))
            kh = (kf * jnp.exp(g - b)).astype(BF16)
            st_ref[h] = st * jnp.exp(g) + _dot_tn(vb, kh)
            if reverse:
                hs = of_ref[pl.ds(r0, L), h * DV:(h + 1) * DV] + o
                mu = jnp.mean(hs, axis=-1, keepdims=True)
                dd = hs - mu
                var = jnp.mean(dd * dd, axis=-1, keepdims=True)
                hn = dd * lax.rsqrt(var + LN_EPS) * ng_ref[:, h * DV:(h + 1) * DV]
                rr = r_ref[pl.ds(r0, L), h * DV:(h + 1) * DV]
                out_ref[pl.ds(r0, L), h * DV:(h + 1) * DV] = (rr * _sigmoid(rr) * hn).astype(out_ref.dtype)
            else:
                out_ref[pl.ds(r0, L), h * DV:(h + 1) * DV] = o
        return carry

    lax.fori_loop(0, n_chunks, body, 0)


def _gla_pass(qkv, glr, gw, gb, B, S, reverse, of=None, r=None, ng=None):
    T = B * S
    ts = min(SEQ_TILE, S)
    chunk = min(GLA_CHUNK, ts)
    nb = S // ts
    if reverse:
        blk = lambda b, j: (b * nb + nb - 1 - j, 0)
    else:
        blk = lambda b, j: (b * nb + j, 0)
    fix2 = lambda b, j: (0, 0)
    fix3 = lambda b, j: (0, 0, 0)
    in_specs = [pl.BlockSpec((ts, qkv.shape[1]), blk), pl.BlockSpec((ts, 2 * GLA_RANK), blk),
                pl.BlockSpec(gw.shape, fix3), pl.BlockSpec(gb.shape, fix2)]
    args = [qkv, glr, gw, gb]
    if reverse:
        in_specs += [pl.BlockSpec((ts, D_MODEL), blk), pl.BlockSpec((ts, D_MODEL), blk),
                     pl.BlockSpec((1, D_MODEL), fix2)]
        args += [of, r, ng.reshape(1, D_MODEL)]
    kern = functools.partial(_gla_kernel, reverse=reverse, ts=ts, chunk=chunk)
    return pl.pallas_call(
        kern, grid=(B, nb), in_specs=in_specs,
        out_specs=pl.BlockSpec((ts, D_MODEL), blk),
        out_shape=jax.ShapeDtypeStruct((T, D_MODEL), BF16 if reverse else F32),
        scratch_shapes=[pltpu.VMEM((GLA_HEADS, GLA_DV, GLA_DK), F32),
                        pltpu.VMEM((ts, GLA_HEADS * GLA_DK), F32)],
        compiler_params=_params("parallel", "arbitrary"),
        name="gla_bwd" if reverse else "gla_fwd")(*args)


def _gla_layer(x, xb, B, S, w_in, gate_w, gate_b, norm_g, w_out, ln_g, ln_b):
    del xb
    qk = GLA_HEADS * GLA_DK
    w = w_in.astype(BF16)
    w_qkv = w[:, :2 * qk + D_MODEL]
    w_r = w[:, 2 * qk + D_MODEL:2 * qk + 2 * D_MODEL]
    w_glr = w[:, 2 * qk + 2 * D_MODEL:]
    qkv, r, glr = _dense(x, [w_qkv, w_r, w_glr], [BF16, F32, F32])
    gw = gate_w.astype(F32)
    gb = gate_b.astype(F32)
    of = _gla_pass(qkv, glr, gw, gb, B, S, False)
    a = _gla_pass(qkv, glr, gw, gb, B, S, True, of, r, norm_g.astype(F32))
    return _outproj_ln(a, w_out.astype(BF16), x, ln_g, ln_b)


def _gelu_tanh(x):
    return 0.5 * x * (1.0 + jnp.tanh(math.sqrt(2.0 / math.pi) * (x + 0.044715 * (x * x * x))))


def _lru_kernel(*refs, reverse, ts):
    if reverse:
        (u_ref, up_ref, un_ref, cw_ref, cb_ref, wg_ref, bg_ref, lam_ref, hf_ref, gate_ref,
         out_ref, a_ref, g_ref, h_ref, hs_ref) = refs
    else:
        (u_ref, up_ref, un_ref, cw_ref, cb_ref, wg_ref, bg_ref, lam_ref,
         out_ref, a_ref, g_ref, h_ref) = refs
        hs_ref = out_ref
    W = LRU_WIDTH
    j = pl.program_id(1)
    nb = pl.num_programs(1)
    jj = (nb - 1 - j) if reverse else j

    @pl.when(j == 0)
    def _():
        h_ref[...] = jnp.zeros_like(h_ref)

    z = u_ref[...]
    prev = jnp.where(jj > 0, up_ref[...], 0.0)
    nxt = jnp.where(jj < nb - 1, un_ref[...], 0.0)
    row = lax.broadcasted_iota(jnp.int32, (ts, W), 0)
    zm1 = jnp.where(row == 0, prev[7:8, :], pltpu.roll(z, 1, axis=0))
    zm2 = pltpu.roll(z, 2, axis=0)
    zm2 = jnp.where(row == 0, prev[6:7, :], jnp.where(row == 1, prev[7:8, :], zm2))
    zp1 = jnp.where(row == ts - 1, nxt[0:1, :], pltpu.roll(z, ts - 1, axis=0))
    u = cw_ref[0:1, :] * zm2 + cw_ref[1:2, :] * zm1 + cw_ref[2:3, :] * z + cw_ref[3:4, :] * zp1 + cb_ref[...]

    ls = LRU_C * _log_sigmoid(lam_ref[...])
    ub = u.astype(BF16)
    for n in range(LRU_BLOCKS):
        sl = slice(n * LRU_BW, (n + 1) * LRU_BW)
        pre = _dot(ub[:, sl], wg_ref[n]) + bg_ref[n]
        r = _sigmoid(pre[:, :LRU_BW])
        ig = _sigmoid(pre[:, LRU_BW:])
        log_a = r * ls[:, sl]
        a = jnp.exp(log_a)
        a_ref[:, sl] = a
        g_ref[:, sl] = jnp.sqrt(1.0 - a * a) * (ig * u[:, sl])

    n_tiles = ts // SUBLANES

    def body(i, h):
        t = (n_tiles - 1 - i) if reverse else i
        r0 = pl.multiple_of(t * SUBLANES, SUBLANES)
        a8 = a_ref[pl.ds(r0, SUBLANES), :]
        g8 = g_ref[pl.ds(r0, SUBLANES), :]
        out = [None] * SUBLANES
        for k in (range(SUBLANES - 1, -1, -1) if reverse else range(SUBLANES)):
            h = a8[k:k + 1, :] * h + g8[k:k + 1, :]
            out[k] = h
        hs_ref[pl.ds(r0, SUBLANES), :] = jnp.concatenate(out, axis=0)
        return h

    h_ref[...] = lax.fori_loop(0, n_tiles, body, h_ref[...])
    if reverse:
        out_ref[...] = (_gelu_tanh(gate_ref[...]) * (hf_ref[...] + hs_ref[...])).astype(out_ref.dtype)


def _lru_pass(u, cw, cb, wg, bg, lam, B, S, reverse, hf=None, gate=None):
    T = B * S
    W = LRU_WIDTH
    ts = min(SEQ_TILE, S)
    nb = S // ts
    tpb = ts // SUBLANES
    n8 = T // SUBLANES
    if reverse:
        seq = lambda b, j: b * nb + nb - 1 - j
    else:
        seq = lambda b, j: b * nb + j
    blk = lambda b, j: (seq(b, j), 0)
    blk_prev = lambda b, j: (jnp.maximum(seq(b, j) * tpb - 1, 0), 0)
    blk_next = lambda b, j: (jnp.minimum((seq(b, j) + 1) * tpb, n8 - 1), 0)
    fix2 = lambda b, j: (0, 0)
    fix3 = lambda b, j: (0, 0, 0)
    in_specs = [pl.BlockSpec((ts, W), blk), pl.BlockSpec((SUBLANES, W), blk_prev),
                pl.BlockSpec((SUBLANES, W), blk_next), pl.BlockSpec(cw.shape, fix2),
                pl.BlockSpec(cb.shape, fix2), pl.BlockSpec(wg.shape, fix3), pl.BlockSpec(bg.shape, fix3),
                pl.BlockSpec(lam.shape, fix2)]
    args = [u, u, u, cw, cb, wg, bg, lam]
    scratch = [pltpu.VMEM((ts, W), F32), pltpu.VMEM((ts, W), F32), pltpu.VMEM((1, W), F32)]
    if reverse:
        in_specs += [pl.BlockSpec((ts, W), blk), pl.BlockSpec((ts, W), blk)]
        args += [hf, gate]
        scratch.append(pltpu.VMEM((ts, W), F32))
    kern = functools.partial(_lru_kernel, reverse=reverse, ts=ts)
    return pl.pallas_call(
        kern, grid=(B, nb), in_specs=in_specs,
        out_specs=pl.BlockSpec((ts, W), blk),
        out_shape=jax.ShapeDtypeStruct((T, W), BF16 if reverse else F32),
        scratch_shapes=scratch,
        compiler_params=_params("parallel", "arbitrary"),
        name="lru_bwd" if reverse else "lru_fwd")(*args)


def _lru_layer(x, xb, B, S, w_in, conv_w, conv_b, gate_a_w, gate_a_b, gate_x_w, gate_x_b, lam, w_out,
               ln_g, ln_b):
    del xb
    W = LRU_WIDTH
    w = w_in.astype(BF16)
    gate, u = _dense(x, [w[:, :W], w[:, W:]], [F32, F32])
    cw = conv_w.astype(F32)
    cb = conv_b.astype(F32).reshape(1, W)
    passes = []
    for d in range(2):
        wg = jnp.concatenate([gate_a_w[d], gate_x_w[d]], axis=-1).astype(BF16)
        bg = jnp.concatenate([gate_a_b[d].reshape(LRU_BLOCKS, 1, LRU_BW),
                              gate_x_b[d].reshape(LRU_BLOCKS, 1, LRU_BW)], axis=-1).astype(F32)
        passes.append((wg, bg, lam[d].astype(F32).reshape(1, W)))
    hf = _lru_pass(u, cw, cb, *passes[0], B, S, False)
    a = _lru_pass(u, cw, cb, *passes[1], B, S, True, hf, gate)
    return _outproj_ln(a, w_out.astype(BF16), x, ln_g, ln_b)


MLA_HW = MLA_NOPE + LANES


def _mla_proj_kernel(x_ref, pos_ref, win_ref, qg_ref, kg_ref, wqn_ref, wqr_ref, wqs_ref, wkv_ref,
                     fr_ref, sg_ref, q_ref, k_ref, v_ref):
    H = MLA_HEADS
    scale = (MLA_NOPE + MLA_ROPE) ** -0.5
    xb = x_ref[...].astype(BF16)
    z = _dot(xb, win_ref[...])
    cq = z[:, :MLA_Q_RANK]
    ckv = z[:, MLA_Q_RANK:MLA_Q_RANK + MLA_KV_RANK]
    kr = z[:, MLA_Q_RANK + MLA_KV_RANK:MLA_Q_RANK + MLA_KV_RANK + LANES]
    krs = z[:, MLA_Q_RANK + MLA_KV_RANK + LANES:]
    qn = (cq * lax.rsqrt(jnp.mean(cq * cq, axis=-1, keepdims=True) + LN_EPS) * qg_ref[...]).astype(BF16)
    kvn = (ckv * lax.rsqrt(jnp.mean(ckv * ckv, axis=-1, keepdims=True) + LN_EPS) * kg_ref[...]).astype(BF16)
    ang = pos_ref[...].astype(F32) * fr_ref[...]
    cosv = jnp.cos(ang)
    lane = lax.broadcasted_iota(jnp.int32, ang.shape, 1)
    cosv = jnp.where(lane < MLA_ROPE, cosv, 0.0)
    sinv = jnp.sin(ang) * sg_ref[...]
    k_rope = kr * cosv + krs * sinv
    kv = _dot(kvn, wkv_ref[...])
    q_nope = _dot(qn, wqn_ref[...])
    q_rope = _dot(qn, wqr_ref[...])
    q_swap = _dot(qn, wqs_ref[...])
    for h in range(H):
        a0 = h * MLA_HW
        q_ref[:, a0:a0 + MLA_NOPE] = (q_nope[:, h * MLA_NOPE:(h + 1) * MLA_NOPE] * scale).astype(BF16)
        qr = q_rope[:, h * LANES:(h + 1) * LANES] * cosv + q_swap[:, h * LANES:(h + 1) * LANES] * sinv
        q_ref[:, a0 + MLA_NOPE:a0 + MLA_HW] = (qr * scale).astype(BF16)
        k_ref[:, a0:a0 + MLA_NOPE] = kv[:, h * 2 * MLA_NOPE:h * 2 * MLA_NOPE + MLA_NOPE].astype(BF16)
        k_ref[:, a0 + MLA_NOPE:a0 + MLA_HW] = k_rope.astype(BF16)
        v_ref[:, h * MLA_DV:(h + 1) * MLA_DV] = kv[:, h * 2 * MLA_NOPE + MLA_NOPE:(h + 1) * 2 * MLA_NOPE].astype(BF16)


def _attn_kernel(q_ref, k_ref, v_ref, o_ref):
    s = _dot_nt(q_ref[...], k_ref[...])
    m = jnp.max(s, axis=-1, keepdims=True)
    p = jnp.exp(s - m)
    l = jnp.sum(p, axis=-1, keepdims=True)
    o_ref[...] = (_dot(p.astype(BF16), v_ref[...]) / l).astype(o_ref.dtype)


def _pad_rope_cols(w, swap):
    half = MLA_ROPE // 2
    if swap:
        w = jnp.concatenate([w[..., half:], w[..., :half]], axis=-1)
    w = jnp.concatenate([w, jnp.zeros_like(w)], axis=-1)
    return w.reshape(w.shape[0], -1)


def _mla_layer(x, xb, B, S, positions, w_in, q_norm_g, kv_norm_g, w_uq, w_ukv, w_out, ln_g, ln_b):
    del xb
    T = B * S
    H = MLA_HEADS
    half = MLA_ROPE // 2
    w_kr = w_in[:, MLA_Q_RANK + MLA_KV_RANK:].reshape(D_MODEL, 1, MLA_ROPE)
    win = jnp.concatenate([w_in[:, :MLA_Q_RANK + MLA_KV_RANK], _pad_rope_cols(w_kr, False),
                           _pad_rope_cols(w_kr, True)], axis=1).astype(BF16)
    wq = w_uq.reshape(MLA_Q_RANK, H, MLA_NOPE + MLA_ROPE)
    wqn = wq[:, :, :MLA_NOPE].reshape(MLA_Q_RANK, H * MLA_NOPE).astype(BF16)
    wqr = _pad_rope_cols(wq[:, :, MLA_NOPE:], False).astype(BF16)
    wqs = _pad_rope_cols(wq[:, :, MLA_NOPE:], True).astype(BF16)
    freq = ROPE_THETA ** (-jnp.arange(half, dtype=F32) / half)
    zeros = jnp.zeros((LANES - MLA_ROPE,), F32)
    fr = jnp.concatenate([freq, freq, zeros]).reshape(1, LANES)
    sg = jnp.concatenate([-jnp.ones((half,), F32), jnp.ones((half,), F32), zeros]).reshape(1, LANES)
    tm = min(ROW_TILE, T)
    row = lambda i: (i, 0)
    fix = lambda i: (0, 0)
    ins = [x, positions.reshape(T, 1), win, q_norm_g.astype(F32).reshape(1, -1),
           kv_norm_g.astype(F32).reshape(1, -1), wqn, wqr, wqs, w_ukv.astype(BF16), fr, sg]
    in_specs = [pl.BlockSpec((tm, D_MODEL), row), pl.BlockSpec((tm, 1), row)]
    in_specs += [pl.BlockSpec(a.shape, fix) for a in ins[2:]]
    q, k, v = pl.pallas_call(
        _mla_proj_kernel, grid=(T // tm,), in_specs=in_specs,
        out_specs=[pl.BlockSpec((tm, H * MLA_HW), row), pl.BlockSpec((tm, H * MLA_HW), row),
                   pl.BlockSpec((tm, H * MLA_DV), row)],
        out_shape=[jax.ShapeDtypeStruct((T, H * MLA_HW), BF16), jax.ShapeDtypeStruct((T, H * MLA_HW), BF16),
                   jax.ShapeDtypeStruct((T, H * MLA_DV), BF16)],
        compiler_params=_params("parallel"), name="mla_proj")(*ins)
    tq = min(Q_TILE, S)
    nq = S // tq
    att = pl.pallas_call(
        _attn_kernel, grid=(B, H, nq),
        in_specs=[pl.BlockSpec((tq, MLA_HW), lambda b, h, i: (b * nq + i, h)),
                  pl.BlockSpec((S, MLA_HW), lambda b, h, i: (b, h)),
                  pl.BlockSpec((S, MLA_DV), lambda b, h, i: (b, h))],
        out_specs=pl.BlockSpec((tq, MLA_DV), lambda b, h, i: (b * nq + i, h)),
        out_shape=jax.ShapeDtypeStruct((T, H * MLA_DV), BF16),
        compiler_params=_params("parallel", "parallel", "arbitrary"), name="mla_attn")(q, k, v)
    return _outproj_ln(att, w_out.astype(BF16), x, ln_g, ln_b)


def _router_kernel(x_ref, wh_ref, wl_ref, pos_ref, gate_ref, *, cap):
    E = N_EXPERTS
    x = x_ref[...]
    S = x.shape[0]
    xh = x.astype(BF16)
    xl = (x - xh.astype(F32)).astype(BF16)
    wh = wh_ref[...]
    logits = _dot_nt(wh, xh) + (_dot_nt(wl_ref[...], xh) + _dot_nt(wh, xl))
    mx = jnp.max(logits, axis=0, keepdims=True)
    ex = jnp.exp(logits - mx)
    aff = ex / jnp.sum(ex, axis=0, keepdims=True)
    bits = pltpu.bitcast(aff, jnp.int32)

    def bit_step(i, thr):
        cand = thr | jnp.left_shift(jnp.int32(1), 30 - i)
        cnt = jnp.sum(jnp.where(bits >= cand, 1.0, 0.0), axis=1, keepdims=True)
        return jnp.where(cnt >= cap, cand, thr)

    thr = lax.fori_loop(0, 31, bit_step, jnp.zeros((E, 1), jnp.int32))
    gt = bits > thr
    eq = bits == thr
    need = cap - jnp.sum(jnp.where(gt, 1.0, 0.0), axis=1, keepdims=True)
    r = lax.broadcasted_iota(jnp.int32, (LANES, LANES), 0)
    c = lax.broadcasted_iota(jnp.int32, (LANES, LANES), 1)
    upper = jnp.where(r < c, 1.0, 0.0).astype(BF16)
    off = jnp.zeros((2 * E, 1), F32)
    for blk in range(S // LANES):
        sl = slice(blk * LANES, (blk + 1) * LANES)
        ind = jnp.concatenate([jnp.where(gt[:, sl], 1.0, 0.0), jnp.where(eq[:, sl], 1.0, 0.0)], axis=0)
        pre = _dot(ind.astype(BF16), upper) + off
        off = off + jnp.sum(ind, axis=1, keepdims=True)
        pg, pe = pre[:E], pre[E:]
        sel = gt[:, sl] | (eq[:, sl] & (pe < need))
        slot = pg + jnp.minimum(pe, need)
        pos_ref[:, sl] = jnp.where(sel, slot, -1.0).astype(jnp.int32)
        gate_ref[:, sl] = jnp.where(sel, aff[:, sl], 0.0)


def _ffn_kernel(xb_ref, pos_ref, gate_ref, wg_ref, wu_ref, wd_ref, y_ref, *, cap):
    S = xb_ref.shape[0]

    @pl.when(pl.program_id(1) == 0)
    def _():
        y_ref[...] = jnp.zeros_like(y_ref)

    slot = lax.broadcasted_iota(jnp.int32, (cap, S), 0)
    hit = slot == pos_ref[0]
    onehot = jnp.where(hit, 1.0, 0.0).astype(BF16)
    gate = jnp.sum(jnp.where(hit, gate_ref[0], 0.0), axis=1, keepdims=True)
    xin = _dot(onehot, xb_ref[...]).astype(BF16)
    hg = _dot(xin, wg_ref[0])
    hu = _dot(xin, wu_ref[0])
    hmid = (hg * _sigmoid(hg) * hu).astype(BF16)
    out = (_dot(hmid, wd_ref[0]) * gate).astype(BF16)
    y_ref[...] += _dot_tn(onehot, out)


def _moe_layer(x, xb, B, S, w_router, w_gate, w_up, w_down, ln_g, ln_b):
    T = B * S
    E = N_EXPERTS
    D = D_MODEL
    cap = CAPACITY_FACTOR * S // E
    wr = w_router.astype(F32).T
    wh = wr.astype(BF16)
    wl = (wr - wh.astype(F32)).astype(BF16)
    pos, gate = pl.pallas_call(
        functools.partial(_router_kernel, cap=cap), grid=(B,),
        in_specs=[pl.BlockSpec((S, D), lambda b: (b, 0)), pl.BlockSpec((E, D), lambda b: (0, 0)),
                  pl.BlockSpec((E, D), lambda b: (0, 0))],
        out_specs=[pl.BlockSpec((E, S), lambda b: (b, 0)), pl.BlockSpec((E, S), lambda b: (b, 0))],
        out_shape=[jax.ShapeDtypeStruct((B * E, S), jnp.int32), jax.ShapeDtypeStruct((B * E, S), F32)],
        compiler_params=_params("parallel"), name="moe_router")(x, wh, wl)
    pos = pos.reshape(B * E, 1, S)
    gate = gate.reshape(B * E, 1, S)
    ff = w_gate.shape[-1]
    y = pl.pallas_call(
        functools.partial(_ffn_kernel, cap=cap), grid=(B, E),
        in_specs=[pl.BlockSpec((S, D), lambda b, e: (b, 0)),
                  pl.BlockSpec((1, 1, S), lambda b, e: (b * E + e, 0, 0)),
                  pl.BlockSpec((1, 1, S), lambda b, e: (b * E + e, 0, 0)),
                  pl.BlockSpec((1, D, ff), lambda b, e: (e, 0, 0)),
                  pl.BlockSpec((1, D, ff), lambda b, e: (e, 0, 0)),
                  pl.BlockSpec((1, ff, D), lambda b, e: (e, 0, 0))],
        out_specs=pl.BlockSpec((S, D), lambda b, e: (b, 0)),
        out_shape=jax.ShapeDtypeStruct((T, D), F32),
        compiler_params=_params("parallel", "arbitrary"), name="moe_ffn")(
            xb, pos, gate, w_gate.astype(BF16), w_up.astype(BF16), w_down.astype(BF16))
    return _residual_ln(x, y, ln_g, ln_b)


def kernel(x, positions, mlstm_w_in, mlstm_gate_b, mlstm_norm_g, mlstm_w_out, gla_w_in, gla_gate_w, gla_gate_b, gla_norm_g, gla_w_out, lru_w_in, lru_conv_w, lru_conv_b, lru_gate_a_w, lru_gate_a_b, lru_gate_x_w, lru_gate_x_b, lru_lambda, lru_w_out, mla_w_in, mla_q_norm_g, mla_kv_norm_g, mla_w_uq, mla_w_ukv, mla_w_out, moe_router, moe_w_gate, moe_w_up, moe_w_down, ln_g, ln_b):
    B, S, D = x.shape
    xf = x.reshape(B * S, D)
    xb = None
    for i in range(DEPTH):
        m = i % N_MIXERS
        j = i // N_MIXERS
        g0, b0 = ln_g[i, 0], ln_b[i, 0]
        if m == 0:
            xf, xb = _mlstm_layer(xf, xb, B, S, mlstm_w_in[j], mlstm_gate_b[j], mlstm_norm_g[j],
                                  mlstm_w_out[j], g0, b0)
        elif m == 1:
            xf, xb = _gla_layer(xf, xb, B, S, gla_w_in[j], gla_gate_w[j], gla_gate_b[j], gla_norm_g[j],
                                gla_w_out[j], g0, b0)
        elif m == 2:
            xf, xb = _lru_layer(xf, xb, B, S, lru_w_in[j], lru_conv_w[j], lru_conv_b[j], lru_gate_a_w[j],
                                lru_gate_a_b[j], lru_gate_x_w[j], lru_gate_x_b[j], lru_lambda[j],
                                lru_w_out[j], g0, b0)
        else:
            xf, xb = _mla_layer(xf, xb, B, S, positions, mla_w_in[j], mla_q_norm_g[j], mla_kv_norm_g[j],
                                mla_w_uq[j], mla_w_ukv[j], mla_w_out[j], g0, b0)
        xf = _moe_layer(xf, xb, B, S, moe_router[i], moe_w_gate[i], moe_w_up[i], moe_w_down[i],
                        ln_g[i, 1], ln_b[i, 1])
    return xf.reshape(B, S, D)
```

```python
import functools
import math

import jax
import jax.numpy as jnp
from jax import lax
from jax.experimental import pallas as pl
from jax.experimental.pallas import tpu as pltpu

F32 = jnp.float32
BF16 = jnp.bfloat16
HIGHEST = lax.Precision.HIGHEST

D_MODEL = 1024
DEPTH = 4
N_MIXERS = 4
ALPHA = (2 * DEPTH) ** 0.25
LN_EPS = 1e-5

ML_HEADS = 4
ML_DV = D_MODEL // ML_HEADS
ML_DK = ML_DV // 2

GLA_HEADS = 4
GLA_DK = D_MODEL // 2 // GLA_HEADS
GLA_DV = D_MODEL // GLA_HEADS
GLA_RANK = 16
GLA_TAU = 16.0

LRU_WIDTH = D_MODEL
LRU_BLOCKS = 4
LRU_BW = LRU_WIDTH // LRU_BLOCKS
CONV_WIDTH = 4
LRU_C = 8.0

MLA_HEADS = 8
MLA_NOPE = 128
MLA_ROPE = 64
MLA_DV = 128
MLA_Q_RANK = 384
MLA_KV_RANK = 256
ROPE_THETA = 10000.0

N_EXPERTS = 16
CAPACITY_FACTOR = 2

V7X_VMEM_BYTES = 64 * 1024 * 1024
VMEM_LIMIT = V7X_VMEM_BYTES - 8 * 1024 * 1024
LANES = 128
SUBLANES = 8

ROW_TILE = 512
SEQ_TILE = 512
ML_CHUNK = 512
GLA_CHUNK = 64
Q_TILE = 2048
Q_SUB = 256
N_CHUNK = 512


def _params(*sem):
    return pltpu.CompilerParams(dimension_semantics=sem, vmem_limit_bytes=VMEM_LIMIT)


def _log_sigmoid(x):
    return jnp.minimum(x, 0.0) - jnp.log(1.0 + jnp.exp(-jnp.abs(x)))


def _sigmoid(x):
    return 0.5 * jnp.tanh(0.5 * x) + 0.5


def _layer_norm(v, g, b):
    mu = jnp.mean(v, axis=-1, keepdims=True)
    d = v - mu
    var = jnp.mean(d * d, axis=-1, keepdims=True)
    return d * lax.rsqrt(var + LN_EPS) * g + b


def _dot(a, b):
    return jnp.dot(a, b, preferred_element_type=F32)


def _dot_nt(a, b):
    return lax.dot_general(a, b, (((1,), (1,)), ((), ())), preferred_element_type=F32)


def _dot_tn(a, b):
    return lax.dot_general(a, b, (((0,), (0,)), ((), ())), preferred_element_type=F32)


def _dense_kernel(*refs, n_w, n_t, has_bias):
    x_ref = refs[0]
    pos = 1
    w_refs = refs[pos:pos + n_w]
    pos += n_w
    b_refs = []
    for hb in has_bias:
        if hb:
            b_refs.append(refs[pos])
            pos += 1
        else:
            b_refs.append(None)
    t_refs = refs[pos:pos + 2 * n_t]
    pos += 2 * n_t
    o_refs = refs[pos:pos + n_w]
    pos += n_w
    ot_refs = refs[pos:pos + n_t]

    xb = x_ref[...].astype(BF16)
    for w_ref, b_ref, o_ref in zip(w_refs, b_refs, o_refs):
        n = w_ref.shape[1]
        for j0 in range(0, n, N_CHUNK):
            j1 = min(n, j0 + N_CHUNK)
            acc = _dot(xb, w_ref[:, j0:j1])
            if b_ref is not None:
                acc = acc + b_ref[:, j0:j1]
            o_ref[:, j0:j1] = acc.astype(o_ref.dtype)
    for i in range(n_t):
        wt_ref, bt_ref = t_refs[2 * i], t_refs[2 * i + 1]
        ot_refs[i][...] = _dot_nt(wt_ref[...], xb) + bt_ref[...]


def _dense(x, ws, dtypes, biases=None, transposed=()):
    T, K = x.shape
    tm = min(ROW_TILE, T)
    if biases is None:
        biases = [None] * len(ws)
    has_bias = tuple(b is not None for b in biases)
    args = [x] + list(ws) + [b for b in biases if b is not None]
    in_specs = [pl.BlockSpec((tm, K), lambda i: (i, 0))]
    in_specs += [pl.BlockSpec(w.shape, lambda i: (0, 0)) for w in ws]
    in_specs += [pl.BlockSpec(b.shape, lambda i: (0, 0)) for b in biases if b is not None]
    for wt, bt in transposed:
        args += [wt, bt]
        in_specs += [pl.BlockSpec(wt.shape, lambda i: (0, 0)), pl.BlockSpec(bt.shape, lambda i: (0, 0))]
    out_shape = [jax.ShapeDtypeStruct((T, w.shape[1]), dt) for w, dt in zip(ws, dtypes)]
    out_specs = [pl.BlockSpec((tm, w.shape[1]), lambda i: (i, 0)) for w in ws]
    for wt, _ in transposed:
        out_shape.append(jax.ShapeDtypeStruct((wt.shape[0], T), F32))
        out_specs.append(pl.BlockSpec((wt.shape[0], tm), lambda i: (0, i)))
    kern = functools.partial(_dense_kernel, n_w=len(ws), n_t=len(transposed), has_bias=has_bias)
    return pl.pallas_call(
        kern, grid=(T // tm,), in_specs=in_specs, out_specs=out_specs, out_shape=out_shape,
        compiler_params=_params("parallel"), name="dense")(*args)


def _outproj_ln_kernel(a_ref, w_ref, x_ref, g_ref, b_ref, o_ref, ob_ref):
    y = _dot(a_ref[...], w_ref[...])
    v = _layer_norm(ALPHA * x_ref[...] + y, g_ref[...], b_ref[...])
    o_ref[...] = v
    half = v.shape[1] // 2
    bits = pltpu.bitcast(v.astype(BF16).astype(F32), jnp.uint32)
    ob_ref[...] = (bits[:, :half] >> 16) | bits[:, half:]


def _outproj_ln(a, w, x, g, b):
    T, K = a.shape
    D = w.shape[1]
    tm = min(ROW_TILE, T)
    row = lambda i: (i, 0)
    fix = lambda i: (0, 0)
    return pl.pallas_call(
        _outproj_ln_kernel, grid=(T // tm,),
        in_specs=[pl.BlockSpec((tm, K), row), pl.BlockSpec((K, D), fix), pl.BlockSpec((tm, D), row),
                  pl.BlockSpec((1, D), fix), pl.BlockSpec((1, D), fix)],
        out_specs=[pl.BlockSpec((tm, D), row), pl.BlockSpec((tm, D // 2), row)],
        out_shape=[jax.ShapeDtypeStruct((T, D), F32), jax.ShapeDtypeStruct((T, D // 2), jnp.uint32)],
        compiler_params=_params("parallel"), name="outproj_ln")(a, w, x, g.reshape(1, D), b.reshape(1, D))


def _residual_ln_kernel(x_ref, y_ref, g_ref, b_ref, o_ref):
    o_ref[...] = _layer_norm(ALPHA * x_ref[...] + y_ref[...], g_ref[...], b_ref[...])


def _residual_ln(x, y, g, b):
    T, D = x.shape
    tm = min(ROW_TILE, T)
    row = lambda i: (i, 0)
    fix = lambda i: (0, 0)
    return pl.pallas_call(
        _residual_ln_kernel, grid=(T // tm,),
        in_specs=[pl.BlockSpec((tm, D), row), pl.BlockSpec((tm, D), row),
                  pl.BlockSpec((1, D), fix), pl.BlockSpec((1, D), fix)],
        out_specs=pl.BlockSpec((tm, D), row),
        out_shape=jax.ShapeDtypeStruct((T, D), F32),
        compiler_params=_params("parallel"), name="residual_ln")(x, y, g.reshape(1, D), b.reshape(1, D))


def _head_norm(h, n_heads, dv):
    outs = []
    for hh in range(n_heads):
        v = h[:, hh * dv:(hh + 1) * dv]
        mu = jnp.mean(v, axis=-1, keepdims=True)
        d = v - mu
        var = jnp.mean(d * d, axis=-1, keepdims=True)
        outs.append(d * lax.rsqrt(var + LN_EPS))
    return jnp.concatenate(outs, axis=1)


def _mlstm_kernel(*refs, reverse, ts, chunk):
    if reverse:
        qkv_ref, gr_ref, hf_ref, o_ref, ng_ref, out_ref, c_ref, n_ref, m_ref = refs
    else:
        qkv_ref, gr_ref, out_ref, c_ref, n_ref, m_ref = refs
    H, DK, DV, L = ML_HEADS, ML_DK, ML_DV, chunk
    scale = DK ** -0.5

    @pl.when(pl.program_id(1) == 0)
    def _():
        c_ref[...] = jnp.zeros_like(c_ref)
        n_ref[...] = jnp.zeros_like(n_ref)
        m_ref[...] = jnp.zeros_like(m_ref)

    rows = lax.broadcasted_iota(jnp.int32, (L, L), 0)
    cols = lax.broadcasted_iota(jnp.int32, (L, L), 1)
    mask = (cols >= rows) if reverse else (cols <= rows)
    eye = rows == cols
    tri = jnp.where((rows >= cols) if reverse else (rows <= cols), 1.0, 0.0).astype(F32)
    d0 = 8 if reverse else 0
    last = 0 if reverse else L - 1
    n_chunks = ts // L
    order = range(n_chunks - 1, -1, -1) if reverse else range(n_chunks)
    neg_inf = -jnp.inf

    for c in order:
        r0 = c * L
        g8 = gr_ref[d0:d0 + 8, r0:r0 + L]
        lf8 = _log_sigmoid(g8)
        b8 = jnp.dot(lf8, tri, precision=HIGHEST, preferred_element_type=F32)
        u8 = g8[0:4, :] - b8[4:8, :]
        for h in range(H):
            qb = qkv_ref[r0:r0 + L, h * DK:(h + 1) * DK]
            kb = qkv_ref[r0:r0 + L, H * DK + h * DK:H * DK + (h + 1) * DK]
            vb = qkv_ref[r0:r0 + L, 2 * H * DK + h * DV:2 * H * DK + (h + 1) * DV]
            u_r = u8[h:h + 1, :]
            b_r = b8[4 + h:5 + h, :]
            m_prev = m_ref[h:h + 1, 0:1]
            u_c = jnp.sum(jnp.where(eye, u_r, 0.0), axis=1, keepdims=True)
            b_c = jnp.sum(jnp.where(eye, b_r, 0.0), axis=1, keepdims=True)
            um = jnp.where(mask, u_r, neg_inf)
            a_c = jnp.maximum(m_prev, jnp.max(um, axis=1, keepdims=True))
            dmat = jnp.exp(um - a_c)
            s = _dot_nt(qb, kb) * (scale * dmat)
            w_int = jnp.exp(m_prev - a_c) * scale
            c_old = c_ref[h]
            num = _dot(s.astype(BF16), vb) + w_int * _dot(qb, c_old.astype(BF16))
            qn = jnp.sum(qb.astype(F32) * n_ref[h], axis=1, keepdims=True)
            den = jnp.sum(s, axis=1, keepdims=True) + w_int * qn
            hh = num / jnp.maximum(jnp.abs(den), jnp.exp(-(a_c + b_c)))
            a_last = jnp.maximum(m_prev, jnp.max(u_r, axis=1, keepdims=True))
            g_tot = b_r[:, last:last + 1]
            ws_c = jnp.exp(u_c - a_last)
            wc = jnp.exp(m_prev - a_last)
            kw = kb.astype(F32) * ws_c
            c_ref[h] = wc * c_old + _dot_tn(kw.astype(BF16), vb)
            n_ref[h] = wc * n_ref[h] + jnp.sum(kw, axis=0, keepdims=True)
            m_ref[h:h + 1, :] = jnp.broadcast_to(g_tot + a_last, (1, LANES))
            if reverse:
                hs = hf_ref[r0:r0 + L, h * DV:(h + 1) * DV] + hh
                mu = jnp.mean(hs, axis=-1, keepdims=True)
                dd = hs - mu
                var = jnp.mean(dd * dd, axis=-1, keepdims=True)
                hn = dd * lax.rsqrt(var + LN_EPS) * ng_ref[:, h * DV:(h + 1) * DV]
                og = _sigmoid(o_ref[r0:r0 + L, h * DV:(h + 1) * DV])
                out_ref[r0:r0 + L, h * DV:(h + 1) * DV] = (og * hn).astype(out_ref.dtype)
            else:
                out_ref[r0:r0 + L, h * DV:(h + 1) * DV] = hh


def _mlstm_pass(qkv, gr, B, S, reverse, hf=None, o=None, ng=None):
    T = B * S
    ts = min(SEQ_TILE, S)
    chunk = min(ML_CHUNK, ts)
    nb = S // ts
    if reverse:
        blk = lambda b, j: (b * nb + nb - 1 - j, 0)
        blk_t = lambda b, j: (0, b * nb + nb - 1 - j)
    else:
        blk = lambda b, j: (b * nb + j, 0)
        blk_t = lambda b, j: (0, b * nb + j)
    wq = qkv.shape[1]
    in_specs = [pl.BlockSpec((ts, wq), blk), pl.BlockSpec((16, ts), blk_t)]
    args = [qkv, gr]
    if reverse:
        in_specs += [pl.BlockSpec((ts, D_MODEL), blk), pl.BlockSpec((ts, D_MODEL), blk),
                     pl.BlockSpec((1, D_MODEL), lambda b, j: (0, 0))]
        args += [hf, o, ng.reshape(1, D_MODEL)]
    out_dtype = BF16 if reverse else F32
    kern = functools.partial(_mlstm_kernel, reverse=reverse, ts=ts, chunk=chunk)
    return pl.pallas_call(
        kern, grid=(B, nb), in_specs=in_specs,
        out_specs=pl.BlockSpec((ts, D_MODEL), blk),
        out_shape=jax.ShapeDtypeStruct((T, D_MODEL), out_dtype),
        scratch_shapes=[pltpu.VMEM((ML_HEADS, ML_DK, ML_DV), F32),
                        pltpu.VMEM((ML_HEADS, 1, ML_DK), F32),
                        pltpu.VMEM((SUBLANES, LANES), F32)],
        compiler_params=_params("parallel", "arbitrary"),
        name="mlstm_bwd" if reverse else "mlstm_fwd")(*args)


def _mlstm_layer(x, xb, B, S, w_in, gate_b, norm_g, w_out, ln_g, ln_b):
    del xb
    qk = ML_HEADS * ML_DK
    w = w_in.astype(BF16)
    w_qkv = w[:, :2 * qk + D_MODEL]
    w_o = w[:, 2 * qk + D_MODEL:2 * qk + 2 * D_MODEL]
    w_g_t = w[:, 2 * qk + 2 * D_MODEL:].T
    b_g = gate_b.astype(F32).reshape(16, 1)
    qkv, o, gr = _dense(x, [w_qkv, w_o], [BF16, F32], transposed=[(w_g_t, b_g)])
    hf = _mlstm_pass(qkv, gr, B, S, False)
    a = _mlstm_pass(qkv, gr, B, S, True, hf, o, norm_g.astype(F32))
    return _outproj_ln(a, w_out.astype(BF16), x, ln_g, ln_b)


def _cumsum_rows(x, n, reverse):
    row = lax.broadcasted_iota(jnp.int32, x.shape, 0)
    sh = 1
    while sh < n:
        if reverse:
            x = x + jnp.where(row < n - sh, pltpu.roll(x, n - sh, axis=0), 0.0)
        else:
            x = x + jnp.where(row >= sh, pltpu.roll(x, sh, axis=0), 0.0)
        sh *= 2
    return x


def _gla_kernel(*refs, reverse, ts, chunk):
    if reverse:
        qkv_ref, glr_ref, gw_ref, gb_ref, of_ref, r_ref, ng_ref, out_ref, st_ref, la_ref = refs
    else:
        qkv_ref, glr_ref, gw_ref, gb_ref, out_ref, st_ref, la_ref = refs
    H, DK, DV, L = GLA_HEADS, GLA_DK, GLA_DV, chunk
    scale = DK ** -0.5
    d = 1 if reverse else 0

    @pl.when(pl.program_id(1) == 0)
    def _():
        st_ref[...] = jnp.zeros_like(st_ref)

    glr = glr_ref[:, d * GLA_RANK:(d + 1) * GLA_RANK]
    pre = jnp.dot(glr, gw_ref[d], precision=HIGHEST, preferred_element_type=F32) + gb_ref[d:d + 1, :]
    la_ref[...] = _log_sigmoid(pre) * (1.0 / GLA_TAU)

    rows = lax.broadcasted_iota(jnp.int32, (L, L), 0)
    cols = lax.broadcasted_iota(jnp.int32, (L, L), 1)
    mask = (cols >= rows) if reverse else (cols <= rows)
    last = 0 if reverse else L - 1
    mid = L // 2
    n_chunks = ts // L

    def body(i, carry):
        c = (n_chunks - 1 - i) if reverse else i
        r0 = pl.multiple_of(c * L, L)
        bsum = _cumsum_rows(la_ref[pl.ds(r0, L), :], L, reverse)
        for h in range(H):
            b = bsum[:, h * DK:(h + 1) * DK]
            qf = qkv_ref[pl.ds(r0, L), h * DK:(h + 1) * DK].astype(F32)
            kf = qkv_ref[pl.ds(r0, L), H * DK + h * DK:H * DK + (h + 1) * DK].astype(F32)
            vb = qkv_ref[pl.ds(r0, L), 2 * H * DK + h * DV:2 * H * DK + (h + 1) * DV]
            beta = b[mid:mid + 1, :]
            g = b[last:last + 1, :]
            qt = (qf * jnp.exp(b - beta)).astype(BF16)
            kt = (kf * jnp.exp(beta - b)).astype(BF16)
            amat = jnp.where(mask, _dot_nt(qt, kt) * scale, 0.0)
            qh = (qf * (jnp.exp(b) * scale)).astype(BF16)
            st = st_ref[h]
            o = _dot(amat.astype(BF16), vb) + _dot_nt(qh, st.astype(BF16))
            kh = (kf * jnp.exp(g - b)).astype(BF16)
            st_ref[h] = st * jnp.exp(g) + _dot_tn(vb, kh)
            if reverse:
                hs = of_ref[pl.ds(r0, L), h * DV:(h + 1) * DV] + o
                mu = jnp.mean(hs, axis=-1, keepdims=True)
                dd = hs - mu
                var = jnp.mean(dd * dd, axis=-1, keepdims=True)
                hn = dd * lax.rsqrt(var + LN_EPS) * ng_ref[:, h * DV:(h + 1) * DV]
                rr = r_ref[pl.ds(r0, L), h * DV:(h + 1) * DV]
                out_ref[pl.ds(r0, L), h * DV:(h + 1) * DV] = (rr * _sigmoid(rr) * hn).astype(out_ref.dtype)
            else:
                out_ref[pl.ds(r0, L), h * DV:(h + 1) * DV] = o
        return carry

    lax.fori_loop(0, n_chunks, body, 0)


def _gla_pass(qkv, glr, gw, gb, B, S, reverse, of=None, r=None, ng=None):
    T = B * S
    ts = min(SEQ_TILE, S)
    chunk = min(GLA_CHUNK, ts)
    nb = S // ts
    if reverse:
        blk = lambda b, j: (b * nb + nb - 1 - j, 0)
    else:
        blk = lambda b, j: (b * nb + j, 0)
    fix2 = lambda b, j: (0, 0)
    fix3 = lambda b, j: (0, 0, 0)
    in_specs = [pl.BlockSpec((ts, qkv.shape[1]), blk), pl.BlockSpec((ts, 2 * GLA_RANK), blk),
                pl.BlockSpec(gw.shape, fix3), pl.BlockSpec(gb.shape, fix2)]
    args = [qkv, glr, gw, gb]
    if reverse:
        in_specs += [pl.BlockSpec((ts, D_MODEL), blk), pl.BlockSpec((ts, D_MODEL), blk),
                     pl.BlockSpec((1, D_MODEL), fix2)]
        args += [of, r, ng.reshape(1, D_MODEL)]
    kern = functools.partial(_gla_kernel, reverse=reverse, ts=ts, chunk=chunk)
    return pl.pallas_call(
        kern, grid=(B, nb), in_specs=in_specs,
        out_specs=pl.BlockSpec((ts, D_MODEL), blk),
        out_shape=jax.ShapeDtypeStruct((T, D_MODEL), BF16 if reverse else F32),
        scratch_shapes=[pltpu.VMEM((GLA_HEADS, GLA_DV, GLA_DK), F32),
                        pltpu.VMEM((ts, GLA_HEADS * GLA_DK), F32)],
        compiler_params=_params("parallel", "arbitrary"),
        name="gla_bwd" if reverse else "gla_fwd")(*args)


def _gla_layer(x, xb, B, S, w_in, gate_w, gate_b, norm_g, w_out, ln_g, ln_b):
    del xb
    qk = GLA_HEADS * GLA_DK
    w = w_in.astype(BF16)
    w_qkv = w[:, :2 * qk + D_MODEL]
    w_r = w[:, 2 * qk + D_MODEL:2 * qk + 2 * D_MODEL]
    w_glr = w[:, 2 * qk + 2 * D_MODEL:]
    qkv, r, glr = _dense(x, [w_qkv, w_r, w_glr], [BF16, F32, F32])
    gw = gate_w.astype(F32)
    gb = gate_b.astype(F32)
    of = _gla_pass(qkv, glr, gw, gb, B, S, False)
    a = _gla_pass(qkv, glr, gw, gb, B, S, True, of, r, norm_g.astype(F32))
    return _outproj_ln(a, w_out.astype(BF16), x, ln_g, ln_b)


def _gelu_tanh(x):
    return 0.5 * x * (1.0 + jnp.tanh(math.sqrt(2.0 / math.pi) * (x + 0.044715 * (x * x * x))))


def _lru_kernel(*refs, reverse, ts):
    if reverse:
        (u_ref, up_ref, un_ref, cw_ref, cb_ref, wg_ref, bg_ref, lam_ref, hf_ref, gate_ref,
         out_ref, a_ref, g_ref, h_ref, hs_ref) = refs
    else:
        (u_ref, up_ref, un_ref, cw_ref, cb_ref, wg_ref, bg_ref, lam_ref,
         out_ref, a_ref, g_ref, h_ref) = refs
        hs_ref = out_ref
    W = LRU_WIDTH
    j = pl.program_id(1)
    nb = pl.num_programs(1)
    jj = (nb - 1 - j) if reverse else j

    @pl.when(j == 0)
    def _():
        h_ref[...] = jnp.zeros_like(h_ref)

    z = u_ref[...]
    prev = jnp.where(jj > 0, up_ref[...], 0.0)
    nxt = jnp.where(jj < nb - 1, un_ref[...], 0.0)
    row = lax.broadcasted_iota(jnp.int32, (ts, W), 0)
    zm1 = jnp.where(row == 0, prev[7:8, :], pltpu.roll(z, 1, axis=0))
    zm2 = pltpu.roll(z, 2, axis=0)
    zm2 = jnp.where(row == 0, prev[6:7, :], jnp.where(row == 1, prev[7:8, :], zm2))
    zp1 = jnp.where(row == ts - 1, nxt[0:1, :], pltpu.roll(z, ts - 1, axis=0))
    u = cw_ref[0:1, :] * zm2 + cw_ref[1:2, :] * zm1 + cw_ref[2:3, :] * z + cw_ref[3:4, :] * zp1 + cb_ref[...]

    ls = LRU_C * _log_sigmoid(lam_ref[...])
    ub = u.astype(BF16)
    for n in range(LRU_BLOCKS):
        sl = slice(n * LRU_BW, (n + 1) * LRU_BW)
        pre = _dot(ub[:, sl], wg_ref[n]) + bg_ref[n]
        r = _sigmoid(pre[:, :LRU_BW])
        ig = _sigmoid(pre[:, LRU_BW:])
        log_a = r * ls[:, sl]
        a = jnp.exp(log_a)
        a_ref[:, sl] = a
        g_ref[:, sl] = jnp.sqrt(1.0 - a * a) * (ig * u[:, sl])

    n_tiles = ts // SUBLANES

    srow = lax.broadcasted_iota(jnp.int32, (SUBLANES, W), 0)
    carry_row = 0 if reverse else SUBLANES - 1

    def body(i, h):
        t = (n_tiles - 1 - i) if reverse else i
        r0 = pl.multiple_of(t * SUBLANES, SUBLANES)
        a8 = a_ref[pl.ds(r0, SUBLANES), :]
        g8 = g_ref[pl.ds(r0, SUBLANES), :]
        d = 1
        while d < SUBLANES:
            shift = SUBLANES - d if reverse else d
            keep = (srow < SUBLANES - d) if reverse else (srow >= d)
            g8 = g8 + a8 * jnp.where(keep, pltpu.roll(g8, shift, axis=0), 0.0)
            a8 = a8 * jnp.where(keep, pltpu.roll(a8, shift, axis=0), 1.0)
            d *= 2
        hs = g8 + a8 * h
        hs_ref[pl.ds(r0, SUBLANES), :] = hs
        return hs[carry_row:carry_row + 1, :]

    h_ref[...] = lax.fori_loop(0, n_tiles, body, h_ref[...], unroll=2)
    if reverse:
        out_ref[...] = (_gelu_tanh(gate_ref[...]) * (hf_ref[...] + hs_ref[...])).astype(out_ref.dtype)


def _lru_pass(u, cw, cb, wg, bg, lam, B, S, reverse, hf=None, gate=None):
    T = B * S
    W = LRU_WIDTH
    ts = min(SEQ_TILE, S)
    nb = S // ts
    tpb = ts // SUBLANES
    n8 = T // SUBLANES
    if reverse:
        seq = lambda b, j: b * nb + nb - 1 - j
    else:
        seq = lambda b, j: b * nb + j
    blk = lambda b, j: (seq(b, j), 0)
    blk_prev = lambda b, j: (jnp.maximum(seq(b, j) * tpb - 1, 0), 0)
    blk_next = lambda b, j: (jnp.minimum((seq(b, j) + 1) * tpb, n8 - 1), 0)
    fix2 = lambda b, j: (0, 0)
    fix3 = lambda b, j: (0, 0, 0)
    in_specs = [pl.BlockSpec((ts, W), blk), pl.BlockSpec((SUBLANES, W), blk_prev),
                pl.BlockSpec((SUBLANES, W), blk_next), pl.BlockSpec(cw.shape, fix2),
                pl.BlockSpec(cb.shape, fix2), pl.BlockSpec(wg.shape, fix3), pl.BlockSpec(bg.shape, fix3),
                pl.BlockSpec(lam.shape, fix2)]
    args = [u, u, u, cw, cb, wg, bg, lam]
    scratch = [pltpu.VMEM((ts, W), F32), pltpu.VMEM((ts, W), F32), pltpu.VMEM((1, W), F32)]
    if reverse:
        in_specs += [pl.BlockSpec((ts, W), blk), pl.BlockSpec((ts, W), blk)]
        args += [hf, gate]
        scratch.append(pltpu.VMEM((ts, W), F32))
    kern = functools.partial(_lru_kernel, reverse=reverse, ts=ts)
    return pl.pallas_call(
        kern, grid=(B, nb), in_specs=in_specs,
        out_specs=pl.BlockSpec((ts, W), blk),
        out_shape=jax.ShapeDtypeStruct((T, W), BF16 if reverse else F32),
        scratch_shapes=scratch,
        compiler_params=_params("parallel", "arbitrary"),
        name="lru_bwd" if reverse else "lru_fwd")(*args)


def _lru_layer(x, xb, B, S, w_in, conv_w, conv_b, gate_a_w, gate_a_b, gate_x_w, gate_x_b, lam, w_out,
               ln_g, ln_b):
    del xb
    W = LRU_WIDTH
    w = w_in.astype(BF16)
    gate, u = _dense(x, [w[:, :W], w[:, W:]], [F32, F32])
    cw = conv_w.astype(F32)
    cb = conv_b.astype(F32).reshape(1, W)
    passes = []
    for d in range(2):
        wg = jnp.concatenate([gate_a_w[d], gate_x_w[d]], axis=-1).astype(BF16)
        bg = jnp.concatenate([gate_a_b[d].reshape(LRU_BLOCKS, 1, LRU_BW),
                              gate_x_b[d].reshape(LRU_BLOCKS, 1, LRU_BW)], axis=-1).astype(F32)
        passes.append((wg, bg, lam[d].astype(F32).reshape(1, W)))
    hf = _lru_pass(u, cw, cb, *passes[0], B, S, False)
    a = _lru_pass(u, cw, cb, *passes[1], B, S, True, hf, gate)
    return _outproj_ln(a, w_out.astype(BF16), x, ln_g, ln_b)


MLA_HW = MLA_NOPE + LANES


def _mla_proj_kernel(x_ref, pos_ref, win_ref, qg_ref, kg_ref, wqn_ref, wqr_ref, wqs_ref, wkv_ref,
                     fr_ref, sg_ref, q_ref, k_ref, v_ref):
    H = MLA_HEADS
    scale = (MLA_NOPE + MLA_ROPE) ** -0.5 * math.log2(math.e)
    xb = x_ref[...].astype(BF16)
    z = _dot(xb, win_ref[...])
    cq = z[:, :MLA_Q_RANK]
    ckv = z[:, MLA_Q_RANK:MLA_Q_RANK + MLA_KV_RANK]
    kr = z[:, MLA_Q_RANK + MLA_KV_RANK:MLA_Q_RANK + MLA_KV_RANK + LANES]
    krs = z[:, MLA_Q_RANK + MLA_KV_RANK + LANES:]
    qn = (cq * lax.rsqrt(jnp.mean(cq * cq, axis=-1, keepdims=True) + LN_EPS) * qg_ref[...]).astype(BF16)
    kvn = (ckv * lax.rsqrt(jnp.mean(ckv * ckv, axis=-1, keepdims=True) + LN_EPS) * kg_ref[...]).astype(BF16)
    ang = pos_ref[...].astype(F32) * fr_ref[...]
    cosv = jnp.cos(ang)
    lane = lax.broadcasted_iota(jnp.int32, ang.shape, 1)
    cosv = jnp.where(lane < MLA_ROPE, cosv, 0.0)
    sinv = jnp.sin(ang) * sg_ref[...]
    k_rope = kr * cosv + krs * sinv
    kv = _dot(kvn, wkv_ref[...])
    q_nope = _dot(qn, wqn_ref[...])
    q_rope = _dot(qn, wqr_ref[...])
    q_swap = _dot(qn, wqs_ref[...])
    ones_col = jnp.where(lane == 0, 1.0, 0.0).astype(BF16)
    for h in range(H):
        a0 = h * MLA_HW
        q_ref[:, a0:a0 + MLA_NOPE] = (q_nope[:, h * MLA_NOPE:(h + 1) * MLA_NOPE] * scale).astype(BF16)
        qr = q_rope[:, h * LANES:(h + 1) * LANES] * cosv + q_swap[:, h * LANES:(h + 1) * LANES] * sinv
        q_ref[:, a0 + MLA_NOPE:a0 + MLA_HW] = (qr * scale).astype(BF16)
        k_ref[:, a0:a0 + MLA_NOPE] = kv[:, h * 2 * MLA_NOPE:h * 2 * MLA_NOPE + MLA_NOPE].astype(BF16)
        k_ref[:, a0 + MLA_NOPE:a0 + MLA_HW] = k_rope.astype(BF16)
        v_ref[:, 2 * h * MLA_DV:(2 * h + 1) * MLA_DV] = kv[:, h * 2 * MLA_NOPE + MLA_NOPE:(h + 1) * 2 * MLA_NOPE].astype(BF16)
        v_ref[:, (2 * h + 1) * MLA_DV:(2 * h + 2) * MLA_DV] = ones_col


def _attn_kernel(q_ref, k_ref, v_ref, o_ref, *, n_sub):
    k = k_ref[...]
    v = v_ref[...]
    rows = q_ref.shape[0] // n_sub
    for i in range(n_sub):
        s = _dot_nt(q_ref[i * rows:(i + 1) * rows, :], k)
        p = jnp.exp2(s - jnp.max(s, axis=-1, keepdims=True))
        acc = _dot(p.astype(BF16), v)
        o_ref[i * rows:(i + 1) * rows, :] = (acc[:, :MLA_DV] / acc[:, MLA_DV:MLA_DV + 1]).astype(o_ref.dtype)


def _pad_rope_cols(w, swap):
    half = MLA_ROPE // 2
    if swap:
        w = jnp.concatenate([w[..., half:], w[..., :half]], axis=-1)
    w = jnp.concatenate([w, jnp.zeros_like(w)], axis=-1)
    return w.reshape(w.shape[0], -1)


def _mla_layer(x, xb, B, S, positions, w_in, q_norm_g, kv_norm_g, w_uq, w_ukv, w_out, ln_g, ln_b):
    del xb
    T = B * S
    H = MLA_HEADS
    half = MLA_ROPE // 2
    w_kr = w_in[:, MLA_Q_RANK + MLA_KV_RANK:].reshape(D_MODEL, 1, MLA_ROPE)
    win = jnp.concatenate([w_in[:, :MLA_Q_RANK + MLA_KV_RANK], _pad_rope_cols(w_kr, False),
                           _pad_rope_cols(w_kr, True)], axis=1).astype(BF16)
    wq = w_uq.reshape(MLA_Q_RANK, H, MLA_NOPE + MLA_ROPE)
    wqn = wq[:, :, :MLA_NOPE].reshape(MLA_Q_RANK, H * MLA_NOPE).astype(BF16)
    wqr = _pad_rope_cols(wq[:, :, MLA_NOPE:], False).astype(BF16)
    wqs = _pad_rope_cols(wq[:, :, MLA_NOPE:], True).astype(BF16)
    freq = ROPE_THETA ** (-jnp.arange(half, dtype=F32) / half)
    zeros = jnp.zeros((LANES - MLA_ROPE,), F32)
    fr = jnp.concatenate([freq, freq, zeros]).reshape(1, LANES)
    sg = jnp.concatenate([-jnp.ones((half,), F32), jnp.ones((half,), F32), zeros]).reshape(1, LANES)
    tm = min(ROW_TILE, T)
    row = lambda i: (i, 0)
    fix = lambda i: (0, 0)
    ins = [x, positions.reshape(T, 1), win, q_norm_g.astype(F32).reshape(1, -1),
           kv_norm_g.astype(F32).reshape(1, -1), wqn, wqr, wqs, w_ukv.astype(BF16), fr, sg]
    in_specs = [pl.BlockSpec((tm, D_MODEL), row), pl.BlockSpec((tm, 1), row)]
    in_specs += [pl.BlockSpec(a.shape, fix) for a in ins[2:]]
    q, k, v = pl.pallas_call(
        _mla_proj_kernel, grid=(T // tm,), in_specs=in_specs,
        out_specs=[pl.BlockSpec((tm, H * MLA_HW), row), pl.BlockSpec((tm, H * MLA_HW), row),
                   pl.BlockSpec((tm, 2 * H * MLA_DV), row)],
        out_shape=[jax.ShapeDtypeStruct((T, H * MLA_HW), BF16), jax.ShapeDtypeStruct((T, H * MLA_HW), BF16),
                   jax.ShapeDtypeStruct((T, 2 * H * MLA_DV), BF16)],
        compiler_params=_params("parallel"), name="mla_proj")(*ins)
    tq = min(Q_TILE, S)
    nq = S // tq
    att = pl.pallas_call(
        functools.partial(_attn_kernel, n_sub=tq // min(Q_SUB, tq)), grid=(B, H, nq),
        in_specs=[pl.BlockSpec((tq, MLA_HW), lambda b, h, i: (b * nq + i, h)),
                  pl.BlockSpec((S, MLA_HW), lambda b, h, i: (b, h)),
                  pl.BlockSpec((S, 2 * MLA_DV), lambda b, h, i: (b, h))],
        out_specs=pl.BlockSpec((tq, MLA_DV), lambda b, h, i: (b * nq + i, h)),
        out_shape=jax.ShapeDtypeStruct((T, H * MLA_DV), BF16),
        compiler_params=_params("parallel", "parallel", "arbitrary"), name="mla_attn")(q, k, v)
    return _outproj_ln(att, w_out.astype(BF16), x, ln_g, ln_b)


def _router_kernel(x_ref, wh_ref, wl_ref, pos_ref, gate_ref, idx_ref, *, cap):
    E = N_EXPERTS
    x = x_ref[...]
    S = x.shape[0]
    xh = x.astype(BF16)
    xl = (x - xh.astype(F32)).astype(BF16)
    wh = wh_ref[...]
    logits = _dot_nt(wh, xh) + (_dot_nt(wl_ref[...], xh) + _dot_nt(wh, xl))
    mx = jnp.max(logits, axis=0, keepdims=True)
    ex = jnp.exp(logits - mx)
    aff = ex / jnp.sum(ex, axis=0, keepdims=True)
    bits = pltpu.bitcast(aff, jnp.int32)

    def bit_step(i, thr):
        cand = thr | jnp.left_shift(jnp.int32(1), 30 - i)
        cnt = jnp.sum(jnp.where(bits >= cand, 1.0, 0.0), axis=1, keepdims=True)
        return jnp.where(cnt >= cap, cand, thr)

    thr = lax.fori_loop(0, 31, bit_step, jnp.zeros((E, 1), jnp.int32))
    gt = bits > thr
    eq = bits == thr
    need = cap - jnp.sum(jnp.where(gt, 1.0, 0.0), axis=1, keepdims=True)
    r = lax.broadcasted_iota(jnp.int32, (LANES, LANES), 0)
    c = lax.broadcasted_iota(jnp.int32, (LANES, LANES), 1)
    upper = jnp.where(r < c, 1.0, 0.0).astype(BF16)
    off = jnp.zeros((2 * E, 1), F32)
    for blk in range(S // LANES):
        sl = slice(blk * LANES, (blk + 1) * LANES)
        ind = jnp.concatenate([jnp.where(gt[:, sl], 1.0, 0.0), jnp.where(eq[:, sl], 1.0, 0.0)], axis=0)
        pre = _dot(ind.astype(BF16), upper) + off
        off = off + jnp.sum(ind, axis=1, keepdims=True)
        pg, pe = pre[:E], pre[E:]
        sel = gt[:, sl] | (eq[:, sl] & (pe < need))
        slot = pg + jnp.minimum(pe, need)
        pos_ref[:, sl] = jnp.where(sel, slot, -1.0).astype(jnp.int32)
        gate_ref[:, sl] = jnp.where(sel, aff[:, sl], 0.0)

    slots = lax.broadcasted_iota(jnp.int32, (cap, S), 0)
    token = lax.broadcasted_iota(jnp.int32, (cap, S), 1).astype(F32)
    eye = (lax.broadcasted_iota(jnp.int32, (cap, cap), 0) == lax.broadcasted_iota(jnp.int32, (cap, cap), 1))
    for e in range(E):
        hit = slots == pos_ref[e:e + 1, :]
        idx_c = jnp.sum(jnp.where(hit, token, 0.0), axis=1, keepdims=True)
        idx_r = jnp.sum(jnp.where(eye, idx_c, 0.0), axis=0, keepdims=True)
        idx_ref[e:e + 1, :] = idx_r.astype(jnp.int32)


def _ffn_kernel(idx_ref, xp_ref, pos_ref, gate_ref, wg_ref, wu_ref, wd_ref, y_ref, xin_ref, *, cap):
    S, half = xp_ref.shape

    @pl.when(pl.program_id(1) == 0)
    def _():
        y_ref[...] = jnp.zeros_like(y_ref)

    def gather(t, carry):
        j0 = pl.multiple_of(t * SUBLANES, SUBLANES)
        rows = [xp_ref[pl.ds(idx_ref[0, 0, j0 + k], 1), :] for k in range(SUBLANES)]
        xin_ref[pl.ds(j0, SUBLANES), :] = jnp.concatenate(rows, axis=0)
        return carry

    lax.fori_loop(0, cap // SUBLANES, gather, 0, unroll=2)
    words = xin_ref[...]
    x_lo = pltpu.bitcast(words << 16, F32).astype(BF16)
    x_hi = pltpu.bitcast(words & jnp.uint32(0xFFFF0000), F32).astype(BF16)

    slot = lax.broadcasted_iota(jnp.int32, (cap, S), 0)
    hit = slot == pos_ref[0]
    onehot = jnp.where(hit, 1.0, 0.0).astype(BF16)
    gate = jnp.sum(jnp.where(hit, gate_ref[0], 0.0), axis=1, keepdims=True)
    hg = _dot(x_lo, wg_ref[0, :half, :]) + _dot(x_hi, wg_ref[0, half:, :])
    hu = _dot(x_lo, wu_ref[0, :half, :]) + _dot(x_hi, wu_ref[0, half:, :])
    hmid = (hg * _sigmoid(hg) * hu).astype(BF16)
    out = (_dot(hmid, wd_ref[0]) * gate).astype(BF16)
    y_ref[...] += _dot_tn(onehot, out)


def _moe_layer(x, xb, B, S, w_router, w_gate, w_up, w_down, ln_g, ln_b):
    T = B * S
    E = N_EXPERTS
    D = D_MODEL
    cap = CAPACITY_FACTOR * S // E
    wr = w_router.astype(F32).T
    wh = wr.astype(BF16)
    wl = (wr - wh.astype(F32)).astype(BF16)
    pos, gate, idx = pl.pallas_call(
        functools.partial(_router_kernel, cap=cap), grid=(B,),
        in_specs=[pl.BlockSpec((S, D), lambda b: (b, 0)), pl.BlockSpec((E, D), lambda b: (0, 0)),
                  pl.BlockSpec((E, D), lambda b: (0, 0))],
        out_specs=[pl.BlockSpec((E, S), lambda b: (b, 0)), pl.BlockSpec((E, S), lambda b: (b, 0)),
                   pl.BlockSpec((E, cap), lambda b: (b, 0))],
        out_shape=[jax.ShapeDtypeStruct((B * E, S), jnp.int32), jax.ShapeDtypeStruct((B * E, S), F32),
                   jax.ShapeDtypeStruct((B * E, cap), jnp.int32)],
        compiler_params=_params("parallel"), name="moe_router")(x, wh, wl)
    pos = pos.reshape(B * E, 1, S)
    gate = gate.reshape(B * E, 1, S)
    idx = idx.reshape(B * E, 1, cap)
    ff = w_gate.shape[-1]
    y = pl.pallas_call(
        functools.partial(_ffn_kernel, cap=cap), grid=(B, E),
        in_specs=[pl.BlockSpec((1, 1, cap), lambda b, e: (b * E + e, 0, 0), memory_space=pltpu.SMEM),
                  pl.BlockSpec((S, D // 2), lambda b, e: (b, 0)),
                  pl.BlockSpec((1, 1, S), lambda b, e: (b * E + e, 0, 0)),
                  pl.BlockSpec((1, 1, S), lambda b, e: (b * E + e, 0, 0)),
                  pl.BlockSpec((1, D, ff), lambda b, e: (e, 0, 0)),
                  pl.BlockSpec((1, D, ff), lambda b, e: (e, 0, 0)),
                  pl.BlockSpec((1, ff, D), lambda b, e: (e, 0, 0))],
        out_specs=pl.BlockSpec((S, D), lambda b, e: (b, 0)),
        out_shape=jax.ShapeDtypeStruct((T, D), F32),
        scratch_shapes=[pltpu.VMEM((cap, D // 2), jnp.uint32)],
        compiler_params=_params("parallel", "arbitrary"), name="moe_ffn")(
            idx, xb, pos, gate, w_gate.astype(BF16), w_up.astype(BF16), w_down.astype(BF16))
    return _residual_ln(x, y, ln_g, ln_b)


def kernel(x, positions, mlstm_w_in, mlstm_gate_b, mlstm_norm_g, mlstm_w_out, gla_w_in, gla_gate_w, gla_gate_b, gla_norm_g, gla_w_out, lru_w_in, lru_conv_w, lru_conv_b, lru_gate_a_w, lru_gate_a_b, lru_gate_x_w, lru_gate_x_b, lru_lambda, lru_w_out, mla_w_in, mla_q_norm_g, mla_kv_norm_g, mla_w_uq, mla_w_ukv, mla_w_out, moe_router, moe_w_gate, moe_w_up, moe_w_down, ln_g, ln_b):
    B, S, D = x.shape
    xf = x.reshape(B * S, D)
    xb = None
    for i in range(DEPTH):
        m = i % N_MIXERS
        j = i // N_MIXERS
        g0, b0 = ln_g[i, 0], ln_b[i, 0]
        if m == 0:
            xf, xb = _mlstm_layer(xf, xb, B, S, mlstm_w_in[j], mlstm_gate_b[j], mlstm_norm_g[j],
                                  mlstm_w_out[j], g0, b0)
        elif m == 1:
            xf, xb = _gla_layer(xf, xb, B, S, gla_w_in[j], gla_gate_w[j], gla_gate_b[j], gla_norm_g[j],
                                gla_w_out[j], g0, b0)
        elif m == 2:
            xf, xb = _lru_layer(xf, xb, B, S, lru_w_in[j], lru_conv_w[j], lru_conv_b[j], lru_gate_a_w[j],
                                lru_gate_a_b[j], lru_gate_x_w[j], lru_gate_x_b[j], lru_lambda[j],
                                lru_w_out[j], g0, b0)
        else:
            xf, xb = _mla_layer(xf, xb, B, S, positions, mla_w_in[j], mla_q_norm_g[j], mla_kv_norm_g[j],
                                mla_w_uq[j], mla_w_ukv[j], mla_w_out[j], g0, b0)
        xf = _moe_layer(xf, xb, B, S, moe_router[i], moe_w_gate[i], moe_w_up[i], moe_w_down[i],
                        ln_g[i, 1], ln_b[i, 1])
    return xf.reshape(B, S, D)
```

```python
import functools
import math

import jax
import jax.numpy as jnp
from jax import lax
from jax.experimental import pallas as pl
from jax.experimental.pallas import tpu as pltpu

F32 = jnp.float32
BF16 = jnp.bfloat16
HIGHEST = lax.Precision.HIGHEST

D_MODEL = 1024
DEPTH = 4
N_MIXERS = 4
ALPHA = (2 * DEPTH) ** 0.25
LN_EPS = 1e-5

ML_HEADS = 4
ML_DV = D_MODEL // ML_HEADS
ML_DK = ML_DV // 2

GLA_HEADS = 4
GLA_DK = D_MODEL // 2 // GLA_HEADS
GLA_DV = D_MODEL // GLA_HEADS
GLA_RANK = 16
GLA_TAU = 16.0

LRU_WIDTH = D_MODEL
LRU_BLOCKS = 4
LRU_BW = LRU_WIDTH // LRU_BLOCKS
CONV_WIDTH = 4
LRU_C = 8.0

MLA_HEADS = 8
MLA_NOPE = 128
MLA_ROPE = 64
MLA_DV = 128
MLA_Q_RANK = 384
MLA_KV_RANK = 256
ROPE_THETA = 10000.0

N_EXPERTS = 16
CAPACITY_FACTOR = 2

V7X_VMEM_BYTES = 64 * 1024 * 1024
VMEM_LIMIT = V7X_VMEM_BYTES - 8 * 1024 * 1024
LANES = 128
SUBLANES = 8

ROW_TILE = 512
SEQ_TILE = 512
ML_CHUNK = 512
GLA_CHUNK = 64
Q_TILE = 2048
Q_SUB = 256
N_CHUNK = 512
FFN_GROUP = 4
COMBINE_TILE = 1024


def _params(*sem):
    return pltpu.CompilerParams(dimension_semantics=sem, vmem_limit_bytes=VMEM_LIMIT)


def _log_sigmoid(x):
    return jnp.minimum(x, 0.0) - jnp.log(1.0 + jnp.exp(-jnp.abs(x)))


def _sigmoid(x):
    return 0.5 * jnp.tanh(0.5 * x) + 0.5


def _layer_norm(v, g, b):
    mu = jnp.mean(v, axis=-1, keepdims=True)
    d = v - mu
    var = jnp.mean(d * d, axis=-1, keepdims=True)
    return d * lax.rsqrt(var + LN_EPS) * g + b


def _dot(a, b):
    return jnp.dot(a, b, preferred_element_type=F32)


def _dot_nt(a, b):
    return lax.dot_general(a, b, (((1,), (1,)), ((), ())), preferred_element_type=F32)


def _dot_tn(a, b):
    return lax.dot_general(a, b, (((0,), (0,)), ((), ())), preferred_element_type=F32)


def _dense_kernel(*refs, n_w, n_t, has_bias):
    x_ref = refs[0]
    pos = 1
    w_refs = refs[pos:pos + n_w]
    pos += n_w
    b_refs = []
    for hb in has_bias:
        if hb:
            b_refs.append(refs[pos])
            pos += 1
        else:
            b_refs.append(None)
    t_refs = refs[pos:pos + 2 * n_t]
    pos += 2 * n_t
    o_refs = refs[pos:pos + n_w]
    pos += n_w
    ot_refs = refs[pos:pos + n_t]

    xb = x_ref[...].astype(BF16)
    for w_ref, b_ref, o_ref in zip(w_refs, b_refs, o_refs):
        n = w_ref.shape[1]
        for j0 in range(0, n, N_CHUNK):
            j1 = min(n, j0 + N_CHUNK)
            acc = _dot(xb, w_ref[:, j0:j1])
            if b_ref is not None:
                acc = acc + b_ref[:, j0:j1]
            o_ref[:, j0:j1] = acc.astype(o_ref.dtype)
    for i in range(n_t):
        wt_ref, bt_ref = t_refs[2 * i], t_refs[2 * i + 1]
        ot_refs[i][...] = _dot_nt(wt_ref[...], xb) + bt_ref[...]


def _dense(x, ws, dtypes, biases=None, transposed=()):
    T, K = x.shape
    tm = min(ROW_TILE, T)
    if biases is None:
        biases = [None] * len(ws)
    has_bias = tuple(b is not None for b in biases)
    args = [x] + list(ws) + [b for b in biases if b is not None]
    in_specs = [pl.BlockSpec((tm, K), lambda i: (i, 0))]
    in_specs += [pl.BlockSpec(w.shape, lambda i: (0, 0)) for w in ws]
    in_specs += [pl.BlockSpec(b.shape, lambda i: (0, 0)) for b in biases if b is not None]
    for wt, bt in transposed:
        args += [wt, bt]
        in_specs += [pl.BlockSpec(wt.shape, lambda i: (0, 0)), pl.BlockSpec(bt.shape, lambda i: (0, 0))]
    out_shape = [jax.ShapeDtypeStruct((T, w.shape[1]), dt) for w, dt in zip(ws, dtypes)]
    out_specs = [pl.BlockSpec((tm, w.shape[1]), lambda i: (i, 0)) for w in ws]
    for wt, _ in transposed:
        out_shape.append(jax.ShapeDtypeStruct((wt.shape[0], T), F32))
        out_specs.append(pl.BlockSpec((wt.shape[0], tm), lambda i: (0, i)))
    kern = functools.partial(_dense_kernel, n_w=len(ws), n_t=len(transposed), has_bias=has_bias)
    return pl.pallas_call(
        kern, grid=(T // tm,), in_specs=in_specs, out_specs=out_specs, out_shape=out_shape,
        compiler_params=_params("parallel"), name="dense")(*args)


def _outproj_ln_kernel(a_ref, w_ref, x_ref, g_ref, b_ref, o_ref, ob_ref):
    y = _dot(a_ref[...], w_ref[...])
    v = _layer_norm(ALPHA * x_ref[...] + y, g_ref[...], b_ref[...])
    o_ref[...] = v
    half = v.shape[1] // 2
    bits = pltpu.bitcast(v.astype(BF16).astype(F32), jnp.uint32)
    ob_ref[...] = (bits[:, :half] >> 16) | bits[:, half:]


def _outproj_ln(a, w, x, g, b):
    T, K = a.shape
    D = w.shape[1]
    tm = min(ROW_TILE, T)
    row = lambda i: (i, 0)
    fix = lambda i: (0, 0)
    return pl.pallas_call(
        _outproj_ln_kernel, grid=(T // tm,),
        in_specs=[pl.BlockSpec((tm, K), row), pl.BlockSpec((K, D), fix), pl.BlockSpec((tm, D), row),
                  pl.BlockSpec((1, D), fix), pl.BlockSpec((1, D), fix)],
        out_specs=[pl.BlockSpec((tm, D), row), pl.BlockSpec((tm, D // 2), row)],
        out_shape=[jax.ShapeDtypeStruct((T, D), F32), jax.ShapeDtypeStruct((T, D // 2), jnp.uint32)],
        compiler_params=_params("parallel"), name="outproj_ln")(a, w, x, g.reshape(1, D), b.reshape(1, D))


def _mlstm_kernel(*refs, reverse, ts, chunk):
    if reverse:
        qkv_ref, gr_ref, hf_ref, o_ref, ng_ref, out_ref, c_ref, n_ref, m_ref = refs
    else:
        qkv_ref, gr_ref, out_ref, c_ref, n_ref, m_ref = refs
    H, DK, DV, L = ML_HEADS, ML_DK, ML_DV, chunk
    scale = DK ** -0.5

    @pl.when(pl.program_id(1) == 0)
    def _():
        c_ref[...] = jnp.zeros_like(c_ref)
        n_ref[...] = jnp.zeros_like(n_ref)
        m_ref[...] = jnp.zeros_like(m_ref)

    rows = lax.broadcasted_iota(jnp.int32, (L, L), 0)
    cols = lax.broadcasted_iota(jnp.int32, (L, L), 1)
    mask = (cols >= rows) if reverse else (cols <= rows)
    eye = rows == cols
    tri = jnp.where((rows >= cols) if reverse else (rows <= cols), 1.0, 0.0).astype(F32)
    d0 = 8 if reverse else 0
    last = 0 if reverse else L - 1
    n_chunks = ts // L
    order = range(n_chunks - 1, -1, -1) if reverse else range(n_chunks)
    neg_inf = -jnp.inf

    for c in order:
        r0 = c * L
        g8 = gr_ref[d0:d0 + 8, r0:r0 + L]
        lf8 = _log_sigmoid(g8)
        b8 = jnp.dot(lf8, tri, precision=HIGHEST, preferred_element_type=F32)
        u8 = g8[0:4, :] - b8[4:8, :]
        for h in range(H):
            qb = qkv_ref[r0:r0 + L, h * DK:(h + 1) * DK]
            kb = qkv_ref[r0:r0 + L, H * DK + h * DK:H * DK + (h + 1) * DK]
            vb = qkv_ref[r0:r0 + L, 2 * H * DK + h * DV:2 * H * DK + (h + 1) * DV]
            u_r = u8[h:h + 1, :]
            b_r = b8[4 + h:5 + h, :]
            m_prev = m_ref[h:h + 1, 0:1]
            u_c = jnp.sum(jnp.where(eye, u_r, 0.0), axis=1, keepdims=True)
            b_c = jnp.sum(jnp.where(eye, b_r, 0.0), axis=1, keepdims=True)
            um = jnp.where(mask, u_r, neg_inf)
            a_c = jnp.maximum(m_prev, jnp.max(um, axis=1, keepdims=True))
            dmat = jnp.exp(um - a_c)
            s = _dot_nt(qb, kb) * (scale * dmat)
            w_int = jnp.exp(m_prev - a_c) * scale
            c_old = c_ref[h]
            num = _dot(s.astype(BF16), vb) + w_int * _dot(qb, c_old.astype(BF16))
            qn = jnp.sum(qb.astype(F32) * n_ref[h], axis=1, keepdims=True)
            den = jnp.sum(s, axis=1, keepdims=True) + w_int * qn
            hh = num / jnp.maximum(jnp.abs(den), jnp.exp(-(a_c + b_c)))
            a_last = jnp.maximum(m_prev, jnp.max(u_r, axis=1, keepdims=True))
            g_tot = b_r[:, last:last + 1]
            ws_c = jnp.exp(u_c - a_last)
            wc = jnp.exp(m_prev - a_last)
            kw = kb.astype(F32) * ws_c
            c_ref[h] = wc * c_old + _dot_tn(kw.astype(BF16), vb)
            n_ref[h] = wc * n_ref[h] + jnp.sum(kw, axis=0, keepdims=True)
            m_ref[h:h + 1, :] = jnp.broadcast_to(g_tot + a_last, (1, LANES))
            if reverse:
                hs = hf_ref[r0:r0 + L, h * DV:(h + 1) * DV] + hh
                mu = jnp.mean(hs, axis=-1, keepdims=True)
                dd = hs - mu
                var = jnp.mean(dd * dd, axis=-1, keepdims=True)
                hn = dd * lax.rsqrt(var + LN_EPS) * ng_ref[:, h * DV:(h + 1) * DV]
                og = _sigmoid(o_ref[r0:r0 + L, h * DV:(h + 1) * DV])
                out_ref[r0:r0 + L, h * DV:(h + 1) * DV] = (og * hn).astype(out_ref.dtype)
            else:
                out_ref[r0:r0 + L, h * DV:(h + 1) * DV] = hh


def _mlstm_pass(qkv, gr, B, S, reverse, hf=None, o=None, ng=None):
    T = B * S
    ts = min(SEQ_TILE, S)
    chunk = min(ML_CHUNK, ts)
    nb = S // ts
    if reverse:
        blk = lambda b, j: (b * nb + nb - 1 - j, 0)
        blk_t = lambda b, j: (0, b * nb + nb - 1 - j)
    else:
        blk = lambda b, j: (b * nb + j, 0)
        blk_t = lambda b, j: (0, b * nb + j)
    wq = qkv.shape[1]
    in_specs = [pl.BlockSpec((ts, wq), blk), pl.BlockSpec((16, ts), blk_t)]
    args = [qkv, gr]
    if reverse:
        in_specs += [pl.BlockSpec((ts, D_MODEL), blk), pl.BlockSpec((ts, D_MODEL), blk),
                     pl.BlockSpec((1, D_MODEL), lambda b, j: (0, 0))]
        args += [hf, o, ng.reshape(1, D_MODEL)]
    out_dtype = BF16 if reverse else F32
    kern = functools.partial(_mlstm_kernel, reverse=reverse, ts=ts, chunk=chunk)
    return pl.pallas_call(
        kern, grid=(B, nb), in_specs=in_specs,
        out_specs=pl.BlockSpec((ts, D_MODEL), blk),
        out_shape=jax.ShapeDtypeStruct((T, D_MODEL), out_dtype),
        scratch_shapes=[pltpu.VMEM((ML_HEADS, ML_DK, ML_DV), F32),
                        pltpu.VMEM((ML_HEADS, 1, ML_DK), F32),
                        pltpu.VMEM((SUBLANES, LANES), F32)],
        compiler_params=_params("parallel", "arbitrary"),
        name="mlstm_bwd" if reverse else "mlstm_fwd")(*args)


def _mlstm_layer(x, xb, B, S, w_in, gate_b, norm_g, w_out, ln_g, ln_b):
    del xb
    qk = ML_HEADS * ML_DK
    w = w_in.astype(BF16)
    w_qkv = w[:, :2 * qk + D_MODEL]
    w_o = w[:, 2 * qk + D_MODEL:2 * qk + 2 * D_MODEL]
    w_g_t = w[:, 2 * qk + 2 * D_MODEL:].T
    b_g = gate_b.astype(F32).reshape(16, 1)
    qkv, o, gr = _dense(x, [w_qkv, w_o], [BF16, F32], transposed=[(w_g_t, b_g)])
    hf = _mlstm_pass(qkv, gr, B, S, False)
    a = _mlstm_pass(qkv, gr, B, S, True, hf, o, norm_g.astype(F32))
    return _outproj_ln(a, w_out.astype(BF16), x, ln_g, ln_b)


def _cumsum_rows(x, n, reverse):
    row = lax.broadcasted_iota(jnp.int32, x.shape, 0)
    sh = 1
    while sh < n:
        if reverse:
            x = x + jnp.where(row < n - sh, pltpu.roll(x, n - sh, axis=0), 0.0)
        else:
            x = x + jnp.where(row >= sh, pltpu.roll(x, sh, axis=0), 0.0)
        sh *= 2
    return x


def _gla_kernel(*refs, reverse, ts, chunk):
    if reverse:
        qkv_ref, glr_ref, gw_ref, gb_ref, of_ref, r_ref, ng_ref, out_ref, st_ref, la_ref = refs
    else:
        qkv_ref, glr_ref, gw_ref, gb_ref, out_ref, st_ref, la_ref = refs
    H, DK, DV, L = GLA_HEADS, GLA_DK, GLA_DV, chunk
    scale = DK ** -0.5
    d = 1 if reverse else 0

    @pl.when(pl.program_id(1) == 0)
    def _():
        st_ref[...] = jnp.zeros_like(st_ref)

    glr = glr_ref[:, d * GLA_RANK:(d + 1) * GLA_RANK]
    pre = jnp.dot(glr, gw_ref[d], precision=HIGHEST, preferred_element_type=F32) + gb_ref[d:d + 1, :]
    la_ref[...] = _log_sigmoid(pre) * (1.0 / GLA_TAU)

    rows = lax.broadcasted_iota(jnp.int32, (L, L), 0)
    cols = lax.broadcasted_iota(jnp.int32, (L, L), 1)
    mask = (cols >= rows) if reverse else (cols <= rows)
    last = 0 if reverse else L - 1
    mid = L // 2
    n_chunks = ts // L

    def body(i, carry):
        c = (n_chunks - 1 - i) if reverse else i
        r0 = pl.multiple_of(c * L, L)
        bsum = _cumsum_rows(la_ref[pl.ds(r0, L), :], L, reverse)
        for h in range(H):
            b = bsum[:, h * DK:(h + 1) * DK]
            qf = qkv_ref[pl.ds(r0, L), h * DK:(h + 1) * DK].astype(F32)
            kf = qkv_ref[pl.ds(r0, L), H * DK + h * DK:H * DK + (h + 1) * DK].astype(F32)
            vb = qkv_ref[pl.ds(r0, L), 2 * H * DK + h * DV:2 * H * DK + (h + 1) * DV]
            beta = b[mid:mid + 1, :]
            g = b[last:last + 1, :]
            qt = (qf * jnp.exp(b - beta)).astype(BF16)
            kt = (kf * jnp.exp(beta - b)).astype(BF16)
            amat = jnp.where(mask, _dot_nt(qt, kt) * scale, 0.0)
            qh = (qf * (jnp.exp(b) * scale)).astype(BF16)
            st = st_ref[h]
            o = _dot(amat.astype(BF16), vb) + _dot_nt(qh, st.astype(BF16))
            kh = (kf * jnp.exp(g - b)).astype(BF16)
            st_ref[h] = st * jnp.exp(g) + _dot_tn(vb, kh)
            if reverse:
                hs = of_ref[pl.ds(r0, L), h * DV:(h + 1) * DV] + o
                mu = jnp.mean(hs, axis=-1, keepdims=True)
                dd = hs - mu
                var = jnp.mean(dd * dd, axis=-1, keepdims=True)
                hn = dd * lax.rsqrt(var + LN_EPS) * ng_ref[:, h * DV:(h + 1) * DV]
                rr = r_ref[pl.ds(r0, L), h * DV:(h + 1) * DV]
                out_ref[pl.ds(r0, L), h * DV:(h + 1) * DV] = (rr * _sigmoid(rr) * hn).astype(out_ref.dtype)
            else:
                out_ref[pl.ds(r0, L), h * DV:(h + 1) * DV] = o
        return carry

    lax.fori_loop(0, n_chunks, body, 0)


def _gla_pass(qkv, glr, gw, gb, B, S, reverse, of=None, r=None, ng=None):
    T = B * S
    ts = min(SEQ_TILE, S)
    chunk = min(GLA_CHUNK, ts)
    nb = S // ts
    if reverse:
        blk = lambda b, j: (b * nb + nb - 1 - j, 0)
    else:
        blk = lambda b, j: (b * nb + j, 0)
    fix2 = lambda b, j: (0, 0)
    fix3 = lambda b, j: (0, 0, 0)
    in_specs = [pl.BlockSpec((ts, qkv.shape[1]), blk), pl.BlockSpec((ts, 2 * GLA_RANK), blk),
                pl.BlockSpec(gw.shape, fix3), pl.BlockSpec(gb.shape, fix2)]
    args = [qkv, glr, gw, gb]
    if reverse:
        in_specs += [pl.BlockSpec((ts, D_MODEL), blk), pl.BlockSpec((ts, D_MODEL), blk),
                     pl.BlockSpec((1, D_MODEL), fix2)]
        args += [of, r, ng.reshape(1, D_MODEL)]
    kern = functools.partial(_gla_kernel, reverse=reverse, ts=ts, chunk=chunk)
    return pl.pallas_call(
        kern, grid=(B, nb), in_specs=in_specs,
        out_specs=pl.BlockSpec((ts, D_MODEL), blk),
        out_shape=jax.ShapeDtypeStruct((T, D_MODEL), BF16 if reverse else F32),
        scratch_shapes=[pltpu.VMEM((GLA_HEADS, GLA_DV, GLA_DK), F32),
                        pltpu.VMEM((ts, GLA_HEADS * GLA_DK), F32)],
        compiler_params=_params("parallel", "arbitrary"),
        name="gla_bwd" if reverse else "gla_fwd")(*args)


def _gla_layer(x, xb, B, S, w_in, gate_w, gate_b, norm_g, w_out, ln_g, ln_b):
    del xb
    qk = GLA_HEADS * GLA_DK
    w = w_in.astype(BF16)
    w_qkv = w[:, :2 * qk + D_MODEL]
    w_r = w[:, 2 * qk + D_MODEL:2 * qk + 2 * D_MODEL]
    w_glr = w[:, 2 * qk + 2 * D_MODEL:]
    qkv, r, glr = _dense(x, [w_qkv, w_r, w_glr], [BF16, F32, F32])
    gw = gate_w.astype(F32)
    gb = gate_b.astype(F32)
    of = _gla_pass(qkv, glr, gw, gb, B, S, False)
    a = _gla_pass(qkv, glr, gw, gb, B, S, True, of, r, norm_g.astype(F32))
    return _outproj_ln(a, w_out.astype(BF16), x, ln_g, ln_b)


def _gelu_tanh(x):
    return 0.5 * x * (1.0 + jnp.tanh(math.sqrt(2.0 / math.pi) * (x + 0.044715 * (x * x * x))))


def _lru_kernel(*refs, reverse, ts):
    if reverse:
        (u_ref, up_ref, un_ref, cw_ref, cb_ref, wg_ref, bg_ref, lam_ref, hf_ref, gate_ref,
         out_ref, a_ref, g_ref, h_ref, hs_ref) = refs
    else:
        (u_ref, up_ref, un_ref, cw_ref, cb_ref, wg_ref, bg_ref, lam_ref,
         out_ref, a_ref, g_ref, h_ref) = refs
        hs_ref = out_ref
    W = LRU_WIDTH
    j = pl.program_id(1)
    nb = pl.num_programs(1)
    jj = (nb - 1 - j) if reverse else j

    @pl.when(j == 0)
    def _():
        h_ref[...] = jnp.zeros_like(h_ref)

    z = u_ref[...]
    prev = jnp.where(jj > 0, up_ref[...], 0.0)
    nxt = jnp.where(jj < nb - 1, un_ref[...], 0.0)
    row = lax.broadcasted_iota(jnp.int32, (ts, W), 0)
    zm1 = jnp.where(row == 0, prev[7:8, :], pltpu.roll(z, 1, axis=0))
    zm2 = pltpu.roll(z, 2, axis=0)
    zm2 = jnp.where(row == 0, prev[6:7, :], jnp.where(row == 1, prev[7:8, :], zm2))
    zp1 = jnp.where(row == ts - 1, nxt[0:1, :], pltpu.roll(z, ts - 1, axis=0))
    u = cw_ref[0:1, :] * zm2 + cw_ref[1:2, :] * zm1 + cw_ref[2:3, :] * z + cw_ref[3:4, :] * zp1 + cb_ref[...]

    ls = LRU_C * _log_sigmoid(lam_ref[...])
    ub = u.astype(BF16)
    for n in range(LRU_BLOCKS):
        sl = slice(n * LRU_BW, (n + 1) * LRU_BW)
        pre = _dot(ub[:, sl], wg_ref[n]) + bg_ref[n]
        r = _sigmoid(pre[:, :LRU_BW])
        ig = _sigmoid(pre[:, LRU_BW:])
        log_a = r * ls[:, sl]
        a = jnp.exp(log_a)
        a_ref[:, sl] = a
        g_ref[:, sl] = jnp.sqrt(1.0 - a * a) * (ig * u[:, sl])

    n_tiles = ts // SUBLANES

    srow = lax.broadcasted_iota(jnp.int32, (SUBLANES, W), 0)
    carry_row = 0 if reverse else SUBLANES - 1

    def body(i, h):
        t = (n_tiles - 1 - i) if reverse else i
        r0 = pl.multiple_of(t * SUBLANES, SUBLANES)
        a8 = a_ref[pl.ds(r0, SUBLANES), :]
        g8 = g_ref[pl.ds(r0, SUBLANES), :]
        d = 1
        while d < SUBLANES:
            shift = SUBLANES - d if reverse else d
            keep = (srow < SUBLANES - d) if reverse else (srow >= d)
            g8 = g8 + a8 * jnp.where(keep, pltpu.roll(g8, shift, axis=0), 0.0)
            a8 = a8 * jnp.where(keep, pltpu.roll(a8, shift, axis=0), 1.0)
            d *= 2
        hs = g8 + a8 * h
        hs_ref[pl.ds(r0, SUBLANES), :] = hs
        return hs[carry_row:carry_row + 1, :]

    h_ref[...] = lax.fori_loop(0, n_tiles, body, h_ref[...], unroll=2)
    if reverse:
        out_ref[...] = (_gelu_tanh(gate_ref[...]) * (hf_ref[...] + hs_ref[...])).astype(out_ref.dtype)


def _lru_pass(u, cw, cb, wg, bg, lam, B, S, reverse, hf=None, gate=None):
    T = B * S
    W = LRU_WIDTH
    ts = min(SEQ_TILE, S)
    nb = S // ts
    tpb = ts // SUBLANES
    n8 = T // SUBLANES
    if reverse:
        seq = lambda b, j: b * nb + nb - 1 - j
    else:
        seq = lambda b, j: b * nb + j
    blk = lambda b, j: (seq(b, j), 0)
    blk_prev = lambda b, j: (jnp.maximum(seq(b, j) * tpb - 1, 0), 0)
    blk_next = lambda b, j: (jnp.minimum((seq(b, j) + 1) * tpb, n8 - 1), 0)
    fix2 = lambda b, j: (0, 0)
    fix3 = lambda b, j: (0, 0, 0)
    in_specs = [pl.BlockSpec((ts, W), blk), pl.BlockSpec((SUBLANES, W), blk_prev),
                pl.BlockSpec((SUBLANES, W), blk_next), pl.BlockSpec(cw.shape, fix2),
                pl.BlockSpec(cb.shape, fix2), pl.BlockSpec(wg.shape, fix3), pl.BlockSpec(bg.shape, fix3),
                pl.BlockSpec(lam.shape, fix2)]
    args = [u, u, u, cw, cb, wg, bg, lam]
    scratch = [pltpu.VMEM((ts, W), F32), pltpu.VMEM((ts, W), F32), pltpu.VMEM((1, W), F32)]
    if reverse:
        in_specs += [pl.BlockSpec((ts, W), blk), pl.BlockSpec((ts, W), blk)]
        args += [hf, gate]
        scratch.append(pltpu.VMEM((ts, W), F32))
    kern = functools.partial(_lru_kernel, reverse=reverse, ts=ts)
    return pl.pallas_call(
        kern, grid=(B, nb), in_specs=in_specs,
        out_specs=pl.BlockSpec((ts, W), blk),
        out_shape=jax.ShapeDtypeStruct((T, W), BF16 if reverse else F32),
        scratch_shapes=scratch,
        compiler_params=_params("parallel", "arbitrary"),
        name="lru_bwd" if reverse else "lru_fwd")(*args)


def _lru_layer(x, xb, B, S, w_in, conv_w, conv_b, gate_a_w, gate_a_b, gate_x_w, gate_x_b, lam, w_out,
               ln_g, ln_b):
    del xb
    W = LRU_WIDTH
    w = w_in.astype(BF16)
    gate, u = _dense(x, [w[:, :W], w[:, W:]], [F32, F32])
    cw = conv_w.astype(F32)
    cb = conv_b.astype(F32).reshape(1, W)
    passes = []
    for d in range(2):
        wg = jnp.concatenate([gate_a_w[d], gate_x_w[d]], axis=-1).astype(BF16)
        bg = jnp.concatenate([gate_a_b[d].reshape(LRU_BLOCKS, 1, LRU_BW),
                              gate_x_b[d].reshape(LRU_BLOCKS, 1, LRU_BW)], axis=-1).astype(F32)
        passes.append((wg, bg, lam[d].astype(F32).reshape(1, W)))
    hf = _lru_pass(u, cw, cb, *passes[0], B, S, False)
    a = _lru_pass(u, cw, cb, *passes[1], B, S, True, hf, gate)
    return _outproj_ln(a, w_out.astype(BF16), x, ln_g, ln_b)


MLA_HW = MLA_NOPE + LANES


def _mla_proj_kernel(x_ref, pos_ref, win_ref, qg_ref, kg_ref, wqn_ref, wqr_ref, wqs_ref, wkv_ref,
                     fr_ref, sg_ref, q_ref, k_ref, v_ref):
    H = MLA_HEADS
    scale = (MLA_NOPE + MLA_ROPE) ** -0.5 * math.log2(math.e)
    xb = x_ref[...].astype(BF16)
    z = _dot(xb, win_ref[...])
    cq = z[:, :MLA_Q_RANK]
    ckv = z[:, MLA_Q_RANK:MLA_Q_RANK + MLA_KV_RANK]
    kr = z[:, MLA_Q_RANK + MLA_KV_RANK:MLA_Q_RANK + MLA_KV_RANK + LANES]
    krs = z[:, MLA_Q_RANK + MLA_KV_RANK + LANES:]
    qn = (cq * lax.rsqrt(jnp.mean(cq * cq, axis=-1, keepdims=True) + LN_EPS) * qg_ref[...]).astype(BF16)
    kvn = (ckv * lax.rsqrt(jnp.mean(ckv * ckv, axis=-1, keepdims=True) + LN_EPS) * kg_ref[...]).astype(BF16)
    ang = pos_ref[...].astype(F32) * fr_ref[...]
    cosv = jnp.cos(ang)
    lane = lax.broadcasted_iota(jnp.int32, ang.shape, 1)
    cosv = jnp.where(lane < MLA_ROPE, cosv, 0.0)
    sinv = jnp.sin(ang) * sg_ref[...]
    k_rope = kr * cosv + krs * sinv
    kv = _dot(kvn, wkv_ref[...])
    q_nope = _dot(qn, wqn_ref[...])
    q_rope = _dot(qn, wqr_ref[...])
    q_swap = _dot(qn, wqs_ref[...])
    ones_col = jnp.where(lane == 0, 1.0, 0.0).astype(BF16)
    for h in range(H):
        a0 = h * MLA_HW
        q_ref[:, a0:a0 + MLA_NOPE] = (q_nope[:, h * MLA_NOPE:(h + 1) * MLA_NOPE] * scale).astype(BF16)
        qr = q_rope[:, h * LANES:(h + 1) * LANES] * cosv + q_swap[:, h * LANES:(h + 1) * LANES] * sinv
        q_ref[:, a0 + MLA_NOPE:a0 + MLA_HW] = (qr * scale).astype(BF16)
        k_ref[:, a0:a0 + MLA_NOPE] = kv[:, h * 2 * MLA_NOPE:h * 2 * MLA_NOPE + MLA_NOPE].astype(BF16)
        k_ref[:, a0 + MLA_NOPE:a0 + MLA_HW] = k_rope.astype(BF16)
        v_ref[:, 2 * h * MLA_DV:(2 * h + 1) * MLA_DV] = kv[:, h * 2 * MLA_NOPE + MLA_NOPE:(h + 1) * 2 * MLA_NOPE].astype(BF16)
        v_ref[:, (2 * h + 1) * MLA_DV:(2 * h + 2) * MLA_DV] = ones_col


def _attn_kernel(q_ref, k_ref, v_ref, o_ref, *, n_sub):
    k = k_ref[...]
    v = v_ref[...]
    rows = q_ref.shape[0] // n_sub
    for i in range(n_sub):
        s = _dot_nt(q_ref[i * rows:(i + 1) * rows, :], k)
        p = jnp.exp2(s - jnp.max(s, axis=-1, keepdims=True))
        acc = _dot(p.astype(BF16), v)
        o_ref[i * rows:(i + 1) * rows, :] = (acc[:, :MLA_DV] / acc[:, MLA_DV:MLA_DV + 1]).astype(o_ref.dtype)


def _pad_rope_cols(w, swap):
    half = MLA_ROPE // 2
    if swap:
        w = jnp.concatenate([w[..., half:], w[..., :half]], axis=-1)
    w = jnp.concatenate([w, jnp.zeros_like(w)], axis=-1)
    return w.reshape(w.shape[0], -1)


def _mla_layer(x, xb, B, S, positions, w_in, q_norm_g, kv_norm_g, w_uq, w_ukv, w_out, ln_g, ln_b):
    del xb
    T = B * S
    H = MLA_HEADS
    half = MLA_ROPE // 2
    w_kr = w_in[:, MLA_Q_RANK + MLA_KV_RANK:].reshape(D_MODEL, 1, MLA_ROPE)
    win = jnp.concatenate([w_in[:, :MLA_Q_RANK + MLA_KV_RANK], _pad_rope_cols(w_kr, False),
                           _pad_rope_cols(w_kr, True)], axis=1).astype(BF16)
    wq = w_uq.reshape(MLA_Q_RANK, H, MLA_NOPE + MLA_ROPE)
    wqn = wq[:, :, :MLA_NOPE].reshape(MLA_Q_RANK, H * MLA_NOPE).astype(BF16)
    wqr = _pad_rope_cols(wq[:, :, MLA_NOPE:], False).astype(BF16)
    wqs = _pad_rope_cols(wq[:, :, MLA_NOPE:], True).astype(BF16)
    freq = ROPE_THETA ** (-jnp.arange(half, dtype=F32) / half)
    zeros = jnp.zeros((LANES - MLA_ROPE,), F32)
    fr = jnp.concatenate([freq, freq, zeros]).reshape(1, LANES)
    sg = jnp.concatenate([-jnp.ones((half,), F32), jnp.ones((half,), F32), zeros]).reshape(1, LANES)
    tm = min(ROW_TILE, T)
    row = lambda i: (i, 0)
    fix = lambda i: (0, 0)
    ins = [x, positions.reshape(T, 1), win, q_norm_g.astype(F32).reshape(1, -1),
           kv_norm_g.astype(F32).reshape(1, -1), wqn, wqr, wqs, w_ukv.astype(BF16), fr, sg]
    in_specs = [pl.BlockSpec((tm, D_MODEL), row), pl.BlockSpec((tm, 1), row)]
    in_specs += [pl.BlockSpec(a.shape, fix) for a in ins[2:]]
    q, k, v = pl.pallas_call(
        _mla_proj_kernel, grid=(T // tm,), in_specs=in_specs,
        out_specs=[pl.BlockSpec((tm, H * MLA_HW), row), pl.BlockSpec((tm, H * MLA_HW), row),
                   pl.BlockSpec((tm, 2 * H * MLA_DV), row)],
        out_shape=[jax.ShapeDtypeStruct((T, H * MLA_HW), BF16), jax.ShapeDtypeStruct((T, H * MLA_HW), BF16),
                   jax.ShapeDtypeStruct((T, 2 * H * MLA_DV), BF16)],
        compiler_params=_params("parallel"), name="mla_proj")(*ins)
    tq = min(Q_TILE, S)
    nq = S // tq
    att = pl.pallas_call(
        functools.partial(_attn_kernel, n_sub=tq // min(Q_SUB, tq)), grid=(B, H, nq),
        in_specs=[pl.BlockSpec((tq, MLA_HW), lambda b, h, i: (b * nq + i, h)),
                  pl.BlockSpec((S, MLA_HW), lambda b, h, i: (b, h)),
                  pl.BlockSpec((S, 2 * MLA_DV), lambda b, h, i: (b, h))],
        out_specs=pl.BlockSpec((tq, MLA_DV), lambda b, h, i: (b * nq + i, h)),
        out_shape=jax.ShapeDtypeStruct((T, H * MLA_DV), BF16),
        compiler_params=_params("parallel", "parallel", "arbitrary"), name="mla_attn")(q, k, v)
    return _outproj_ln(att, w_out.astype(BF16), x, ln_g, ln_b)


def _router_kernel(x_ref, wh_ref, wl_ref, pos_ref, gate_ref, idx_ref, *, cap):
    E = N_EXPERTS
    x = x_ref[...]
    S = x.shape[0]
    xh = x.astype(BF16)
    xl = (x - xh.astype(F32)).astype(BF16)
    wh = wh_ref[...]
    both = _dot_nt(jnp.concatenate([wh, wl_ref[...]], axis=0), xh)
    logits = both[:E] + (both[E:] + _dot_nt(wh, xl))
    mx = jnp.max(logits, axis=0, keepdims=True)
    ex = jnp.exp(logits - mx)
    aff = ex / jnp.sum(ex, axis=0, keepdims=True)
    bits = pltpu.bitcast(aff, jnp.int32)

    def bit_step(i, thr):
        cand = thr | jnp.left_shift(jnp.int32(1), 30 - i)
        cnt = jnp.sum(jnp.where(bits >= cand, 1.0, 0.0), axis=1, keepdims=True)
        return jnp.where(cnt >= cap, cand, thr)

    thr = lax.fori_loop(0, 31, bit_step, jnp.zeros((E, 1), jnp.int32))
    gt = bits > thr
    eq = bits == thr
    need = cap - jnp.sum(jnp.where(gt, 1.0, 0.0), axis=1, keepdims=True)
    r = lax.broadcasted_iota(jnp.int32, (LANES, LANES), 0)
    c = lax.broadcasted_iota(jnp.int32, (LANES, LANES), 1)
    upper = jnp.where(r < c, 1.0, 0.0).astype(BF16)
    off = jnp.zeros((2 * E, 1), F32)
    for blk in range(S // LANES):
        sl = slice(blk * LANES, (blk + 1) * LANES)
        ind = jnp.concatenate([jnp.where(gt[:, sl], 1.0, 0.0), jnp.where(eq[:, sl], 1.0, 0.0)], axis=0)
        pre = _dot(ind.astype(BF16), upper) + off
        off = off + jnp.sum(ind, axis=1, keepdims=True)
        pg, pe = pre[:E], pre[E:]
        sel = gt[:, sl] | (eq[:, sl] & (pe < need))
        slot = pg + jnp.minimum(pe, need)
        pos_ref[:, sl] = jnp.where(sel, slot, -1.0).astype(jnp.int32)

    slots = lax.broadcasted_iota(jnp.int32, (cap, S), 0)
    token = lax.broadcasted_iota(jnp.int32, (1, S), 1)
    tok_hi = (token >> 6).astype(F32)
    tok_lo = (token & 63).astype(F32)
    pad = jnp.zeros((3, S), F32)
    for e in range(E):
        onehot = jnp.where(slots == pos_ref[e:e + 1, :], 1.0, 0.0).astype(BF16)
        a = aff[e:e + 1, :]
        a0 = a.astype(BF16).astype(F32)
        a1 = (a - a0).astype(BF16).astype(F32)
        a2 = a - a0 - a1
        vals = jnp.concatenate([tok_hi, tok_lo, a0, a1, a2, pad], axis=0).astype(BF16)
        res = _dot_nt(vals, onehot)
        idx_ref[e:e + 1, :] = (res[0:1, :] * 64.0 + res[1:2, :]).astype(jnp.int32)
        gate_ref[e:e + 1, :] = res[2:3, :] + res[3:4, :] + res[4:5, :]


def _gather_kernel(idx_ref, xp_ref, xin_ref, *, n_rows):
    def gather(t, carry):
        j0 = pl.multiple_of(t * SUBLANES, SUBLANES)
        rows = [xp_ref[pl.ds(idx_ref[0, 0, j0 + k], 1), :] for k in range(SUBLANES)]
        xin_ref[pl.ds(j0, SUBLANES), :] = jnp.concatenate(rows, axis=0)
        return carry

    lax.fori_loop(0, n_rows // SUBLANES, gather, 0, unroll=2)


def _ffn_kernel(xin_ref, gate_ref, wg_ref, wu_ref, wd_ref, out_ref, wgb_ref, wub_ref, wdb_ref):
    G, _, cap, half = xin_ref.shape

    @pl.when(pl.program_id(1) == 0)
    def _():
        wgb_ref[...] = wg_ref[0].astype(BF16)
        wub_ref[...] = wu_ref[0].astype(BF16)
        wdb_ref[...] = wd_ref[0].astype(BF16)

    words = xin_ref[...].reshape(G * cap, half)
    x_lo = pltpu.bitcast(words << 16, F32).astype(BF16)
    x_hi = pltpu.bitcast(words & jnp.uint32(0xFFFF0000), F32).astype(BF16)
    hg = _dot(x_lo, wgb_ref[:half, :]) + _dot(x_hi, wgb_ref[half:, :])
    hu = _dot(x_lo, wub_ref[:half, :]) + _dot(x_hi, wub_ref[half:, :])
    hmid = (hg * _sigmoid(hg) * hu).astype(BF16)
    out = _dot(hmid, wdb_ref[...])
    eye = (lax.broadcasted_iota(jnp.int32, (cap, cap), 0) == lax.broadcasted_iota(jnp.int32, (cap, cap), 1))
    for g in range(G):
        gate = jnp.sum(jnp.where(eye, gate_ref[g, 0], 0.0), axis=1, keepdims=True)
        out_ref[g, 0] = (out[g * cap:(g + 1) * cap, :] * gate).astype(out_ref.dtype)


def _combine_kernel(outs_ref, pos_ref, x_ref, g_ref, b_ref, o_ref, *, cap):
    E, ts = pos_ref.shape
    posf = pos_ref[...].astype(F32)
    pos_t = jnp.concatenate([posf, jnp.full((LANES - E, ts), -1.0, F32)], axis=0).T
    lane = lax.broadcasted_iota(jnp.int32, (ts, cap), 1).astype(F32)
    onehot = jnp.concatenate(
        [jnp.where(pos_t[:, e:e + 1] == lane, 1.0, 0.0).astype(BF16) for e in range(E)], axis=1)
    y = _dot(onehot, outs_ref[...])
    o_ref[...] = _layer_norm(ALPHA * x_ref[...] + y, g_ref[...], b_ref[...])


def _moe_layer(x, xb, B, S, w_router, w_gate, w_up, w_down, ln_g, ln_b):
    T = B * S
    E = N_EXPERTS
    D = D_MODEL
    cap = CAPACITY_FACTOR * S // E
    wr = w_router.astype(F32).T
    wh = wr.astype(BF16)
    wl = (wr - wh.astype(F32)).astype(BF16)
    pos, gate, idx = pl.pallas_call(
        functools.partial(_router_kernel, cap=cap), grid=(B,),
        in_specs=[pl.BlockSpec((S, D), lambda b: (b, 0)), pl.BlockSpec((E, D), lambda b: (0, 0)),
                  pl.BlockSpec((E, D), lambda b: (0, 0))],
        out_specs=[pl.BlockSpec((E, S), lambda b: (b, 0)), pl.BlockSpec((E, cap), lambda b: (b, 0)),
                   pl.BlockSpec((E, cap), lambda b: (b, 0))],
        out_shape=[jax.ShapeDtypeStruct((B * E, S), jnp.int32), jax.ShapeDtypeStruct((B * E, cap), F32),
                   jax.ShapeDtypeStruct((B * E, cap), jnp.int32)],
        compiler_params=_params("parallel"), name="moe_router")(x, wh, wl)
    xin = pl.pallas_call(
        functools.partial(_gather_kernel, n_rows=E * cap), grid=(B,),
        in_specs=[pl.BlockSpec((1, 1, E * cap), lambda b: (b, 0, 0), memory_space=pltpu.SMEM),
                  pl.BlockSpec((S, D // 2), lambda b: (b, 0))],
        out_specs=pl.BlockSpec((E * cap, D // 2), lambda b: (b, 0)),
        out_shape=jax.ShapeDtypeStruct((B * E * cap, D // 2), jnp.uint32),
        compiler_params=_params("parallel"), name="moe_gather")(idx.reshape(B, 1, E * cap), xb)
    ff = w_gate.shape[-1]
    G = math.gcd(B, FFN_GROUP)
    outs = pl.pallas_call(
        _ffn_kernel, grid=(E, B // G),
        in_specs=[pl.BlockSpec((G, 1, cap, D // 2), lambda e, b: (b, e, 0, 0)),
                  pl.BlockSpec((G, 1, 1, cap), lambda e, b: (b, e, 0, 0)),
                  pl.BlockSpec((1, D, ff), lambda e, b: (e, 0, 0)),
                  pl.BlockSpec((1, D, ff), lambda e, b: (e, 0, 0)),
                  pl.BlockSpec((1, ff, D), lambda e, b: (e, 0, 0))],
        out_specs=pl.BlockSpec((G, 1, cap, D), lambda e, b: (b, e, 0, 0)),
        out_shape=jax.ShapeDtypeStruct((B, E, cap, D), BF16),
        scratch_shapes=[pltpu.VMEM((D, ff), BF16), pltpu.VMEM((D, ff), BF16), pltpu.VMEM((ff, D), BF16)],
        compiler_params=_params("arbitrary", "arbitrary"), name="moe_ffn")(
            xin.reshape(B, E, cap, D // 2), gate.reshape(B, E, 1, cap), w_gate, w_up, w_down)
    ts = min(COMBINE_TILE, S)
    nb = S // ts
    return pl.pallas_call(
        functools.partial(_combine_kernel, cap=cap), grid=(B, nb),
        in_specs=[pl.BlockSpec((E * cap, D), lambda b, j: (b, 0)),
                  pl.BlockSpec((E, ts), lambda b, j: (b, j)),
                  pl.BlockSpec((ts, D), lambda b, j: (b * nb + j, 0)),
                  pl.BlockSpec((1, D), lambda b, j: (0, 0)), pl.BlockSpec((1, D), lambda b, j: (0, 0))],
        out_specs=pl.BlockSpec((ts, D), lambda b, j: (b * nb + j, 0)),
        out_shape=jax.ShapeDtypeStruct((T, D), F32),
        compiler_params=_params("parallel", "arbitrary"), name="moe_combine")(
            outs.reshape(B * E * cap, D), pos, x, ln_g.reshape(1, D), ln_b.reshape(1, D))


def kernel(x, positions, mlstm_w_in, mlstm_gate_b, mlstm_norm_g, mlstm_w_out, gla_w_in, gla_gate_w, gla_gate_b, gla_norm_g, gla_w_out, lru_w_in, lru_conv_w, lru_conv_b, lru_gate_a_w, lru_gate_a_b, lru_gate_x_w, lru_gate_x_b, lru_lambda, lru_w_out, mla_w_in, mla_q_norm_g, mla_kv_norm_g, mla_w_uq, mla_w_ukv, mla_w_out, moe_router, moe_w_gate, moe_w_up, moe_w_down, ln_g, ln_b):
    B, S, D = x.shape
    xf = x.reshape(B * S, D)
    xb = None
    for i in range(DEPTH):
        m = i % N_MIXERS
        j = i // N_MIXERS
        g0, b0 = ln_g[i, 0], ln_b[i, 0]
        if m == 0:
            xf, xb = _mlstm_layer(xf, xb, B, S, mlstm_w_in[j], mlstm_gate_b[j], mlstm_norm_g[j],
                                  mlstm_w_out[j], g0, b0)
        elif m == 1:
            xf, xb = _gla_layer(xf, xb, B, S, gla_w_in[j], gla_gate_w[j], gla_gate_b[j], gla_norm_g[j],
                                gla_w_out[j], g0, b0)
        elif m == 2:
            xf, xb = _lru_layer(xf, xb, B, S, lru_w_in[j], lru_conv_w[j], lru_conv_b[j], lru_gate_a_w[j],
                                lru_gate_a_b[j], lru_gate_x_w[j], lru_gate_x_b[j], lru_lambda[j],
                                lru_w_out[j], g0, b0)
        else:
            xf, xb = _mla_layer(xf, xb, B, S, positions, mla_w_in[j], mla_q_norm_g[j], mla_kv_norm_g[j],
                                mla_w_uq[j], mla_w_ukv[j], mla_w_out[j], g0, b0)
        xf = _moe_layer(xf, xb, B, S, moe_router[i], moe_w_gate[i], moe_w_up[i], moe_w_down[i],
                        ln_g[i, 1], ln_b[i, 1])
    return xf.reshape(B, S, D)
```

```python
import functools
import math

import jax
import jax.numpy as jnp
from jax import lax
from jax.experimental import pallas as pl
from jax.experimental.pallas import tpu as pltpu

F32 = jnp.float32
BF16 = jnp.bfloat16

D_MODEL = 1024
DEPTH = 4
N_MIXERS = 4
ALPHA = (2 * DEPTH) ** 0.25
LN_EPS = 1e-5

ML_HEADS = 4
ML_DV = D_MODEL // ML_HEADS
ML_DK = ML_DV // 2

GLA_HEADS = 4
GLA_DK = D_MODEL // 2 // GLA_HEADS
GLA_DV = D_MODEL // GLA_HEADS
GLA_RANK = 16
GLA_TAU = 16.0

LRU_WIDTH = D_MODEL
LRU_BLOCKS = 4
LRU_BW = LRU_WIDTH // LRU_BLOCKS
CONV_WIDTH = 4
LRU_C = 8.0

MLA_HEADS = 8
MLA_NOPE = 128
MLA_ROPE = 64
MLA_DV = 128
MLA_Q_RANK = 384
MLA_KV_RANK = 256
ROPE_THETA = 10000.0

N_EXPERTS = 16
CAPACITY_FACTOR = 2

V7X_VMEM_BYTES = 64 * 1024 * 1024
VMEM_LIMIT = V7X_VMEM_BYTES - 8 * 1024 * 1024
LANES = 128
SUBLANES = 8

ROW_TILE = 512
SEQ_TILE = 512
ML_CHUNK = 512
GLA_CHUNK = 128
Q_TILE = 2048
Q_SUB = 256
N_CHUNK = 512
FFN_GROUP = 4
COMBINE_TILE = 1024


def _params(*sem):
    return pltpu.CompilerParams(dimension_semantics=sem, vmem_limit_bytes=VMEM_LIMIT)


def _log_sigmoid(x):
    return jnp.minimum(x, 0.0) - jnp.log(1.0 + jnp.exp(-jnp.abs(x)))


def _sigmoid(x):
    return 0.5 * jnp.tanh(0.5 * x) + 0.5


def _layer_norm(v, g, b):
    mu = jnp.mean(v, axis=-1, keepdims=True)
    d = v - mu
    var = jnp.mean(d * d, axis=-1, keepdims=True)
    return d * lax.rsqrt(var + LN_EPS) * g + b


def _dot(a, b):
    return jnp.dot(a, b, preferred_element_type=F32)


def _dot_nt(a, b):
    return lax.dot_general(a, b, (((1,), (1,)), ((), ())), preferred_element_type=F32)


def _split3(a):
    a0 = a.astype(BF16)
    r1 = a - a0.astype(F32)
    a1 = r1.astype(BF16)
    a2 = (r1 - a1.astype(F32)).astype(BF16)
    return a0, a1, a2


def _dot_tn(a, b):
    return lax.dot_general(a, b, (((0,), (0,)), ((), ())), preferred_element_type=F32)


def _dense_kernel(*refs, n_w, n_t, has_bias):
    x_ref = refs[0]
    pos = 1
    w_refs = refs[pos:pos + n_w]
    pos += n_w
    b_refs = []
    for hb in has_bias:
        if hb:
            b_refs.append(refs[pos])
            pos += 1
        else:
            b_refs.append(None)
    t_refs = refs[pos:pos + 2 * n_t]
    pos += 2 * n_t
    o_refs = refs[pos:pos + n_w]
    pos += n_w
    ot_refs = refs[pos:pos + n_t]

    xb = x_ref[...].astype(BF16)
    for w_ref, b_ref, o_ref in zip(w_refs, b_refs, o_refs):
        n = w_ref.shape[1]
        for j0 in range(0, n, N_CHUNK):
            j1 = min(n, j0 + N_CHUNK)
            acc = _dot(xb, w_ref[:, j0:j1])
            if b_ref is not None:
                acc = acc + b_ref[:, j0:j1]
            o_ref[:, j0:j1] = acc.astype(o_ref.dtype)
    for i in range(n_t):
        wt_ref, bt_ref = t_refs[2 * i], t_refs[2 * i + 1]
        ot_refs[i][...] = _dot_nt(wt_ref[...], xb) + bt_ref[...]


def _dense(x, ws, dtypes, biases=None, transposed=()):
    T, K = x.shape
    tm = min(ROW_TILE, T)
    if biases is None:
        biases = [None] * len(ws)
    has_bias = tuple(b is not None for b in biases)
    args = [x] + list(ws) + [b for b in biases if b is not None]
    in_specs = [pl.BlockSpec((tm, K), lambda i: (i, 0))]
    in_specs += [pl.BlockSpec(w.shape, lambda i: (0, 0)) for w in ws]
    in_specs += [pl.BlockSpec(b.shape, lambda i: (0, 0)) for b in biases if b is not None]
    for wt, bt in transposed:
        args += [wt, bt]
        in_specs += [pl.BlockSpec(wt.shape, lambda i: (0, 0)), pl.BlockSpec(bt.shape, lambda i: (0, 0))]
    out_shape = [jax.ShapeDtypeStruct((T, w.shape[1]), dt) for w, dt in zip(ws, dtypes)]
    out_specs = [pl.BlockSpec((tm, w.shape[1]), lambda i: (i, 0)) for w in ws]
    for wt, _ in transposed:
        out_shape.append(jax.ShapeDtypeStruct((wt.shape[0], T), F32))
        out_specs.append(pl.BlockSpec((wt.shape[0], tm), lambda i: (0, i)))
    kern = functools.partial(_dense_kernel, n_w=len(ws), n_t=len(transposed), has_bias=has_bias)
    return pl.pallas_call(
        kern, grid=(T // tm,), in_specs=in_specs, out_specs=out_specs, out_shape=out_shape,
        compiler_params=_params("parallel"), name="dense")(*args)


def _outproj_ln_kernel(a_ref, w_ref, x_ref, g_ref, b_ref, o_ref, ob_ref):
    y = _dot(a_ref[...], w_ref[...])
    v = _layer_norm(ALPHA * x_ref[...] + y, g_ref[...], b_ref[...])
    o_ref[...] = v
    half = v.shape[1] // 2
    bits = pltpu.bitcast(v.astype(BF16).astype(F32), jnp.uint32)
    ob_ref[...] = (bits[:, :half] >> 16) | bits[:, half:]


def _outproj_ln(a, w, x, g, b):
    T, K = a.shape
    D = w.shape[1]
    tm = min(ROW_TILE, T)
    row = lambda i: (i, 0)
    fix = lambda i: (0, 0)
    return pl.pallas_call(
        _outproj_ln_kernel, grid=(T // tm,),
        in_specs=[pl.BlockSpec((tm, K), row), pl.BlockSpec((K, D), fix), pl.BlockSpec((tm, D), row),
                  pl.BlockSpec((1, D), fix), pl.BlockSpec((1, D), fix)],
        out_specs=[pl.BlockSpec((tm, D), row), pl.BlockSpec((tm, D // 2), row)],
        out_shape=[jax.ShapeDtypeStruct((T, D), F32), jax.ShapeDtypeStruct((T, D // 2), jnp.uint32)],
        compiler_params=_params("parallel"), name="outproj_ln")(a, w, x, g.reshape(1, D), b.reshape(1, D))


def _mlstm_kernel(*refs, reverse, ts, chunk):
    if reverse:
        qkv_ref, gr_ref, hf_ref, o_ref, ng_ref, out_ref, c_ref, n_ref, m_ref = refs
    else:
        qkv_ref, gr_ref, out_ref, c_ref, n_ref, m_ref = refs
    H, DK, DV, L = ML_HEADS, ML_DK, ML_DV, chunk
    scale = DK ** -0.5

    @pl.when(pl.program_id(1) == 0)
    def _():
        c_ref[...] = jnp.zeros_like(c_ref)
        n_ref[...] = jnp.zeros_like(n_ref)
        m_ref[...] = jnp.zeros_like(m_ref)

    rows = lax.broadcasted_iota(jnp.int32, (L, L), 0)
    cols = lax.broadcasted_iota(jnp.int32, (L, L), 1)
    mask = (cols >= rows) if reverse else (cols <= rows)
    eye = rows == cols
    tri = jnp.where((rows >= cols) if reverse else (rows <= cols), 1.0, 0.0).astype(BF16)
    d0 = 8 if reverse else 0
    last = 0 if reverse else L - 1
    n_chunks = ts // L
    order = range(n_chunks - 1, -1, -1) if reverse else range(n_chunks)
    neg_inf = -jnp.inf

    for c in order:
        r0 = c * L
        g8 = gr_ref[d0:d0 + 8, r0:r0 + L]
        lf8 = _log_sigmoid(g8)
        pieces = _dot(jnp.concatenate(_split3(lf8), axis=0), tri)
        b8 = pieces[0:8] + pieces[8:16] + pieces[16:24]
        u8 = g8[0:4, :] - b8[4:8, :]
        for h in range(H):
            qb = qkv_ref[r0:r0 + L, h * DK:(h + 1) * DK]
            kb = qkv_ref[r0:r0 + L, H * DK + h * DK:H * DK + (h + 1) * DK]
            vb = qkv_ref[r0:r0 + L, 2 * H * DK + h * DV:2 * H * DK + (h + 1) * DV]
            u_r = u8[h:h + 1, :]
            b_r = b8[4 + h:5 + h, :]
            m_prev = m_ref[h:h + 1, 0:1]
            u_c = jnp.sum(jnp.where(eye, u_r, 0.0), axis=1, keepdims=True)
            b_c = jnp.sum(jnp.where(eye, b_r, 0.0), axis=1, keepdims=True)
            um = jnp.where(mask, u_r, neg_inf)
            a_c = jnp.maximum(m_prev, jnp.max(um, axis=1, keepdims=True))
            dmat = jnp.exp(um - a_c)
            s = _dot_nt(qb, kb) * (scale * dmat)
            w_int = jnp.exp(m_prev - a_c) * scale
            c_old = c_ref[h]
            num = _dot(s.astype(BF16), vb) + w_int * _dot(qb, c_old.astype(BF16))
            qn = jnp.sum(qb.astype(F32) * n_ref[h], axis=1, keepdims=True)
            den = jnp.sum(s, axis=1, keepdims=True) + w_int * qn
            hh = num / jnp.maximum(jnp.abs(den), jnp.exp(-(a_c + b_c)))
            a_last = jnp.maximum(m_prev, jnp.max(u_r, axis=1, keepdims=True))
            g_tot = b_r[:, last:last + 1]
            ws_c = jnp.exp(u_c - a_last)
            wc = jnp.exp(m_prev - a_last)
            kw = kb.astype(F32) * ws_c
            c_ref[h] = wc * c_old + _dot_tn(kw.astype(BF16), vb)
            n_ref[h] = wc * n_ref[h] + jnp.sum(kw, axis=0, keepdims=True)
            m_ref[h:h + 1, :] = jnp.broadcast_to(g_tot + a_last, (1, LANES))
            if reverse:
                hs = hf_ref[r0:r0 + L, h * DV:(h + 1) * DV] + hh
                mu = jnp.mean(hs, axis=-1, keepdims=True)
                dd = hs - mu
                var = jnp.mean(dd * dd, axis=-1, keepdims=True)
                hn = dd * lax.rsqrt(var + LN_EPS) * ng_ref[:, h * DV:(h + 1) * DV]
                og = _sigmoid(o_ref[r0:r0 + L, h * DV:(h + 1) * DV])
                out_ref[r0:r0 + L, h * DV:(h + 1) * DV] = (og * hn).astype(out_ref.dtype)
            else:
                out_ref[r0:r0 + L, h * DV:(h + 1) * DV] = hh


def _mlstm_pass(qkv, gr, B, S, reverse, hf=None, o=None, ng=None):
    T = B * S
    ts = min(SEQ_TILE, S)
    chunk = min(ML_CHUNK, ts)
    nb = S // ts
    if reverse:
        blk = lambda b, j: (b * nb + nb - 1 - j, 0)
        blk_t = lambda b, j: (0, b * nb + nb - 1 - j)
    else:
        blk = lambda b, j: (b * nb + j, 0)
        blk_t = lambda b, j: (0, b * nb + j)
    wq = qkv.shape[1]
    in_specs = [pl.BlockSpec((ts, wq), blk), pl.BlockSpec((16, ts), blk_t)]
    args = [qkv, gr]
    if reverse:
        in_specs += [pl.BlockSpec((ts, D_MODEL), blk), pl.BlockSpec((ts, D_MODEL), blk),
                     pl.BlockSpec((1, D_MODEL), lambda b, j: (0, 0))]
        args += [hf, o, ng.reshape(1, D_MODEL)]
    out_dtype = BF16 if reverse else F32
    kern = functools.partial(_mlstm_kernel, reverse=reverse, ts=ts, chunk=chunk)
    return pl.pallas_call(
        kern, grid=(B, nb), in_specs=in_specs,
        out_specs=pl.BlockSpec((ts, D_MODEL), blk),
        out_shape=jax.ShapeDtypeStruct((T, D_MODEL), out_dtype),
        scratch_shapes=[pltpu.VMEM((ML_HEADS, ML_DK, ML_DV), F32),
                        pltpu.VMEM((ML_HEADS, 1, ML_DK), F32),
                        pltpu.VMEM((SUBLANES, LANES), F32)],
        compiler_params=_params("parallel", "arbitrary"),
        name="mlstm_bwd" if reverse else "mlstm_fwd")(*args)


def _mlstm_layer(x, xb, B, S, w_in, gate_b, norm_g, w_out, ln_g, ln_b):
    del xb
    qk = ML_HEADS * ML_DK
    w = w_in.astype(BF16)
    w_qkv = w[:, :2 * qk + D_MODEL]
    w_o = w[:, 2 * qk + D_MODEL:2 * qk + 2 * D_MODEL]
    w_g_t = w[:, 2 * qk + 2 * D_MODEL:].T
    b_g = gate_b.astype(F32).reshape(16, 1)
    qkv, o, gr = _dense(x, [w_qkv, w_o], [BF16, F32], transposed=[(w_g_t, b_g)])
    hf = _mlstm_pass(qkv, gr, B, S, False)
    a = _mlstm_pass(qkv, gr, B, S, True, hf, o, norm_g.astype(F32))
    return _outproj_ln(a, w_out.astype(BF16), x, ln_g, ln_b)


def _cumsum_rows(x, n, reverse):
    row = lax.broadcasted_iota(jnp.int32, x.shape, 0)
    sh = 1
    while sh < n:
        if reverse:
            x = x + jnp.where(row < n - sh, pltpu.roll(x, n - sh, axis=0), 0.0)
        else:
            x = x + jnp.where(row >= sh, pltpu.roll(x, sh, axis=0), 0.0)
        sh *= 2
    return x


def _gla_kernel(*refs, reverse, ts, chunk):
    if reverse:
        qkv_ref, glr_ref, gw_ref, gb_ref, of_ref, r_ref, ng_ref, out_ref, st_ref, la_ref = refs
    else:
        qkv_ref, glr_ref, gw_ref, gb_ref, out_ref, st_ref, la_ref = refs
    H, DK, DV, L = GLA_HEADS, GLA_DK, GLA_DV, chunk
    scale = DK ** -0.5
    d = 1 if reverse else 0

    @pl.when(pl.program_id(1) == 0)
    def _():
        st_ref[...] = jnp.zeros_like(st_ref)

    glr = glr_ref[:, d * GLA_RANK:(d + 1) * GLA_RANK]
    g0, g1, _ = _split3(glr)
    pre = _dot(jnp.concatenate([g0, g1, g0], axis=1), gw_ref[d]) + gb_ref[d:d + 1, :]
    la_ref[...] = _log_sigmoid(pre) * (1.0 / GLA_TAU)

    rows = lax.broadcasted_iota(jnp.int32, (L, L), 0)
    cols = lax.broadcasted_iota(jnp.int32, (L, L), 1)
    mask = (cols >= rows) if reverse else (cols <= rows)
    last = 0 if reverse else L - 1
    mid = L // 2
    n_chunks = ts // L

    def body(i, carry):
        c = (n_chunks - 1 - i) if reverse else i
        r0 = pl.multiple_of(c * L, L)
        bsum = _cumsum_rows(la_ref[pl.ds(r0, L), :], L, reverse)
        for h in range(H):
            b = bsum[:, h * DK:(h + 1) * DK]
            qf = qkv_ref[pl.ds(r0, L), h * DK:(h + 1) * DK].astype(F32)
            kf = qkv_ref[pl.ds(r0, L), H * DK + h * DK:H * DK + (h + 1) * DK].astype(F32)
            vb = qkv_ref[pl.ds(r0, L), 2 * H * DK + h * DV:2 * H * DK + (h + 1) * DV]
            beta = b[mid:mid + 1, :]
            g = b[last:last + 1, :]
            qt = (qf * jnp.exp(b - beta)).astype(BF16)
            kt = (kf * jnp.exp(beta - b)).astype(BF16)
            amat = jnp.where(mask, _dot_nt(qt, kt) * scale, 0.0)
            qh = (qf * (jnp.exp(b) * scale)).astype(BF16)
            st = st_ref[h]
            o = _dot(amat.astype(BF16), vb) + _dot_nt(qh, st.astype(BF16))
            kh = (kf * jnp.exp(g - b)).astype(BF16)
            st_ref[h] = st * jnp.exp(g) + _dot_tn(vb, kh)
            if reverse:
                hs = of_ref[pl.ds(r0, L), h * DV:(h + 1) * DV] + o
                mu = jnp.mean(hs, axis=-1, keepdims=True)
                dd = hs - mu
                var = jnp.mean(dd * dd, axis=-1, keepdims=True)
                hn = dd * lax.rsqrt(var + LN_EPS) * ng_ref[:, h * DV:(h + 1) * DV]
                rr = r_ref[pl.ds(r0, L), h * DV:(h + 1) * DV]
                out_ref[pl.ds(r0, L), h * DV:(h + 1) * DV] = (rr * _sigmoid(rr) * hn).astype(out_ref.dtype)
            else:
                out_ref[pl.ds(r0, L), h * DV:(h + 1) * DV] = o
        return carry

    lax.fori_loop(0, n_chunks, body, 0)


def _gla_pass(qkv, glr, gw, gb, B, S, reverse, of=None, r=None, ng=None):
    T = B * S
    ts = min(SEQ_TILE, S)
    chunk = min(GLA_CHUNK, ts)
    nb = S // ts
    if reverse:
        blk = lambda b, j: (b * nb + nb - 1 - j, 0)
    else:
        blk = lambda b, j: (b * nb + j, 0)
    fix2 = lambda b, j: (0, 0)
    fix3 = lambda b, j: (0, 0, 0)
    in_specs = [pl.BlockSpec((ts, qkv.shape[1]), blk), pl.BlockSpec((ts, 2 * GLA_RANK), blk),
                pl.BlockSpec(gw.shape, fix3), pl.BlockSpec(gb.shape, fix2)]
    args = [qkv, glr, gw, gb]
    if reverse:
        in_specs += [pl.BlockSpec((ts, D_MODEL), blk), pl.BlockSpec((ts, D_MODEL), blk),
                     pl.BlockSpec((1, D_MODEL), fix2)]
        args += [of, r, ng.reshape(1, D_MODEL)]
    kern = functools.partial(_gla_kernel, reverse=reverse, ts=ts, chunk=chunk)
    return pl.pallas_call(
        kern, grid=(B, nb), in_specs=in_specs,
        out_specs=pl.BlockSpec((ts, D_MODEL), blk),
        out_shape=jax.ShapeDtypeStruct((T, D_MODEL), BF16 if reverse else F32),
        scratch_shapes=[pltpu.VMEM((GLA_HEADS, GLA_DV, GLA_DK), F32),
                        pltpu.VMEM((ts, GLA_HEADS * GLA_DK), F32)],
        compiler_params=_params("parallel", "arbitrary"),
        name="gla_bwd" if reverse else "gla_fwd")(*args)


def _gla_layer(x, xb, B, S, w_in, gate_w, gate_b, norm_g, w_out, ln_g, ln_b):
    del xb
    qk = GLA_HEADS * GLA_DK
    w = w_in.astype(BF16)
    w_qkv = w[:, :2 * qk + D_MODEL]
    w_r = w[:, 2 * qk + D_MODEL:2 * qk + 2 * D_MODEL]
    w_glr = w[:, 2 * qk + 2 * D_MODEL:]
    qkv, r, glr = _dense(x, [w_qkv, w_r, w_glr], [BF16, F32, F32])
    gw0 = gate_w.astype(BF16)
    gw1 = (gate_w.astype(F32) - gw0.astype(F32)).astype(BF16)
    gw = jnp.concatenate([gw0, gw0, gw1], axis=1)
    gb = gate_b.astype(F32)
    of = _gla_pass(qkv, glr, gw, gb, B, S, False)
    a = _gla_pass(qkv, glr, gw, gb, B, S, True, of, r, norm_g.astype(F32))
    return _outproj_ln(a, w_out.astype(BF16), x, ln_g, ln_b)


def _gelu_tanh(x):
    return 0.5 * x * (1.0 + jnp.tanh(math.sqrt(2.0 / math.pi) * (x + 0.044715 * (x * x * x))))


def _lru_kernel(*refs, reverse, ts):
    if reverse:
        (u_ref, up_ref, un_ref, cw_ref, cb_ref, wg_ref, bg_ref, lam_ref, hf_ref, gate_ref,
         out_ref, a_ref, g_ref, h_ref, hs_ref) = refs
    else:
        (u_ref, up_ref, un_ref, cw_ref, cb_ref, wg_ref, bg_ref, lam_ref,
         out_ref, a_ref, g_ref, h_ref) = refs
        hs_ref = out_ref
    W = LRU_WIDTH
    j = pl.program_id(1)
    nb = pl.num_programs(1)
    jj = (nb - 1 - j) if reverse else j

    @pl.when(j == 0)
    def _():
        h_ref[...] = jnp.zeros_like(h_ref)

    z = u_ref[...]
    prev = jnp.where(jj > 0, up_ref[...], 0.0)
    nxt = jnp.where(jj < nb - 1, un_ref[...], 0.0)
    row = lax.broadcasted_iota(jnp.int32, (ts, W), 0)
    zm1 = jnp.where(row == 0, prev[7:8, :], pltpu.roll(z, 1, axis=0))
    zm2 = pltpu.roll(z, 2, axis=0)
    zm2 = jnp.where(row == 0, prev[6:7, :], jnp.where(row == 1, prev[7:8, :], zm2))
    zp1 = jnp.where(row == ts - 1, nxt[0:1, :], pltpu.roll(z, ts - 1, axis=0))
    u = cw_ref[0:1, :] * zm2 + cw_ref[1:2, :] * zm1 + cw_ref[2:3, :] * z + cw_ref[3:4, :] * zp1 + cb_ref[...]

    ls = LRU_C * _log_sigmoid(lam_ref[...])
    ub = u.astype(BF16)
    for n in range(LRU_BLOCKS):
        sl = slice(n * LRU_BW, (n + 1) * LRU_BW)
        pre = _dot(ub[:, sl], wg_ref[n]) + bg_ref[n]
        r = _sigmoid(pre[:, :LRU_BW])
        ig = _sigmoid(pre[:, LRU_BW:])
        log_a = r * ls[:, sl]
        a = jnp.exp(log_a)
        a_ref[:, sl] = a
        g_ref[:, sl] = jnp.sqrt(1.0 - a * a) * (ig * u[:, sl])

    n_tiles = ts // SUBLANES

    srow = lax.broadcasted_iota(jnp.int32, (SUBLANES, W), 0)
    carry_row = 0 if reverse else SUBLANES - 1

    def body(i, h):
        t = (n_tiles - 1 - i) if reverse else i
        r0 = pl.multiple_of(t * SUBLANES, SUBLANES)
        a8 = a_ref[pl.ds(r0, SUBLANES), :]
        g8 = g_ref[pl.ds(r0, SUBLANES), :]
        d = 1
        while d < SUBLANES:
            shift = SUBLANES - d if reverse else d
            keep = (srow < SUBLANES - d) if reverse else (srow >= d)
            g8 = g8 + a8 * jnp.where(keep, pltpu.roll(g8, shift, axis=0), 0.0)
            a8 = a8 * jnp.where(keep, pltpu.roll(a8, shift, axis=0), 1.0)
            d *= 2
        hs = g8 + a8 * h
        hs_ref[pl.ds(r0, SUBLANES), :] = hs
        return hs[carry_row:carry_row + 1, :]

    h_ref[...] = lax.fori_loop(0, n_tiles, body, h_ref[...], unroll=2)
    if reverse:
        out_ref[...] = (_gelu_tanh(gate_ref[...]) * (hf_ref[...] + hs_ref[...])).astype(out_ref.dtype)


def _lru_pass(u, cw, cb, wg, bg, lam, B, S, reverse, hf=None, gate=None):
    T = B * S
    W = LRU_WIDTH
    ts = min(SEQ_TILE, S)
    nb = S // ts
    tpb = ts // SUBLANES
    n8 = T // SUBLANES
    if reverse:
        seq = lambda b, j: b * nb + nb - 1 - j
    else:
        seq = lambda b, j: b * nb + j
    blk = lambda b, j: (seq(b, j), 0)
    blk_prev = lambda b, j: (jnp.maximum(seq(b, j) * tpb - 1, 0), 0)
    blk_next = lambda b, j: (jnp.minimum((seq(b, j) + 1) * tpb, n8 - 1), 0)
    fix2 = lambda b, j: (0, 0)
    fix3 = lambda b, j: (0, 0, 0)
    in_specs = [pl.BlockSpec((ts, W), blk), pl.BlockSpec((SUBLANES, W), blk_prev),
                pl.BlockSpec((SUBLANES, W), blk_next), pl.BlockSpec(cw.shape, fix2),
                pl.BlockSpec(cb.shape, fix2), pl.BlockSpec(wg.shape, fix3), pl.BlockSpec(bg.shape, fix3),
                pl.BlockSpec(lam.shape, fix2)]
    args = [u, u, u, cw, cb, wg, bg, lam]
    scratch = [pltpu.VMEM((ts, W), F32), pltpu.VMEM((ts, W), F32), pltpu.VMEM((1, W), F32)]
    if reverse:
        in_specs += [pl.BlockSpec((ts, W), blk), pl.BlockSpec((ts, W), blk)]
        args += [hf, gate]
        scratch.append(pltpu.VMEM((ts, W), F32))
    kern = functools.partial(_lru_kernel, reverse=reverse, ts=ts)
    return pl.pallas_call(
        kern, grid=(B, nb), in_specs=in_specs,
        out_specs=pl.BlockSpec((ts, W), blk),
        out_shape=jax.ShapeDtypeStruct((T, W), BF16 if reverse else F32),
        scratch_shapes=scratch,
        compiler_params=_params("parallel", "arbitrary"),
        name="lru_bwd" if reverse else "lru_fwd")(*args)


def _lru_layer(x, xb, B, S, w_in, conv_w, conv_b, gate_a_w, gate_a_b, gate_x_w, gate_x_b, lam, w_out,
               ln_g, ln_b):
    del xb
    W = LRU_WIDTH
    w = w_in.astype(BF16)
    gate, u = _dense(x, [w[:, :W], w[:, W:]], [F32, F32])
    cw = conv_w.astype(F32)
    cb = conv_b.astype(F32).reshape(1, W)
    passes = []
    for d in range(2):
        wg = jnp.concatenate([gate_a_w[d], gate_x_w[d]], axis=-1).astype(BF16)
        bg = jnp.concatenate([gate_a_b[d].reshape(LRU_BLOCKS, 1, LRU_BW),
                              gate_x_b[d].reshape(LRU_BLOCKS, 1, LRU_BW)], axis=-1).astype(F32)
        passes.append((wg, bg, lam[d].astype(F32).reshape(1, W)))
    hf = _lru_pass(u, cw, cb, *passes[0], B, S, False)
    a = _lru_pass(u, cw, cb, *passes[1], B, S, True, hf, gate)
    return _outproj_ln(a, w_out.astype(BF16), x, ln_g, ln_b)


MLA_HW = MLA_NOPE + LANES


def _mla_proj_kernel(x_ref, pos_ref, win_ref, qg_ref, kg_ref, wqn_ref, wqr_ref, wqs_ref, wkv_ref,
                     fr_ref, sg_ref, q_ref, k_ref, v_ref):
    H = MLA_HEADS
    scale = (MLA_NOPE + MLA_ROPE) ** -0.5 * math.log2(math.e)
    xb = x_ref[...].astype(BF16)
    z = _dot(xb, win_ref[...])
    cq = z[:, :MLA_Q_RANK]
    ckv = z[:, MLA_Q_RANK:MLA_Q_RANK + MLA_KV_RANK]
    kr = z[:, MLA_Q_RANK + MLA_KV_RANK:MLA_Q_RANK + MLA_KV_RANK + LANES]
    krs = z[:, MLA_Q_RANK + MLA_KV_RANK + LANES:]
    qn = (cq * lax.rsqrt(jnp.mean(cq * cq, axis=-1, keepdims=True) + LN_EPS) * qg_ref[...]).astype(BF16)
    kvn = (ckv * lax.rsqrt(jnp.mean(ckv * ckv, axis=-1, keepdims=True) + LN_EPS) * kg_ref[...]).astype(BF16)
    ang = pos_ref[...].astype(F32) * fr_ref[...]
    cosv = jnp.cos(ang)
    lane = lax.broadcasted_iota(jnp.int32, ang.shape, 1)
    cosv = jnp.where(lane < MLA_ROPE, cosv, 0.0)
    sinv = jnp.sin(ang) * sg_ref[...]
    k_rope = kr * cosv + krs * sinv
    kv = _dot(kvn, wkv_ref[...])
    q_nope = _dot(qn, wqn_ref[...])
    q_rope = _dot(qn, wqr_ref[...])
    q_swap = _dot(qn, wqs_ref[...])
    ones_col = jnp.where(lane == 0, 1.0, 0.0).astype(BF16)
    for h in range(H):
        a0 = h * MLA_HW
        q_ref[:, a0:a0 + MLA_NOPE] = (q_nope[:, h * MLA_NOPE:(h + 1) * MLA_NOPE] * scale).astype(BF16)
        qr = q_rope[:, h * LANES:(h + 1) * LANES] * cosv + q_swap[:, h * LANES:(h + 1) * LANES] * sinv
        q_ref[:, a0 + MLA_NOPE:a0 + MLA_HW] = (qr * scale).astype(BF16)
        k_ref[:, a0:a0 + MLA_NOPE] = kv[:, h * 2 * MLA_NOPE:h * 2 * MLA_NOPE + MLA_NOPE].astype(BF16)
        k_ref[:, a0 + MLA_NOPE:a0 + MLA_HW] = k_rope.astype(BF16)
        v_ref[:, 2 * h * MLA_DV:(2 * h + 1) * MLA_DV] = kv[:, h * 2 * MLA_NOPE + MLA_NOPE:(h + 1) * 2 * MLA_NOPE].astype(BF16)
        v_ref[:, (2 * h + 1) * MLA_DV:(2 * h + 2) * MLA_DV] = ones_col


def _attn_kernel(q_ref, k_ref, v_ref, o_ref, *, n_sub):
    k = k_ref[...]
    v = v_ref[...]
    rows = q_ref.shape[0] // n_sub
    for i in range(n_sub):
        s = _dot_nt(q_ref[i * rows:(i + 1) * rows, :], k)
        p = jnp.exp2(s - jnp.max(s, axis=-1, keepdims=True))
        acc = _dot(p.astype(BF16), v)
        o_ref[i * rows:(i + 1) * rows, :] = (acc[:, :MLA_DV] / acc[:, MLA_DV:MLA_DV + 1]).astype(o_ref.dtype)


def _pad_rope_cols(w, swap):
    half = MLA_ROPE // 2
    if swap:
        w = jnp.concatenate([w[..., half:], w[..., :half]], axis=-1)
    w = jnp.concatenate([w, jnp.zeros_like(w)], axis=-1)
    return w.reshape(w.shape[0], -1)


def _mla_layer(x, xb, B, S, positions, w_in, q_norm_g, kv_norm_g, w_uq, w_ukv, w_out, ln_g, ln_b):
    del xb
    T = B * S
    H = MLA_HEADS
    half = MLA_ROPE // 2
    w_kr = w_in[:, MLA_Q_RANK + MLA_KV_RANK:].reshape(D_MODEL, 1, MLA_ROPE)
    win = jnp.concatenate([w_in[:, :MLA_Q_RANK + MLA_KV_RANK], _pad_rope_cols(w_kr, False),
                           _pad_rope_cols(w_kr, True)], axis=1).astype(BF16)
    wq = w_uq.reshape(MLA_Q_RANK, H, MLA_NOPE + MLA_ROPE)
    wqn = wq[:, :, :MLA_NOPE].reshape(MLA_Q_RANK, H * MLA_NOPE).astype(BF16)
    wqr = _pad_rope_cols(wq[:, :, MLA_NOPE:], False).astype(BF16)
    wqs = _pad_rope_cols(wq[:, :, MLA_NOPE:], True).astype(BF16)
    freq = ROPE_THETA ** (-jnp.arange(half, dtype=F32) / half)
    zeros = jnp.zeros((LANES - MLA_ROPE,), F32)
    fr = jnp.concatenate([freq, freq, zeros]).reshape(1, LANES)
    sg = jnp.concatenate([-jnp.ones((half,), F32), jnp.ones((half,), F32), zeros]).reshape(1, LANES)
    tm = min(ROW_TILE, T)
    row = lambda i: (i, 0)
    fix = lambda i: (0, 0)
    ins = [x, positions.reshape(T, 1), win, q_norm_g.astype(F32).reshape(1, -1),
           kv_norm_g.astype(F32).reshape(1, -1), wqn, wqr, wqs, w_ukv.astype(BF16), fr, sg]
    in_specs = [pl.BlockSpec((tm, D_MODEL), row), pl.BlockSpec((tm, 1), row)]
    in_specs += [pl.BlockSpec(a.shape, fix) for a in ins[2:]]
    q, k, v = pl.pallas_call(
        _mla_proj_kernel, grid=(T // tm,), in_specs=in_specs,
        out_specs=[pl.BlockSpec((tm, H * MLA_HW), row), pl.BlockSpec((tm, H * MLA_HW), row),
                   pl.BlockSpec((tm, 2 * H * MLA_DV), row)],
        out_shape=[jax.ShapeDtypeStruct((T, H * MLA_HW), BF16), jax.ShapeDtypeStruct((T, H * MLA_HW), BF16),
                   jax.ShapeDtypeStruct((T, 2 * H * MLA_DV), BF16)],
        compiler_params=_params("parallel"), name="mla_proj")(*ins)
    tq = min(Q_TILE, S)
    nq = S // tq
    att = pl.pallas_call(
        functools.partial(_attn_kernel, n_sub=tq // min(Q_SUB, tq)), grid=(B, H, nq),
        in_specs=[pl.BlockSpec((tq, MLA_HW), lambda b, h, i: (b * nq + i, h)),
                  pl.BlockSpec((S, MLA_HW), lambda b, h, i: (b, h)),
                  pl.BlockSpec((S, 2 * MLA_DV), lambda b, h, i: (b, h))],
        out_specs=pl.BlockSpec((tq, MLA_DV), lambda b, h, i: (b * nq + i, h)),
        out_shape=jax.ShapeDtypeStruct((T, H * MLA_DV), BF16),
        compiler_params=_params("parallel", "parallel", "arbitrary"), name="mla_attn")(q, k, v)
    return _outproj_ln(att, w_out.astype(BF16), x, ln_g, ln_b)


def _router_kernel(x_ref, wh_ref, wl_ref, pos_ref, gate_ref, idx_ref, *, cap):
    E = N_EXPERTS
    x = x_ref[...]
    S = x.shape[0]
    xh = x.astype(BF16)
    xl = (x - xh.astype(F32)).astype(BF16)
    wh = wh_ref[...]
    both = _dot_nt(jnp.concatenate([wh, wl_ref[...]], axis=0), xh)
    logits = both[:E] + (both[E:] + _dot_nt(wh, xl))
    mx = jnp.max(logits, axis=0, keepdims=True)
    ex = jnp.exp(logits - mx)
    aff = ex / jnp.sum(ex, axis=0, keepdims=True)
    bits = pltpu.bitcast(aff, jnp.int32)

    def bit_step(i, thr):
        cand = thr | jnp.left_shift(jnp.int32(1), 30 - i)
        cnt = jnp.sum(jnp.where(bits >= cand, 1.0, 0.0), axis=1, keepdims=True)
        return jnp.where(cnt >= cap, cand, thr)

    thr = lax.fori_loop(0, 31, bit_step, jnp.zeros((E, 1), jnp.int32))
    gt = bits > thr
    eq = bits == thr
    need = cap - jnp.sum(jnp.where(gt, 1.0, 0.0), axis=1, keepdims=True)
    r = lax.broadcasted_iota(jnp.int32, (LANES, LANES), 0)
    c = lax.broadcasted_iota(jnp.int32, (LANES, LANES), 1)
    upper = jnp.where(r < c, 1.0, 0.0).astype(BF16)
    off = jnp.zeros((2 * E, 1), F32)
    for blk in range(S // LANES):
        sl = slice(blk * LANES, (blk + 1) * LANES)
        ind = jnp.concatenate([jnp.where(gt[:, sl], 1.0, 0.0), jnp.where(eq[:, sl], 1.0, 0.0)], axis=0)
        pre = _dot(ind.astype(BF16), upper) + off
        off = off + jnp.sum(ind, axis=1, keepdims=True)
        pg, pe = pre[:E], pre[E:]
        sel = gt[:, sl] | (eq[:, sl] & (pe < need))
        slot = pg + jnp.minimum(pe, need)
        pos_ref[:, sl] = jnp.where(sel, slot, -1.0).astype(jnp.int32)

    slots = lax.broadcasted_iota(jnp.int32, (cap, S), 0)
    token = lax.broadcasted_iota(jnp.int32, (1, S), 1)
    tok_hi = (token >> 6).astype(F32)
    tok_lo = (token & 63).astype(F32)
    pad = jnp.zeros((3, S), F32)
    for e in range(E):
        onehot = jnp.where(slots == pos_ref[e:e + 1, :], 1.0, 0.0).astype(BF16)
        a = aff[e:e + 1, :]
        a0 = a.astype(BF16).astype(F32)
        a1 = (a - a0).astype(BF16).astype(F32)
        a2 = a - a0 - a1
        vals = jnp.concatenate([tok_hi, tok_lo, a0, a1, a2, pad], axis=0).astype(BF16)
        res = _dot_nt(vals, onehot)
        idx_ref[e:e + 1, :] = (res[0:1, :] * 64.0 + res[1:2, :]).astype(jnp.int32)
        gate_ref[e:e + 1, :] = res[2:3, :] + res[3:4, :] + res[4:5, :]


def _gather_kernel(idx_ref, xp_ref, xin_ref, *, n_rows):
    def gather(t, carry):
        j0 = pl.multiple_of(t * SUBLANES, SUBLANES)
        rows = [xp_ref[pl.ds(idx_ref[0, 0, j0 + k], 1), :] for k in range(SUBLANES)]
        xin_ref[pl.ds(j0, SUBLANES), :] = jnp.concatenate(rows, axis=0)
        return carry

    lax.fori_loop(0, n_rows // SUBLANES, gather, 0, unroll=2)


def _ffn_kernel(xin_ref, gate_ref, wg_ref, wu_ref, wd_ref, out_ref, wgb_ref, wub_ref, wdb_ref):
    G, _, cap, half = xin_ref.shape

    @pl.when(pl.program_id(1) == 0)
    def _():
        wgb_ref[...] = wg_ref[0, 0].astype(BF16)
        wub_ref[...] = wu_ref[0, 0].astype(BF16)
        wdb_ref[...] = wd_ref[0, 0].astype(BF16)

    words = xin_ref[...].reshape(G * cap, half)
    x_lo = pltpu.bitcast(words << 16, F32).astype(BF16)
    x_hi = pltpu.bitcast(words & jnp.uint32(0xFFFF0000), F32).astype(BF16)
    hg = _dot(x_lo, wgb_ref[:half, :]) + _dot(x_hi, wgb_ref[half:, :])
    hu = _dot(x_lo, wub_ref[:half, :]) + _dot(x_hi, wub_ref[half:, :])
    hmid = (hg * _sigmoid(hg) * hu).astype(BF16)
    out = _dot(hmid, wdb_ref[...])
    eye = (lax.broadcasted_iota(jnp.int32, (cap, cap), 0) == lax.broadcasted_iota(jnp.int32, (cap, cap), 1))
    for g in range(G):
        gate = jnp.sum(jnp.where(eye, gate_ref[g, 0], 0.0), axis=1, keepdims=True)
        out_ref[g, 0] = (out[g * cap:(g + 1) * cap, :] * gate).astype(out_ref.dtype)


def _combine_kernel(outs_ref, pos_ref, x_ref, g_ref, b_ref, o_ref, *, cap):
    E, ts = pos_ref.shape
    posf = pos_ref[...].astype(F32)
    pos_t = jnp.concatenate([posf, jnp.full((LANES - E, ts), -1.0, F32)], axis=0).T
    lane = lax.broadcasted_iota(jnp.int32, (ts, cap), 1).astype(F32)
    onehot = jnp.concatenate(
        [jnp.where(pos_t[:, e:e + 1] == lane, 1.0, 0.0).astype(BF16) for e in range(E)], axis=1)
    y = _dot(onehot, outs_ref[...])
    o_ref[...] = _layer_norm(ALPHA * x_ref[...] + y, g_ref[...], b_ref[...])


def _moe_layer(x, xb, B, S, layer, w_router, w_gate, w_up, w_down, ln_g, ln_b):
    T = B * S
    E = N_EXPERTS
    D = D_MODEL
    cap = CAPACITY_FACTOR * S // E
    wr = w_router.astype(F32).T
    wh = wr.astype(BF16)
    wl = (wr - wh.astype(F32)).astype(BF16)
    pos, gate, idx = pl.pallas_call(
        functools.partial(_router_kernel, cap=cap), grid=(B,),
        in_specs=[pl.BlockSpec((S, D), lambda b: (b, 0)), pl.BlockSpec((E, D), lambda b: (0, 0)),
                  pl.BlockSpec((E, D), lambda b: (0, 0))],
        out_specs=[pl.BlockSpec((E, S), lambda b: (b, 0)), pl.BlockSpec((E, cap), lambda b: (b, 0)),
                   pl.BlockSpec((E, cap), lambda b: (b, 0))],
        out_shape=[jax.ShapeDtypeStruct((B * E, S), jnp.int32), jax.ShapeDtypeStruct((B * E, cap), F32),
                   jax.ShapeDtypeStruct((B * E, cap), jnp.int32)],
        compiler_params=_params("parallel"), name="moe_router")(x, wh, wl)
    xin = pl.pallas_call(
        functools.partial(_gather_kernel, n_rows=E * cap), grid=(B,),
        in_specs=[pl.BlockSpec((1, 1, E * cap), lambda b: (b, 0, 0), memory_space=pltpu.SMEM),
                  pl.BlockSpec((S, D // 2), lambda b: (b, 0))],
        out_specs=pl.BlockSpec((E * cap, D // 2), lambda b: (b, 0)),
        out_shape=jax.ShapeDtypeStruct((B * E * cap, D // 2), jnp.uint32),
        compiler_params=_params("parallel"), name="moe_gather")(idx.reshape(B, 1, E * cap), xb)
    ff = w_gate.shape[-1]
    G = math.gcd(B, FFN_GROUP)
    outs = pl.pallas_call(
        _ffn_kernel, grid=(E, B // G),
        in_specs=[pl.BlockSpec((G, 1, cap, D // 2), lambda e, b: (b, e, 0, 0)),
                  pl.BlockSpec((G, 1, 1, cap), lambda e, b: (b, e, 0, 0)),
                  pl.BlockSpec((1, 1, D, ff), lambda e, b: (layer, e, 0, 0)),
                  pl.BlockSpec((1, 1, D, ff), lambda e, b: (layer, e, 0, 0)),
                  pl.BlockSpec((1, 1, ff, D), lambda e, b: (layer, e, 0, 0))],
        out_specs=pl.BlockSpec((G, 1, cap, D), lambda e, b: (b, e, 0, 0)),
        out_shape=jax.ShapeDtypeStruct((B, E, cap, D), BF16),
        scratch_shapes=[pltpu.VMEM((D, ff), BF16), pltpu.VMEM((D, ff), BF16), pltpu.VMEM((ff, D), BF16)],
        compiler_params=_params("arbitrary", "arbitrary"), name="moe_ffn")(
            xin.reshape(B, E, cap, D // 2), gate.reshape(B, E, 1, cap), w_gate, w_up, w_down)
    ts = min(COMBINE_TILE, S)
    nb = S // ts
    return pl.pallas_call(
        functools.partial(_combine_kernel, cap=cap), grid=(B, nb),
        in_specs=[pl.BlockSpec((E * cap, D), lambda b, j: (b, 0)),
                  pl.BlockSpec((E, ts), lambda b, j: (b, j)),
                  pl.BlockSpec((ts, D), lambda b, j: (b * nb + j, 0)),
                  pl.BlockSpec((1, D), lambda b, j: (0, 0)), pl.BlockSpec((1, D), lambda b, j: (0, 0))],
        out_specs=pl.BlockSpec((ts, D), lambda b, j: (b * nb + j, 0)),
        out_shape=jax.ShapeDtypeStruct((T, D), F32),
        compiler_params=_params("parallel", "arbitrary"), name="moe_combine")(
            outs.reshape(B * E * cap, D), pos, x, ln_g.reshape(1, D), ln_b.reshape(1, D))


def kernel(x, positions, mlstm_w_in, mlstm_gate_b, mlstm_norm_g, mlstm_w_out, gla_w_in, gla_gate_w, gla_gate_b, gla_norm_g, gla_w_out, lru_w_in, lru_conv_w, lru_conv_b, lru_gate_a_w, lru_gate_a_b, lru_gate_x_w, lru_gate_x_b, lru_lambda, lru_w_out, mla_w_in, mla_q_norm_g, mla_kv_norm_g, mla_w_uq, mla_w_ukv, mla_w_out, moe_router, moe_w_gate, moe_w_up, moe_w_down, ln_g, ln_b):
    B, S, D = x.shape
    xf = x.reshape(B * S, D)
    xb = None
    for i in range(DEPTH):
        m = i % N_MIXERS
        j = i // N_MIXERS
        g0, b0 = ln_g[i, 0], ln_b[i, 0]
        if m == 0:
            xf, xb = _mlstm_layer(xf, xb, B, S, mlstm_w_in[j], mlstm_gate_b[j], mlstm_norm_g[j],
                                  mlstm_w_out[j], g0, b0)
        elif m == 1:
            xf, xb = _gla_layer(xf, xb, B, S, gla_w_in[j], gla_gate_w[j], gla_gate_b[j], gla_norm_g[j],
                                gla_w_out[j], g0, b0)
        elif m == 2:
            xf, xb = _lru_layer(xf, xb, B, S, lru_w_in[j], lru_conv_w[j], lru_conv_b[j], lru_gate_a_w[j],
                                lru_gate_a_b[j], lru_gate_x_w[j], lru_gate_x_b[j], lru_lambda[j],
                                lru_w_out[j], g0, b0)
        else:
            xf, xb = _mla_layer(xf, xb, B, S, positions, mla_w_in[j], mla_q_norm_g[j], mla_kv_norm_g[j],
                                mla_w_uq[j], mla_w_ukv[j], mla_w_out[j], g0, b0)
        xf = _moe_layer(xf, xb, B, S, i, moe_router[i], moe_w_gate, moe_w_up, moe_w_down,
                        ln_g[i, 1], ln_b[i, 1])
    return xf.reshape(B, S, D)
```

```python
import functools
import math

import jax
import jax.numpy as jnp
from jax import lax
from jax.experimental import pallas as pl
from jax.experimental.pallas import tpu as pltpu

F32 = jnp.float32
BF16 = jnp.bfloat16

D_MODEL = 1024
DEPTH = 4
N_MIXERS = 4
ALPHA = (2 * DEPTH) ** 0.25
LN_EPS = 1e-5

ML_HEADS = 4
ML_DV = D_MODEL // ML_HEADS
ML_DK = ML_DV // 2

GLA_HEADS = 4
GLA_DK = D_MODEL // 2 // GLA_HEADS
GLA_DV = D_MODEL // GLA_HEADS
GLA_RANK = 16
GLA_TAU = 16.0

LRU_WIDTH = D_MODEL
LRU_BLOCKS = 4
LRU_BW = LRU_WIDTH // LRU_BLOCKS
CONV_WIDTH = 4
LRU_C = 8.0

MLA_HEADS = 8
MLA_NOPE = 128
MLA_ROPE = 64
MLA_DV = 128
MLA_Q_RANK = 384
MLA_KV_RANK = 256
ROPE_THETA = 10000.0

N_EXPERTS = 16
CAPACITY_FACTOR = 2

V7X_VMEM_BYTES = 64 * 1024 * 1024
VMEM_LIMIT = V7X_VMEM_BYTES - 8 * 1024 * 1024
LANES = 128
SUBLANES = 8

ROW_TILE = 512
SEQ_TILE = 512
ML_CHUNK = 512
GLA_CHUNK = 128
Q_TILE = 2048
Q_SUB = 256
N_CHUNK = 512
FFN_GROUP = 4
COMBINE_TILE = 1024


def _params(*sem):
    return pltpu.CompilerParams(dimension_semantics=sem, vmem_limit_bytes=VMEM_LIMIT)


def _log_sigmoid(x):
    return jnp.minimum(x, 0.0) - jnp.log(1.0 + jnp.exp(-jnp.abs(x)))


def _sigmoid(x):
    return 0.5 * jnp.tanh(0.5 * x) + 0.5


def _layer_norm(v, g, b):
    mu = jnp.mean(v, axis=-1, keepdims=True)
    d = v - mu
    var = jnp.mean(d * d, axis=-1, keepdims=True)
    return d * lax.rsqrt(var + LN_EPS) * g + b


def _dot(a, b):
    return jnp.dot(a, b, preferred_element_type=F32)


def _dot_nt(a, b):
    return lax.dot_general(a, b, (((1,), (1,)), ((), ())), preferred_element_type=F32)


def _split3(a):
    a0 = a.astype(BF16)
    r1 = a - a0.astype(F32)
    a1 = r1.astype(BF16)
    a2 = (r1 - a1.astype(F32)).astype(BF16)
    return a0, a1, a2


def _dot_tn(a, b):
    return lax.dot_general(a, b, (((0,), (0,)), ((), ())), preferred_element_type=F32)


def _dense_kernel(*refs, n_w, n_t, has_bias):
    x_ref = refs[0]
    pos = 1
    w_refs = refs[pos:pos + n_w]
    pos += n_w
    b_refs = []
    for hb in has_bias:
        if hb:
            b_refs.append(refs[pos])
            pos += 1
        else:
            b_refs.append(None)
    t_refs = refs[pos:pos + 2 * n_t]
    pos += 2 * n_t
    o_refs = refs[pos:pos + n_w]
    pos += n_w
    ot_refs = refs[pos:pos + n_t]

    xb = x_ref[...].astype(BF16)
    for w_ref, b_ref, o_ref in zip(w_refs, b_refs, o_refs):
        n = w_ref.shape[1]
        for j0 in range(0, n, N_CHUNK):
            j1 = min(n, j0 + N_CHUNK)
            acc = _dot(xb, w_ref[:, j0:j1])
            if b_ref is not None:
                acc = acc + b_ref[:, j0:j1]
            o_ref[:, j0:j1] = acc.astype(o_ref.dtype)
    for i in range(n_t):
        wt_ref, bt_ref = t_refs[2 * i], t_refs[2 * i + 1]
        ot_refs[i][...] = (_dot_nt(wt_ref[...], xb) + bt_ref[...]).astype(ot_refs[i].dtype)


def _dense(x, ws, dtypes, biases=None, transposed=()):
    T, K = x.shape
    tm = min(ROW_TILE, T)
    if biases is None:
        biases = [None] * len(ws)
    has_bias = tuple(b is not None for b in biases)
    args = [x] + list(ws) + [b for b in biases if b is not None]
    in_specs = [pl.BlockSpec((tm, K), lambda i: (i, 0))]
    in_specs += [pl.BlockSpec(w.shape, lambda i: (0, 0)) for w in ws]
    in_specs += [pl.BlockSpec(b.shape, lambda i: (0, 0)) for b in biases if b is not None]
    for wt, bt, _ in transposed:
        args += [wt, bt]
        in_specs += [pl.BlockSpec(wt.shape, lambda i: (0, 0)), pl.BlockSpec(bt.shape, lambda i: (0, 0))]
    out_shape = [jax.ShapeDtypeStruct((T, w.shape[1]), dt) for w, dt in zip(ws, dtypes)]
    out_specs = [pl.BlockSpec((tm, w.shape[1]), lambda i: (i, 0)) for w in ws]
    for wt, _, dt in transposed:
        out_shape.append(jax.ShapeDtypeStruct((wt.shape[0], T), dt))
        out_specs.append(pl.BlockSpec((wt.shape[0], tm), lambda i: (0, i)))
    kern = functools.partial(_dense_kernel, n_w=len(ws), n_t=len(transposed), has_bias=has_bias)
    return pl.pallas_call(
        kern, grid=(T // tm,), in_specs=in_specs, out_specs=out_specs, out_shape=out_shape,
        compiler_params=_params("parallel"), name="dense")(*args)


def _outproj_ln_kernel(a_ref, w_ref, x_ref, g_ref, b_ref, wr_ref, o_ref, ob_ref, lg_ref):
    y = _dot(a_ref[...], w_ref[...])
    v = _layer_norm(ALPHA * x_ref[...] + y, g_ref[...], b_ref[...])
    o_ref[...] = v
    E = lg_ref.shape[0]
    vh = v.astype(BF16)
    vl = (v - vh.astype(F32)).astype(BF16)
    both = _dot_nt(wr_ref[...], vh)
    lg_ref[...] = both[:E] + (both[E:] + _dot_nt(wr_ref[:E, :], vl))
    half = v.shape[1] // 2
    bits = pltpu.bitcast(v.astype(BF16).astype(F32), jnp.uint32)
    ob_ref[...] = (bits[:, :half] >> 16) | bits[:, half:]


def _outproj_ln(a, w, x, g, b, w_router):
    T, K = a.shape
    E = w_router.shape[1]
    wr = w_router.astype(F32).T
    wh = wr.astype(BF16)
    wr2 = jnp.concatenate([wh, (wr - wh.astype(F32)).astype(BF16)], axis=0)
    D = w.shape[1]
    tm = min(ROW_TILE, T)
    row = lambda i: (i, 0)
    fix = lambda i: (0, 0)
    return pl.pallas_call(
        _outproj_ln_kernel, grid=(T // tm,),
        in_specs=[pl.BlockSpec((tm, K), row), pl.BlockSpec((K, D), fix), pl.BlockSpec((tm, D), row),
                  pl.BlockSpec((1, D), fix), pl.BlockSpec((1, D), fix), pl.BlockSpec((2 * E, D), fix)],
        out_specs=[pl.BlockSpec((tm, D), row), pl.BlockSpec((tm, D // 2), row),
                   pl.BlockSpec((E, tm), lambda i: (0, i))],
        out_shape=[jax.ShapeDtypeStruct((T, D), F32), jax.ShapeDtypeStruct((T, D // 2), jnp.uint32),
                   jax.ShapeDtypeStruct((E, T), F32)],
        compiler_params=_params("parallel"), name="outproj_ln")(a, w, x, g.reshape(1, D), b.reshape(1, D), wr2)


def _mlstm_kernel(*refs, reverse, ts, chunk):
    if reverse:
        qv_ref, kt_ref, gr_ref, hf_ref, o_ref, ng_ref, out_ref, c_ref, m_ref = refs
    else:
        qv_ref, kt_ref, gr_ref, out_ref, c_ref, m_ref = refs
    H, DK, DV, L = ML_HEADS, ML_DK, ML_DV, chunk
    scale = DK ** -0.5

    @pl.when(pl.program_id(1) == 0)
    def _():
        c_ref[...] = jnp.zeros_like(c_ref)
        m_ref[...] = jnp.zeros_like(m_ref)

    rows = lax.broadcasted_iota(jnp.int32, (L, L), 0)
    cols = lax.broadcasted_iota(jnp.int32, (L, L), 1)
    mask = (cols >= rows) if reverse else (cols <= rows)
    eye = rows == cols
    tri = jnp.where((rows >= cols) if reverse else (rows <= cols), 1.0, 0.0).astype(BF16)
    ones_col = jnp.where(lax.broadcasted_iota(jnp.int32, (L, LANES), 1) == 0, 1.0, 0.0).astype(BF16)
    d0 = 8 if reverse else 0
    last = 0 if reverse else L - 1
    n_chunks = ts // L
    order = range(n_chunks - 1, -1, -1) if reverse else range(n_chunks)
    neg_inf = -jnp.inf

    for c in order:
        r0 = c * L
        g8 = gr_ref[d0:d0 + 8, r0:r0 + L]
        lf8 = _log_sigmoid(g8)
        pieces = _dot(jnp.concatenate(_split3(lf8), axis=0), tri)
        b8 = pieces[0:8] + pieces[8:16] + pieces[16:24]
        u8 = g8[0:4, :] - b8[4:8, :]
        for h in range(H):
            qb = qv_ref[r0:r0 + L, h * DK:(h + 1) * DK]
            kt = kt_ref[h * DK:(h + 1) * DK, r0:r0 + L]
            vx = jnp.concatenate([qv_ref[r0:r0 + L, H * DK + h * DV:H * DK + (h + 1) * DV], ones_col], axis=1)
            u_r = u8[h:h + 1, :]
            b_r = b8[4 + h:5 + h, :]
            m_prev = m_ref[h:h + 1, 0:1]
            b_c = jnp.sum(jnp.where(eye, b_r, 0.0), axis=1, keepdims=True)
            um = jnp.where(mask, u_r, neg_inf)
            a_c = jnp.maximum(m_prev, jnp.max(um, axis=1, keepdims=True))
            s = _dot(qb, kt) * (scale * jnp.exp(um - a_c))
            w_int = jnp.exp(m_prev - a_c) * scale
            c_old = c_ref[h]
            acc = _dot(s.astype(BF16), vx) + w_int * _dot(qb, c_old.astype(BF16))
            den = acc[:, DV:DV + 1]
            hh = acc[:, :DV] / jnp.maximum(jnp.abs(den), jnp.exp(-(a_c + b_c)))
            a_last = jnp.maximum(m_prev, jnp.max(u_r, axis=1, keepdims=True))
            g_tot = b_r[:, last:last + 1]
            wc = jnp.exp(m_prev - a_last)
            kw = (kt.astype(F32) * jnp.exp(u_r - a_last)).astype(BF16)
            c_ref[h] = wc * c_old + _dot(kw, vx)
            m_ref[h:h + 1, :] = jnp.broadcast_to(g_tot + a_last, (1, LANES))
            if reverse:
                hs = hf_ref[r0:r0 + L, h * DV:(h + 1) * DV] + hh
                mu = jnp.mean(hs, axis=-1, keepdims=True)
                dd = hs - mu
                var = jnp.mean(dd * dd, axis=-1, keepdims=True)
                hn = dd * lax.rsqrt(var + LN_EPS) * ng_ref[:, h * DV:(h + 1) * DV]
                og = _sigmoid(o_ref[r0:r0 + L, h * DV:(h + 1) * DV])
                out_ref[r0:r0 + L, h * DV:(h + 1) * DV] = (og * hn).astype(out_ref.dtype)
            else:
                out_ref[r0:r0 + L, h * DV:(h + 1) * DV] = hh


def _mlstm_pass(qv, kt, gr, B, S, reverse, hf=None, o=None, ng=None):
    T = B * S
    ts = min(SEQ_TILE, S)
    chunk = min(ML_CHUNK, ts)
    nb = S // ts
    if reverse:
        blk = lambda b, j: (b * nb + nb - 1 - j, 0)
        blk_t = lambda b, j: (0, b * nb + nb - 1 - j)
    else:
        blk = lambda b, j: (b * nb + j, 0)
        blk_t = lambda b, j: (0, b * nb + j)
    in_specs = [pl.BlockSpec((ts, qv.shape[1]), blk), pl.BlockSpec((kt.shape[0], ts), blk_t),
                pl.BlockSpec((16, ts), blk_t)]
    args = [qv, kt, gr]
    if reverse:
        in_specs += [pl.BlockSpec((ts, D_MODEL), blk), pl.BlockSpec((ts, D_MODEL), blk),
                     pl.BlockSpec((1, D_MODEL), lambda b, j: (0, 0))]
        args += [hf, o, ng.reshape(1, D_MODEL)]
    out_dtype = BF16 if reverse else F32
    kern = functools.partial(_mlstm_kernel, reverse=reverse, ts=ts, chunk=chunk)
    return pl.pallas_call(
        kern, grid=(B, nb), in_specs=in_specs,
        out_specs=pl.BlockSpec((ts, D_MODEL), blk),
        out_shape=jax.ShapeDtypeStruct((T, D_MODEL), out_dtype),
        scratch_shapes=[pltpu.VMEM((ML_HEADS, ML_DK, ML_DV + LANES), F32),
                        pltpu.VMEM((SUBLANES, LANES), F32)],
        compiler_params=_params("parallel", "arbitrary"),
        name="mlstm_bwd" if reverse else "mlstm_fwd")(*args)


def _mlstm_layer(x, xb, B, S, w_in, gate_b, norm_g, w_out, ln_g, ln_b, w_router):
    del xb
    qk = ML_HEADS * ML_DK
    w = w_in.astype(BF16)
    w_qv = jnp.concatenate([w[:, :qk], w[:, 2 * qk:2 * qk + D_MODEL]], axis=1)
    w_k_t = w[:, qk:2 * qk].T
    w_o = w[:, 2 * qk + D_MODEL:2 * qk + 2 * D_MODEL]
    w_g_t = w[:, 2 * qk + 2 * D_MODEL:].T
    b_g = gate_b.astype(F32).reshape(16, 1)
    qv, o, kt, gr = _dense(x, [w_qv, w_o], [BF16, F32],
                           transposed=[(w_k_t, jnp.zeros((qk, 1), F32), BF16), (w_g_t, b_g, F32)])
    hf = _mlstm_pass(qv, kt, gr, B, S, False)
    a = _mlstm_pass(qv, kt, gr, B, S, True, hf, o, norm_g.astype(F32))
    return _outproj_ln(a, w_out.astype(BF16), x, ln_g, ln_b, w_router)


def _cumsum_rows(x, n, reverse):
    row = lax.broadcasted_iota(jnp.int32, x.shape, 0)
    sh = 1
    while sh < n:
        if reverse:
            x = x + jnp.where(row < n - sh, pltpu.roll(x, n - sh, axis=0), 0.0)
        else:
            x = x + jnp.where(row >= sh, pltpu.roll(x, sh, axis=0), 0.0)
        sh *= 2
    return x


def _gla_kernel(*refs, reverse, ts, chunk):
    if reverse:
        qkv_ref, glr_ref, gw_ref, gb_ref, of_ref, r_ref, ng_ref, out_ref, st_ref, la_ref = refs
    else:
        qkv_ref, glr_ref, gw_ref, gb_ref, out_ref, st_ref, la_ref = refs
    H, DK, DV, L = GLA_HEADS, GLA_DK, GLA_DV, chunk
    scale = DK ** -0.5
    d = 1 if reverse else 0

    @pl.when(pl.program_id(1) == 0)
    def _():
        st_ref[...] = jnp.zeros_like(st_ref)

    glr = glr_ref[:, d * GLA_RANK:(d + 1) * GLA_RANK]
    g0, g1, _ = _split3(glr)
    pre = _dot(jnp.concatenate([g0, g1, g0], axis=1), gw_ref[d]) + gb_ref[d:d + 1, :]
    la_ref[...] = _log_sigmoid(pre) * (1.0 / GLA_TAU)

    rows = lax.broadcasted_iota(jnp.int32, (L, L), 0)
    cols = lax.broadcasted_iota(jnp.int32, (L, L), 1)
    mask = (cols >= rows) if reverse else (cols <= rows)
    last = 0 if reverse else L - 1
    mid = L // 2
    n_chunks = ts // L

    def body(i, carry):
        c = (n_chunks - 1 - i) if reverse else i
        r0 = pl.multiple_of(c * L, L)
        bsum = _cumsum_rows(la_ref[pl.ds(r0, L), :], L, reverse)
        for h in range(H):
            b = bsum[:, h * DK:(h + 1) * DK]
            qf = qkv_ref[pl.ds(r0, L), h * DK:(h + 1) * DK].astype(F32)
            kf = qkv_ref[pl.ds(r0, L), H * DK + h * DK:H * DK + (h + 1) * DK].astype(F32)
            vb = qkv_ref[pl.ds(r0, L), 2 * H * DK + h * DV:2 * H * DK + (h + 1) * DV]
            beta = b[mid:mid + 1, :]
            g = b[last:last + 1, :]
            qt = (qf * jnp.exp(b - beta)).astype(BF16)
            kt = (kf * jnp.exp(beta - b)).astype(BF16)
            amat = jnp.where(mask, _dot_nt(qt, kt) * scale, 0.0)
            qh = (qf * (jnp.exp(b) * scale)).astype(BF16)
            st = st_ref[h]
            o = _dot(amat.astype(BF16), vb) + _dot_nt(qh, st.astype(BF16))
            kh = (kf * jnp.exp(g - b)).astype(BF16)
            st_ref[h] = st * jnp.exp(g) + _dot_tn(vb, kh)
            if reverse:
                hs = of_ref[pl.ds(r0, L), h * DV:(h + 1) * DV] + o
                mu = jnp.mean(hs, axis=-1, keepdims=True)
                dd = hs - mu
                var = jnp.mean(dd * dd, axis=-1, keepdims=True)
                hn = dd * lax.rsqrt(var + LN_EPS) * ng_ref[:, h * DV:(h + 1) * DV]
                rr = r_ref[pl.ds(r0, L), h * DV:(h + 1) * DV]
                out_ref[pl.ds(r0, L), h * DV:(h + 1) * DV] = (rr * _sigmoid(rr) * hn).astype(out_ref.dtype)
            else:
                out_ref[pl.ds(r0, L), h * DV:(h + 1) * DV] = o
        return carry

    lax.fori_loop(0, n_chunks, body, 0)


def _gla_pass(qkv, glr, gw, gb, B, S, reverse, of=None, r=None, ng=None):
    T = B * S
    ts = min(SEQ_TILE, S)
    chunk = min(GLA_CHUNK, ts)
    nb = S // ts
    if reverse:
        blk = lambda b, j: (b * nb + nb - 1 - j, 0)
    else:
        blk = lambda b, j: (b * nb + j, 0)
    fix2 = lambda b, j: (0, 0)
    fix3 = lambda b, j: (0, 0, 0)
    in_specs = [pl.BlockSpec((ts, qkv.shape[1]), blk), pl.BlockSpec((ts, 2 * GLA_RANK), blk),
                pl.BlockSpec(gw.shape, fix3), pl.BlockSpec(gb.shape, fix2)]
    args = [qkv, glr, gw, gb]
    if reverse:
        in_specs += [pl.BlockSpec((ts, D_MODEL), blk), pl.BlockSpec((ts, D_MODEL), blk),
                     pl.BlockSpec((1, D_MODEL), fix2)]
        args += [of, r, ng.reshape(1, D_MODEL)]
    kern = functools.partial(_gla_kernel, reverse=reverse, ts=ts, chunk=chunk)
    return pl.pallas_call(
        kern, grid=(B, nb), in_specs=in_specs,
        out_specs=pl.BlockSpec((ts, D_MODEL), blk),
        out_shape=jax.ShapeDtypeStruct((T, D_MODEL), BF16 if reverse else F32),
        scratch_shapes=[pltpu.VMEM((GLA_HEADS, GLA_DV, GLA_DK), F32),
                        pltpu.VMEM((ts, GLA_HEADS * GLA_DK), F32)],
        compiler_params=_params("parallel", "arbitrary"),
        name="gla_bwd" if reverse else "gla_fwd")(*args)


def _gla_layer(x, xb, B, S, w_in, gate_w, gate_b, norm_g, w_out, ln_g, ln_b, w_router):
    del xb
    qk = GLA_HEADS * GLA_DK
    w = w_in.astype(BF16)
    w_qkv = w[:, :2 * qk + D_MODEL]
    w_r = w[:, 2 * qk + D_MODEL:2 * qk + 2 * D_MODEL]
    w_glr = w[:, 2 * qk + 2 * D_MODEL:]
    qkv, r, glr = _dense(x, [w_qkv, w_r, w_glr], [BF16, F32, F32])
    gw0 = gate_w.astype(BF16)
    gw1 = (gate_w.astype(F32) - gw0.astype(F32)).astype(BF16)
    gw = jnp.concatenate([gw0, gw0, gw1], axis=1)
    gb = gate_b.astype(F32)
    of = _gla_pass(qkv, glr, gw, gb, B, S, False)
    a = _gla_pass(qkv, glr, gw, gb, B, S, True, of, r, norm_g.astype(F32))
    return _outproj_ln(a, w_out.astype(BF16), x, ln_g, ln_b, w_router)


def _gelu_tanh(x):
    return 0.5 * x * (1.0 + jnp.tanh(math.sqrt(2.0 / math.pi) * (x + 0.044715 * (x * x * x))))


def _lru_kernel(*refs, reverse, ts):
    if reverse:
        (u_ref, up_ref, un_ref, cw_ref, cb_ref, wg_ref, bg_ref, lam_ref, hf_ref, gate_ref,
         out_ref, a_ref, g_ref, h_ref, hs_ref) = refs
    else:
        (u_ref, up_ref, un_ref, cw_ref, cb_ref, wg_ref, bg_ref, lam_ref,
         out_ref, a_ref, g_ref, h_ref) = refs
        hs_ref = out_ref
    W = LRU_WIDTH
    j = pl.program_id(1)
    nb = pl.num_programs(1)
    jj = (nb - 1 - j) if reverse else j

    @pl.when(j == 0)
    def _():
        h_ref[...] = jnp.zeros_like(h_ref)

    z = u_ref[...]
    prev = jnp.where(jj > 0, up_ref[...], 0.0)
    nxt = jnp.where(jj < nb - 1, un_ref[...], 0.0)
    row = lax.broadcasted_iota(jnp.int32, (ts, W), 0)
    zm1 = jnp.where(row == 0, prev[7:8, :], pltpu.roll(z, 1, axis=0))
    zm2 = pltpu.roll(z, 2, axis=0)
    zm2 = jnp.where(row == 0, prev[6:7, :], jnp.where(row == 1, prev[7:8, :], zm2))
    zp1 = jnp.where(row == ts - 1, nxt[0:1, :], pltpu.roll(z, ts - 1, axis=0))
    u = cw_ref[0:1, :] * zm2 + cw_ref[1:2, :] * zm1 + cw_ref[2:3, :] * z + cw_ref[3:4, :] * zp1 + cb_ref[...]

    ls = LRU_C * _log_sigmoid(lam_ref[...])
    ub = u.astype(BF16)
    for n in range(LRU_BLOCKS):
        sl = slice(n * LRU_BW, (n + 1) * LRU_BW)
        pre = _dot(ub[:, sl], wg_ref[n]) + bg_ref[n]
        r = _sigmoid(pre[:, :LRU_BW])
        ig = _sigmoid(pre[:, LRU_BW:])
        log_a = r * ls[:, sl]
        a = jnp.exp(log_a)
        a_ref[:, sl] = a
        g_ref[:, sl] = jnp.sqrt(1.0 - a * a) * (ig * u[:, sl])

    n_tiles = ts // SUBLANES

    srow = lax.broadcasted_iota(jnp.int32, (SUBLANES, W), 0)
    carry_row = 0 if reverse else SUBLANES - 1

    def body(i, h):
        t = (n_tiles - 1 - i) if reverse else i
        r0 = pl.multiple_of(t * SUBLANES, SUBLANES)
        a8 = a_ref[pl.ds(r0, SUBLANES), :]
        g8 = g_ref[pl.ds(r0, SUBLANES), :]
        d = 1
        while d < SUBLANES:
            shift = SUBLANES - d if reverse else d
            keep = (srow < SUBLANES - d) if reverse else (srow >= d)
            g8 = g8 + a8 * jnp.where(keep, pltpu.roll(g8, shift, axis=0), 0.0)
            a8 = a8 * jnp.where(keep, pltpu.roll(a8, shift, axis=0), 1.0)
            d *= 2
        hs = g8 + a8 * h
        hs_ref[pl.ds(r0, SUBLANES), :] = hs
        return hs[carry_row:carry_row + 1, :]

    h_ref[...] = lax.fori_loop(0, n_tiles, body, h_ref[...], unroll=2)
    if reverse:
        out_ref[...] = (_gelu_tanh(gate_ref[...]) * (hf_ref[...] + hs_ref[...])).astype(out_ref.dtype)


def _lru_pass(u, cw, cb, wg, bg, lam, B, S, reverse, hf=None, gate=None):
    T = B * S
    W = LRU_WIDTH
    ts = min(SEQ_TILE, S)
    nb = S // ts
    tpb = ts // SUBLANES
    n8 = T // SUBLANES
    if reverse:
        seq = lambda b, j: b * nb + nb - 1 - j
    else:
        seq = lambda b, j: b * nb + j
    blk = lambda b, j: (seq(b, j), 0)
    blk_prev = lambda b, j: (jnp.maximum(seq(b, j) * tpb - 1, 0), 0)
    blk_next = lambda b, j: (jnp.minimum((seq(b, j) + 1) * tpb, n8 - 1), 0)
    fix2 = lambda b, j: (0, 0)
    fix3 = lambda b, j: (0, 0, 0)
    in_specs = [pl.BlockSpec((ts, W), blk), pl.BlockSpec((SUBLANES, W), blk_prev),
                pl.BlockSpec((SUBLANES, W), blk_next), pl.BlockSpec(cw.shape, fix2),
                pl.BlockSpec(cb.shape, fix2), pl.BlockSpec(wg.shape, fix3), pl.BlockSpec(bg.shape, fix3),
                pl.BlockSpec(lam.shape, fix2)]
    args = [u, u, u, cw, cb, wg, bg, lam]
    scratch = [pltpu.VMEM((ts, W), F32), pltpu.VMEM((ts, W), F32), pltpu.VMEM((1, W), F32)]
    if reverse:
        in_specs += [pl.BlockSpec((ts, W), blk), pl.BlockSpec((ts, W), blk)]
        args += [hf, gate]
        scratch.append(pltpu.VMEM((ts, W), F32))
    kern = functools.partial(_lru_kernel, reverse=reverse, ts=ts)
    return pl.pallas_call(
        kern, grid=(B, nb), in_specs=in_specs,
        out_specs=pl.BlockSpec((ts, W), blk),
        out_shape=jax.ShapeDtypeStruct((T, W), BF16 if reverse else F32),
        scratch_shapes=scratch,
        compiler_params=_params("parallel", "arbitrary"),
        name="lru_bwd" if reverse else "lru_fwd")(*args)


def _lru_layer(x, xb, B, S, w_in, conv_w, conv_b, gate_a_w, gate_a_b, gate_x_w, gate_x_b, lam, w_out,
               ln_g, ln_b, w_router):
    del xb
    W = LRU_WIDTH
    w = w_in.astype(BF16)
    gate, u = _dense(x, [w[:, :W], w[:, W:]], [F32, F32])
    cw = conv_w.astype(F32)
    cb = conv_b.astype(F32).reshape(1, W)
    passes = []
    for d in range(2):
        wg = jnp.concatenate([gate_a_w[d], gate_x_w[d]], axis=-1).astype(BF16)
        bg = jnp.concatenate([gate_a_b[d].reshape(LRU_BLOCKS, 1, LRU_BW),
                              gate_x_b[d].reshape(LRU_BLOCKS, 1, LRU_BW)], axis=-1).astype(F32)
        passes.append((wg, bg, lam[d].astype(F32).reshape(1, W)))
    hf = _lru_pass(u, cw, cb, *passes[0], B, S, False)
    a = _lru_pass(u, cw, cb, *passes[1], B, S, True, hf, gate)
    return _outproj_ln(a, w_out.astype(BF16), x, ln_g, ln_b, w_router)


MLA_HW = MLA_NOPE + LANES


def _mla_proj_kernel(x_ref, pos_ref, win_ref, qg_ref, kg_ref, wqn_ref, wqr_ref, wqs_ref, wkv_ref,
                     fr_ref, sg_ref, q_ref, k_ref, v_ref):
    H = MLA_HEADS
    scale = (MLA_NOPE + MLA_ROPE) ** -0.5 * math.log2(math.e)
    xb = x_ref[...].astype(BF16)
    z = _dot(xb, win_ref[...])
    cq = z[:, :MLA_Q_RANK]
    ckv = z[:, MLA_Q_RANK:MLA_Q_RANK + MLA_KV_RANK]
    kr = z[:, MLA_Q_RANK + MLA_KV_RANK:MLA_Q_RANK + MLA_KV_RANK + LANES]
    krs = z[:, MLA_Q_RANK + MLA_KV_RANK + LANES:]
    qn = (cq * lax.rsqrt(jnp.mean(cq * cq, axis=-1, keepdims=True) + LN_EPS) * qg_ref[...]).astype(BF16)
    kvn = (ckv * lax.rsqrt(jnp.mean(ckv * ckv, axis=-1, keepdims=True) + LN_EPS) * kg_ref[...]).astype(BF16)
    ang = pos_ref[...].astype(F32) * fr_ref[...]
    cosv = jnp.cos(ang)
    lane = lax.broadcasted_iota(jnp.int32, ang.shape, 1)
    cosv = jnp.where(lane < MLA_ROPE, cosv, 0.0)
    sinv = jnp.sin(ang) * sg_ref[...]
    k_rope = kr * cosv + krs * sinv
    kv = _dot(kvn, wkv_ref[...])
    q_nope = _dot(qn, wqn_ref[...])
    q_rope = _dot(qn, wqr_ref[...])
    q_swap = _dot(qn, wqs_ref[...])
    ones_col = jnp.where(lane == 0, 1.0, 0.0).astype(BF16)
    for h in range(H):
        a0 = h * MLA_HW
        q_ref[:, a0:a0 + MLA_NOPE] = (q_nope[:, h * MLA_NOPE:(h + 1) * MLA_NOPE] * scale).astype(BF16)
        qr = q_rope[:, h * LANES:(h + 1) * LANES] * cosv + q_swap[:, h * LANES:(h + 1) * LANES] * sinv
        q_ref[:, a0 + MLA_NOPE:a0 + MLA_HW] = (qr * scale).astype(BF16)
        k_ref[:, a0:a0 + MLA_NOPE] = kv[:, h * 2 * MLA_NOPE:h * 2 * MLA_NOPE + MLA_NOPE].astype(BF16)
        k_ref[:, a0 + MLA_NOPE:a0 + MLA_HW] = k_rope.astype(BF16)
        v_ref[:, 2 * h * MLA_DV:(2 * h + 1) * MLA_DV] = kv[:, h * 2 * MLA_NOPE + MLA_NOPE:(h + 1) * 2 * MLA_NOPE].astype(BF16)
        v_ref[:, (2 * h + 1) * MLA_DV:(2 * h + 2) * MLA_DV] = ones_col


def _attn_kernel(q_ref, k_ref, v_ref, o_ref, *, n_sub):
    k = k_ref[...]
    v = v_ref[...]
    rows = q_ref.shape[0] // n_sub
    for i in range(n_sub):
        s = _dot_nt(q_ref[i * rows:(i + 1) * rows, :], k)
        p = jnp.exp2(s - jnp.max(s, axis=-1, keepdims=True))
        acc = _dot(p.astype(BF16), v)
        o_ref[i * rows:(i + 1) * rows, :] = (acc[:, :MLA_DV] / acc[:, MLA_DV:MLA_DV + 1]).astype(o_ref.dtype)


def _pad_rope_cols(w, swap):
    half = MLA_ROPE // 2
    if swap:
        w = jnp.concatenate([w[..., half:], w[..., :half]], axis=-1)
    w = jnp.concatenate([w, jnp.zeros_like(w)], axis=-1)
    return w.reshape(w.shape[0], -1)


def _mla_layer(x, xb, B, S, positions, w_in, q_norm_g, kv_norm_g, w_uq, w_ukv, w_out, ln_g, ln_b, w_router):
    del xb
    T = B * S
    H = MLA_HEADS
    half = MLA_ROPE // 2
    w_kr = w_in[:, MLA_Q_RANK + MLA_KV_RANK:].reshape(D_MODEL, 1, MLA_ROPE)
    win = jnp.concatenate([w_in[:, :MLA_Q_RANK + MLA_KV_RANK], _pad_rope_cols(w_kr, False),
                           _pad_rope_cols(w_kr, True)], axis=1).astype(BF16)
    wq = w_uq.reshape(MLA_Q_RANK, H, MLA_NOPE + MLA_ROPE)
    wqn = wq[:, :, :MLA_NOPE].reshape(MLA_Q_RANK, H * MLA_NOPE).astype(BF16)
    wqr = _pad_rope_cols(wq[:, :, MLA_NOPE:], False).astype(BF16)
    wqs = _pad_rope_cols(wq[:, :, MLA_NOPE:], True).astype(BF16)
    freq = ROPE_THETA ** (-jnp.arange(half, dtype=F32) / half)
    zeros = jnp.zeros((LANES - MLA_ROPE,), F32)
    fr = jnp.concatenate([freq, freq, zeros]).reshape(1, LANES)
    sg = jnp.concatenate([-jnp.ones((half,), F32), jnp.ones((half,), F32), zeros]).reshape(1, LANES)
    tm = min(ROW_TILE, T)
    row = lambda i: (i, 0)
    fix = lambda i: (0, 0)
    ins = [x, positions.reshape(T, 1), win, q_norm_g.astype(F32).reshape(1, -1),
           kv_norm_g.astype(F32).reshape(1, -1), wqn, wqr, wqs, w_ukv.astype(BF16), fr, sg]
    in_specs = [pl.BlockSpec((tm, D_MODEL), row), pl.BlockSpec((tm, 1), row)]
    in_specs += [pl.BlockSpec(a.shape, fix) for a in ins[2:]]
    q, k, v = pl.pallas_call(
        _mla_proj_kernel, grid=(T // tm,), in_specs=in_specs,
        out_specs=[pl.BlockSpec((tm, H * MLA_HW), row), pl.BlockSpec((tm, H * MLA_HW), row),
                   pl.BlockSpec((tm, 2 * H * MLA_DV), row)],
        out_shape=[jax.ShapeDtypeStruct((T, H * MLA_HW), BF16), jax.ShapeDtypeStruct((T, H * MLA_HW), BF16),
                   jax.ShapeDtypeStruct((T, 2 * H * MLA_DV), BF16)],
        compiler_params=_params("parallel"), name="mla_proj")(*ins)
    tq = min(Q_TILE, S)
    nq = S // tq
    att = pl.pallas_call(
        functools.partial(_attn_kernel, n_sub=tq // min(Q_SUB, tq)), grid=(B, H, nq),
        in_specs=[pl.BlockSpec((tq, MLA_HW), lambda b, h, i: (b * nq + i, h)),
                  pl.BlockSpec((S, MLA_HW), lambda b, h, i: (b, h)),
                  pl.BlockSpec((S, 2 * MLA_DV), lambda b, h, i: (b, h))],
        out_specs=pl.BlockSpec((tq, MLA_DV), lambda b, h, i: (b * nq + i, h)),
        out_shape=jax.ShapeDtypeStruct((T, H * MLA_DV), BF16),
        compiler_params=_params("parallel", "parallel", "arbitrary"), name="mla_attn")(q, k, v)
    return _outproj_ln(att, w_out.astype(BF16), x, ln_g, ln_b, w_router)


def _router_kernel(lg_ref, pos_ref, gate_ref, idx_ref, *, cap):
    logits = lg_ref[...]
    E, S = logits.shape
    mx = jnp.max(logits, axis=0, keepdims=True)
    ex = jnp.exp(logits - mx)
    aff = ex / jnp.sum(ex, axis=0, keepdims=True)
    bits = pltpu.bitcast(aff, jnp.int32)

    def bit_step(i, thr):
        cand = thr | jnp.left_shift(jnp.int32(1), 30 - i)
        cnt = jnp.sum(jnp.where(bits >= cand, 1.0, 0.0), axis=1, keepdims=True)
        return jnp.where(cnt >= cap, cand, thr)

    thr = lax.fori_loop(0, 31, bit_step, jnp.zeros((E, 1), jnp.int32))
    gt = bits > thr
    eq = bits == thr
    need = cap - jnp.sum(jnp.where(gt, 1.0, 0.0), axis=1, keepdims=True)
    r = lax.broadcasted_iota(jnp.int32, (LANES, LANES), 0)
    c = lax.broadcasted_iota(jnp.int32, (LANES, LANES), 1)
    upper = jnp.where(r < c, 1.0, 0.0).astype(BF16)
    off = jnp.zeros((2 * E, 1), F32)
    for blk in range(S // LANES):
        sl = slice(blk * LANES, (blk + 1) * LANES)
        ind = jnp.concatenate([jnp.where(gt[:, sl], 1.0, 0.0), jnp.where(eq[:, sl], 1.0, 0.0)], axis=0)
        pre = _dot(ind.astype(BF16), upper) + off
        off = off + jnp.sum(ind, axis=1, keepdims=True)
        pg, pe = pre[:E], pre[E:]
        sel = gt[:, sl] | (eq[:, sl] & (pe < need))
        slot = pg + jnp.minimum(pe, need)
        pos_ref[:, sl] = jnp.where(sel, slot, -1.0).astype(jnp.int32)

    slots = lax.broadcasted_iota(jnp.int32, (cap, S), 0)
    token = lax.broadcasted_iota(jnp.int32, (1, S), 1)
    tok_hi = (token >> 6).astype(F32)
    tok_lo = (token & 63).astype(F32)
    pad = jnp.zeros((3, S), F32)
    for e in range(E):
        onehot = jnp.where(slots == pos_ref[e:e + 1, :], 1.0, 0.0).astype(BF16)
        a = aff[e:e + 1, :]
        a0 = a.astype(BF16).astype(F32)
        a1 = (a - a0).astype(BF16).astype(F32)
        a2 = a - a0 - a1
        vals = jnp.concatenate([tok_hi, tok_lo, a0, a1, a2, pad], axis=0).astype(BF16)
        res = _dot_nt(vals, onehot)
        idx_ref[e:e + 1, :] = (res[0:1, :] * 64.0 + res[1:2, :]).astype(jnp.int32)
        gate_ref[e:e + 1, :] = res[2:3, :] + res[3:4, :] + res[4:5, :]


def _gather_kernel(idx_ref, xp_ref, xin_ref, *, n_rows):
    def gather(t, carry):
        j0 = pl.multiple_of(t * SUBLANES, SUBLANES)
        rows = [xp_ref[pl.ds(idx_ref[0, 0, j0 + k], 1), :] for k in range(SUBLANES)]
        xin_ref[pl.ds(j0, SUBLANES), :] = jnp.concatenate(rows, axis=0)
        return carry

    lax.fori_loop(0, n_rows // SUBLANES, gather, 0, unroll=2)


def _ffn_kernel(xin_ref, gate_ref, wg_ref, wu_ref, wd_ref, out_ref, wgb_ref, wub_ref, wdb_ref):
    G, _, cap, half = xin_ref.shape

    @pl.when(pl.program_id(1) == 0)
    def _():
        wgb_ref[...] = wg_ref[0, 0].astype(BF16)
        wub_ref[...] = wu_ref[0, 0].astype(BF16)
        wdb_ref[...] = wd_ref[0, 0].astype(BF16)

    words = xin_ref[...].reshape(G * cap, half)
    x_lo = pltpu.bitcast(words << 16, F32).astype(BF16)
    x_hi = pltpu.bitcast(words & jnp.uint32(0xFFFF0000), F32).astype(BF16)
    hg = _dot(x_lo, wgb_ref[:half, :]) + _dot(x_hi, wgb_ref[half:, :])
    hu = _dot(x_lo, wub_ref[:half, :]) + _dot(x_hi, wub_ref[half:, :])
    hmid = (hg * _sigmoid(hg) * hu).astype(BF16)
    out = _dot(hmid, wdb_ref[...])
    eye = (lax.broadcasted_iota(jnp.int32, (cap, cap), 0) == lax.broadcasted_iota(jnp.int32, (cap, cap), 1))
    for g in range(G):
        gate = jnp.sum(jnp.where(eye, gate_ref[g, 0], 0.0), axis=1, keepdims=True)
        out_ref[g, 0] = (out[g * cap:(g + 1) * cap, :] * gate).astype(out_ref.dtype)


def _combine_kernel(outs_ref, pos_ref, x_ref, g_ref, b_ref, o_ref, *, cap):
    E, ts = pos_ref.shape
    posf = pos_ref[...].astype(F32)
    pos_t = jnp.concatenate([posf, jnp.full((LANES - E, ts), -1.0, F32)], axis=0).T
    lane = lax.broadcasted_iota(jnp.int32, (ts, cap), 1).astype(F32)
    onehot = jnp.concatenate(
        [jnp.where(pos_t[:, e:e + 1] == lane, 1.0, 0.0).astype(BF16) for e in range(E)], axis=1)
    y = _dot(onehot, outs_ref[...])
    o_ref[...] = _layer_norm(ALPHA * x_ref[...] + y, g_ref[...], b_ref[...])


def _moe_layer(x, xb, logits, B, S, layer, w_gate, w_up, w_down, ln_g, ln_b):
    T = B * S
    E = N_EXPERTS
    D = D_MODEL
    cap = CAPACITY_FACTOR * S // E
    pos, gate, idx = pl.pallas_call(
        functools.partial(_router_kernel, cap=cap), grid=(B,),
        in_specs=[pl.BlockSpec((E, S), lambda b: (0, b))],
        out_specs=[pl.BlockSpec((E, S), lambda b: (b, 0)), pl.BlockSpec((E, cap), lambda b: (b, 0)),
                   pl.BlockSpec((E, cap), lambda b: (b, 0))],
        out_shape=[jax.ShapeDtypeStruct((B * E, S), jnp.int32), jax.ShapeDtypeStruct((B * E, cap), F32),
                   jax.ShapeDtypeStruct((B * E, cap), jnp.int32)],
        compiler_params=_params("parallel"), name="moe_router")(logits)
    xin = pl.pallas_call(
        functools.partial(_gather_kernel, n_rows=E * cap), grid=(B,),
        in_specs=[pl.BlockSpec((1, 1, E * cap), lambda b: (b, 0, 0), memory_space=pltpu.SMEM),
                  pl.BlockSpec((S, D // 2), lambda b: (b, 0))],
        out_specs=pl.BlockSpec((E * cap, D // 2), lambda b: (b, 0)),
        out_shape=jax.ShapeDtypeStruct((B * E * cap, D // 2), jnp.uint32),
        compiler_params=_params("parallel"), name="moe_gather")(idx.reshape(B, 1, E * cap), xb)
    ff = w_gate.shape[-1]
    G = math.gcd(B, FFN_GROUP)
    outs = pl.pallas_call(
        _ffn_kernel, grid=(E, B // G),
        in_specs=[pl.BlockSpec((G, 1, cap, D // 2), lambda e, b: (b, e, 0, 0)),
                  pl.BlockSpec((G, 1, 1, cap), lambda e, b: (b, e, 0, 0)),
                  pl.BlockSpec((1, 1, D, ff), lambda e, b: (layer, e, 0, 0)),
                  pl.BlockSpec((1, 1, D, ff), lambda e, b: (layer, e, 0, 0)),
                  pl.BlockSpec((1, 1, ff, D), lambda e, b: (layer, e, 0, 0))],
        out_specs=pl.BlockSpec((G, 1, cap, D), lambda e, b: (b, e, 0, 0)),
        out_shape=jax.ShapeDtypeStruct((B, E, cap, D), BF16),
        scratch_shapes=[pltpu.VMEM((D, ff), BF16), pltpu.VMEM((D, ff), BF16), pltpu.VMEM((ff, D), BF16)],
        compiler_params=_params("arbitrary", "arbitrary"), name="moe_ffn")(
            xin.reshape(B, E, cap, D // 2), gate.reshape(B, E, 1, cap), w_gate, w_up, w_down)
    ts = min(COMBINE_TILE, S)
    nb = S // ts
    return pl.pallas_call(
        functools.partial(_combine_kernel, cap=cap), grid=(B, nb),
        in_specs=[pl.BlockSpec((E * cap, D), lambda b, j: (b, 0)),
                  pl.BlockSpec((E, ts), lambda b, j: (b, j)),
                  pl.BlockSpec((ts, D), lambda b, j: (b * nb + j, 0)),
                  pl.BlockSpec((1, D), lambda b, j: (0, 0)), pl.BlockSpec((1, D), lambda b, j: (0, 0))],
        out_specs=pl.BlockSpec((ts, D), lambda b, j: (b * nb + j, 0)),
        out_shape=jax.ShapeDtypeStruct((T, D), F32),
        compiler_params=_params("parallel", "arbitrary"), name="moe_combine")(
            outs.reshape(B * E * cap, D), pos, x, ln_g.reshape(1, D), ln_b.reshape(1, D))


def kernel(x, positions, mlstm_w_in, mlstm_gate_b, mlstm_norm_g, mlstm_w_out, gla_w_in, gla_gate_w, gla_gate_b, gla_norm_g, gla_w_out, lru_w_in, lru_conv_w, lru_conv_b, lru_gate_a_w, lru_gate_a_b, lru_gate_x_w, lru_gate_x_b, lru_lambda, lru_w_out, mla_w_in, mla_q_norm_g, mla_kv_norm_g, mla_w_uq, mla_w_ukv, mla_w_out, moe_router, moe_w_gate, moe_w_up, moe_w_down, ln_g, ln_b):
    B, S, D = x.shape
    xf = x.reshape(B * S, D)
    xb = None
    for i in range(DEPTH):
        m = i % N_MIXERS
        j = i // N_MIXERS
        g0, b0, wr = ln_g[i, 0], ln_b[i, 0], moe_router[i]
        if m == 0:
            xf, xb, lg = _mlstm_layer(xf, xb, B, S, mlstm_w_in[j], mlstm_gate_b[j], mlstm_norm_g[j],
                                      mlstm_w_out[j], g0, b0, wr)
        elif m == 1:
            xf, xb, lg = _gla_layer(xf, xb, B, S, gla_w_in[j], gla_gate_w[j], gla_gate_b[j], gla_norm_g[j],
                                    gla_w_out[j], g0, b0, wr)
        elif m == 2:
            xf, xb, lg = _lru_layer(xf, xb, B, S, lru_w_in[j], lru_conv_w[j], lru_conv_b[j], lru_gate_a_w[j],
                                    lru_gate_a_b[j], lru_gate_x_w[j], lru_gate_x_b[j], lru_lambda[j],
                                    lru_w_out[j], g0, b0, wr)
        else:
            xf, xb, lg = _mla_layer(xf, xb, B, S, positions, mla_w_in[j], mla_q_norm_g[j], mla_kv_norm_g[j],
                                    mla_w_uq[j], mla_w_ukv[j], mla_w_out[j], g0, b0, wr)
        xf = _moe_layer(xf, xb, lg, B, S, i, moe_w_gate, moe_w_up, moe_w_down, ln_g[i, 1], ln_b[i, 1])
    return xf.reshape(B, S, D)
```

```python
import functools
import math

import jax
import jax.numpy as jnp
from jax import lax
from jax.experimental import pallas as pl
from jax.experimental.pallas import tpu as pltpu

F32 = jnp.float32
BF16 = jnp.bfloat16

D_MODEL = 1024
DEPTH = 4
N_MIXERS = 4
ALPHA = (2 * DEPTH) ** 0.25
LN_EPS = 1e-5

ML_HEADS = 4
ML_DV = D_MODEL // ML_HEADS
ML_DK = ML_DV // 2

GLA_HEADS = 4
GLA_DK = D_MODEL // 2 // GLA_HEADS
GLA_DV = D_MODEL // GLA_HEADS
GLA_RANK = 16
GLA_TAU = 16.0

LRU_WIDTH = D_MODEL
LRU_BLOCKS = 4
LRU_BW = LRU_WIDTH // LRU_BLOCKS
CONV_WIDTH = 4
LRU_C = 8.0

MLA_HEADS = 8
MLA_NOPE = 128
MLA_ROPE = 64
MLA_DV = 128
MLA_Q_RANK = 384
MLA_KV_RANK = 256
ROPE_THETA = 10000.0

N_EXPERTS = 16
CAPACITY_FACTOR = 2

V7X_VMEM_BYTES = 64 * 1024 * 1024
VMEM_LIMIT = V7X_VMEM_BYTES - 8 * 1024 * 1024
LANES = 128
SUBLANES = 8

ROW_TILE = 1024
SEQ_TILE = 1024
ML_CHUNK = 512
GLA_CHUNK = 128
Q_TILE = 2048
Q_SUB = 256
N_CHUNK = 512
FFN_GROUP = 4
COMBINE_TILE = 1024


def _params(*sem):
    return pltpu.CompilerParams(dimension_semantics=sem, vmem_limit_bytes=VMEM_LIMIT)


def _log_sigmoid(x):
    return jnp.minimum(x, 0.0) - jnp.log(1.0 + jnp.exp(-jnp.abs(x)))


def _sigmoid(x):
    return 0.5 * jnp.tanh(0.5 * x) + 0.5


def _layer_norm(v, g, b):
    mu = jnp.mean(v, axis=-1, keepdims=True)
    d = v - mu
    var = jnp.mean(d * d, axis=-1, keepdims=True)
    return d * lax.rsqrt(var + LN_EPS) * g + b


def _dot(a, b):
    return jnp.dot(a, b, preferred_element_type=F32)


def _dot_nt(a, b):
    return lax.dot_general(a, b, (((1,), (1,)), ((), ())), preferred_element_type=F32)


def _split3(a):
    a0 = a.astype(BF16)
    r1 = a - a0.astype(F32)
    a1 = r1.astype(BF16)
    a2 = (r1 - a1.astype(F32)).astype(BF16)
    return a0, a1, a2


def _dot_tn(a, b):
    return lax.dot_general(a, b, (((0,), (0,)), ((), ())), preferred_element_type=F32)


def _dense_kernel(*refs, n_w, n_t, has_bias):
    x_ref = refs[0]
    pos = 1
    w_refs = refs[pos:pos + n_w]
    pos += n_w
    b_refs = []
    for hb in has_bias:
        if hb:
            b_refs.append(refs[pos])
            pos += 1
        else:
            b_refs.append(None)
    t_refs = refs[pos:pos + 2 * n_t]
    pos += 2 * n_t
    o_refs = refs[pos:pos + n_w]
    pos += n_w
    ot_refs = refs[pos:pos + n_t]

    xb = x_ref[...].astype(BF16)
    for w_ref, b_ref, o_ref in zip(w_refs, b_refs, o_refs):
        n = w_ref.shape[1]
        for j0 in range(0, n, N_CHUNK):
            j1 = min(n, j0 + N_CHUNK)
            acc = _dot(xb, w_ref[:, j0:j1])
            if b_ref is not None:
                acc = acc + b_ref[:, j0:j1]
            o_ref[:, j0:j1] = acc.astype(o_ref.dtype)
    for i in range(n_t):
        wt_ref, bt_ref = t_refs[2 * i], t_refs[2 * i + 1]
        ot_refs[i][...] = (_dot_nt(wt_ref[...], xb) + bt_ref[...]).astype(ot_refs[i].dtype)


def _dense(x, ws, dtypes, biases=None, transposed=()):
    T, K = x.shape
    tm = min(ROW_TILE, T)
    if biases is None:
        biases = [None] * len(ws)
    has_bias = tuple(b is not None for b in biases)
    args = [x] + list(ws) + [b for b in biases if b is not None]
    in_specs = [pl.BlockSpec((tm, K), lambda i: (i, 0))]
    in_specs += [pl.BlockSpec(w.shape, lambda i: (0, 0)) for w in ws]
    in_specs += [pl.BlockSpec(b.shape, lambda i: (0, 0)) for b in biases if b is not None]
    for wt, bt, _ in transposed:
        args += [wt, bt]
        in_specs += [pl.BlockSpec(wt.shape, lambda i: (0, 0)), pl.BlockSpec(bt.shape, lambda i: (0, 0))]
    out_shape = [jax.ShapeDtypeStruct((T, w.shape[1]), dt) for w, dt in zip(ws, dtypes)]
    out_specs = [pl.BlockSpec((tm, w.shape[1]), lambda i: (i, 0)) for w in ws]
    for wt, _, dt in transposed:
        out_shape.append(jax.ShapeDtypeStruct((wt.shape[0], T), dt))
        out_specs.append(pl.BlockSpec((wt.shape[0], tm), lambda i: (0, i)))
    kern = functools.partial(_dense_kernel, n_w=len(ws), n_t=len(transposed), has_bias=has_bias)
    return pl.pallas_call(
        kern, grid=(T // tm,), in_specs=in_specs, out_specs=out_specs, out_shape=out_shape,
        compiler_params=_params("parallel"), name="dense")(*args)


def _outproj_ln_kernel(a_ref, w_ref, x_ref, g_ref, b_ref, wr_ref, o_ref, ob_ref, lg_ref):
    y = _dot(a_ref[...], w_ref[...])
    v = _layer_norm(ALPHA * x_ref[...] + y, g_ref[...], b_ref[...])
    o_ref[...] = v
    E = lg_ref.shape[0]
    vh = v.astype(BF16)
    vl = (v - vh.astype(F32)).astype(BF16)
    both = _dot_nt(wr_ref[...], vh)
    lg_ref[...] = both[:E] + (both[E:] + _dot_nt(wr_ref[:E, :], vl))
    half = v.shape[1] // 2
    bits = pltpu.bitcast(v.astype(BF16).astype(F32), jnp.uint32)
    ob_ref[...] = (bits[:, :half] >> 16) | bits[:, half:]


def _outproj_ln(a, w, x, g, b, w_router):
    T, K = a.shape
    E = w_router.shape[1]
    wr = w_router.astype(F32).T
    wh = wr.astype(BF16)
    wr2 = jnp.concatenate([wh, (wr - wh.astype(F32)).astype(BF16)], axis=0)
    D = w.shape[1]
    tm = min(ROW_TILE, T)
    row = lambda i: (i, 0)
    fix = lambda i: (0, 0)
    return pl.pallas_call(
        _outproj_ln_kernel, grid=(T // tm,),
        in_specs=[pl.BlockSpec((tm, K), row), pl.BlockSpec((K, D), fix), pl.BlockSpec((tm, D), row),
                  pl.BlockSpec((1, D), fix), pl.BlockSpec((1, D), fix), pl.BlockSpec((2 * E, D), fix)],
        out_specs=[pl.BlockSpec((tm, D), row), pl.BlockSpec((tm, D // 2), row),
                   pl.BlockSpec((E, tm), lambda i: (0, i))],
        out_shape=[jax.ShapeDtypeStruct((T, D), F32), jax.ShapeDtypeStruct((T, D // 2), jnp.uint32),
                   jax.ShapeDtypeStruct((E, T), F32)],
        compiler_params=_params("parallel"), name="outproj_ln")(a, w, x, g.reshape(1, D), b.reshape(1, D), wr2)


def _mlstm_kernel(*refs, reverse, ts, chunk):
    if reverse:
        qv_ref, kt_ref, gr_ref, hf_ref, o_ref, ng_ref, out_ref, c_ref, m_ref = refs
    else:
        qv_ref, kt_ref, gr_ref, out_ref, c_ref, m_ref = refs
    H, DK, DV, L = ML_HEADS, ML_DK, ML_DV, chunk
    scale = DK ** -0.5

    @pl.when(pl.program_id(1) == 0)
    def _():
        c_ref[...] = jnp.zeros_like(c_ref)
        m_ref[...] = jnp.zeros_like(m_ref)

    rows = lax.broadcasted_iota(jnp.int32, (L, L), 0)
    cols = lax.broadcasted_iota(jnp.int32, (L, L), 1)
    mask = (cols >= rows) if reverse else (cols <= rows)
    eye = rows == cols
    tri = jnp.where((rows >= cols) if reverse else (rows <= cols), 1.0, 0.0).astype(BF16)
    ones_col = jnp.where(lax.broadcasted_iota(jnp.int32, (L, LANES), 1) == 0, 1.0, 0.0).astype(BF16)
    d0 = 8 if reverse else 0
    last = 0 if reverse else L - 1
    n_chunks = ts // L
    order = range(n_chunks - 1, -1, -1) if reverse else range(n_chunks)
    neg_inf = -jnp.inf

    for c in order:
        r0 = c * L
        g8 = gr_ref[d0:d0 + 8, r0:r0 + L]
        lf8 = _log_sigmoid(g8)
        pieces = _dot(jnp.concatenate(_split3(lf8), axis=0), tri)
        b8 = pieces[0:8] + pieces[8:16] + pieces[16:24]
        u8 = g8[0:4, :] - b8[4:8, :]
        for h in range(H):
            qb = qv_ref[r0:r0 + L, h * DK:(h + 1) * DK]
            kt = kt_ref[h * DK:(h + 1) * DK, r0:r0 + L]
            vx = jnp.concatenate([qv_ref[r0:r0 + L, H * DK + h * DV:H * DK + (h + 1) * DV], ones_col], axis=1)
            u_r = u8[h:h + 1, :]
            b_r = b8[4 + h:5 + h, :]
            m_prev = m_ref[h:h + 1, 0:1]
            b_c = jnp.sum(jnp.where(eye, b_r, 0.0), axis=1, keepdims=True)
            um = jnp.where(mask, u_r, neg_inf)
            a_c = jnp.maximum(m_prev, jnp.max(um, axis=1, keepdims=True))
            s = _dot(qb, kt) * (scale * jnp.exp(um - a_c))
            w_int = jnp.exp(m_prev - a_c) * scale
            c_old = c_ref[h]
            acc = _dot(s.astype(BF16), vx) + w_int * _dot(qb, c_old.astype(BF16))
            den = acc[:, DV:DV + 1]
            hh = acc[:, :DV] / jnp.maximum(jnp.abs(den), jnp.exp(-(a_c + b_c)))
            a_last = jnp.maximum(m_prev, jnp.max(u_r, axis=1, keepdims=True))
            g_tot = b_r[:, last:last + 1]
            wc = jnp.exp(m_prev - a_last)
            kw = (kt.astype(F32) * jnp.exp(u_r - a_last)).astype(BF16)
            c_ref[h] = wc * c_old + _dot(kw, vx)
            m_ref[h:h + 1, :] = jnp.broadcast_to(g_tot + a_last, (1, LANES))
            if reverse:
                hs = hf_ref[r0:r0 + L, h * DV:(h + 1) * DV] + hh
                mu = jnp.mean(hs, axis=-1, keepdims=True)
                dd = hs - mu
                var = jnp.mean(dd * dd, axis=-1, keepdims=True)
                hn = dd * lax.rsqrt(var + LN_EPS) * ng_ref[:, h * DV:(h + 1) * DV]
                og = _sigmoid(o_ref[r0:r0 + L, h * DV:(h + 1) * DV])
                out_ref[r0:r0 + L, h * DV:(h + 1) * DV] = (og * hn).astype(out_ref.dtype)
            else:
                out_ref[r0:r0 + L, h * DV:(h + 1) * DV] = hh


def _mlstm_pass(qv, kt, gr, B, S, reverse, hf=None, o=None, ng=None):
    T = B * S
    ts = min(SEQ_TILE, S)
    chunk = min(ML_CHUNK, ts)
    nb = S // ts
    if reverse:
        blk = lambda b, j: (b * nb + nb - 1 - j, 0)
        blk_t = lambda b, j: (0, b * nb + nb - 1 - j)
    else:
        blk = lambda b, j: (b * nb + j, 0)
        blk_t = lambda b, j: (0, b * nb + j)
    in_specs = [pl.BlockSpec((ts, qv.shape[1]), blk), pl.BlockSpec((kt.shape[0], ts), blk_t),
                pl.BlockSpec((16, ts), blk_t)]
    args = [qv, kt, gr]
    if reverse:
        in_specs += [pl.BlockSpec((ts, D_MODEL), blk), pl.BlockSpec((ts, D_MODEL), blk),
                     pl.BlockSpec((1, D_MODEL), lambda b, j: (0, 0))]
        args += [hf, o, ng.reshape(1, D_MODEL)]
    out_dtype = BF16 if reverse else F32
    kern = functools.partial(_mlstm_kernel, reverse=reverse, ts=ts, chunk=chunk)
    return pl.pallas_call(
        kern, grid=(B, nb), in_specs=in_specs,
        out_specs=pl.BlockSpec((ts, D_MODEL), blk),
        out_shape=jax.ShapeDtypeStruct((T, D_MODEL), out_dtype),
        scratch_shapes=[pltpu.VMEM((ML_HEADS, ML_DK, ML_DV + LANES), F32),
                        pltpu.VMEM((SUBLANES, LANES), F32)],
        compiler_params=_params("parallel", "arbitrary"),
        name="mlstm_bwd" if reverse else "mlstm_fwd")(*args)


def _mlstm_layer(x, xb, B, S, w_in, gate_b, norm_g, w_out, ln_g, ln_b, w_router):
    del xb
    qk = ML_HEADS * ML_DK
    w = w_in.astype(BF16)
    w_qv = jnp.concatenate([w[:, :qk], w[:, 2 * qk:2 * qk + D_MODEL]], axis=1)
    w_k_t = w[:, qk:2 * qk].T
    w_o = w[:, 2 * qk + D_MODEL:2 * qk + 2 * D_MODEL]
    w_g_t = w[:, 2 * qk + 2 * D_MODEL:].T
    b_g = gate_b.astype(F32).reshape(16, 1)
    qv, o, kt, gr = _dense(x, [w_qv, w_o], [BF16, F32],
                           transposed=[(w_k_t, jnp.zeros((qk, 1), F32), BF16), (w_g_t, b_g, F32)])
    hf = _mlstm_pass(qv, kt, gr, B, S, False)
    a = _mlstm_pass(qv, kt, gr, B, S, True, hf, o, norm_g.astype(F32))
    return _outproj_ln(a, w_out.astype(BF16), x, ln_g, ln_b, w_router)


def _cumsum_rows(x, n, reverse):
    row = lax.broadcasted_iota(jnp.int32, x.shape, 0)
    sh = 1
    while sh < n:
        if reverse:
            x = x + jnp.where(row < n - sh, pltpu.roll(x, n - sh, axis=0), 0.0)
        else:
            x = x + jnp.where(row >= sh, pltpu.roll(x, sh, axis=0), 0.0)
        sh *= 2
    return x


def _gla_kernel(*refs, reverse, ts, chunk):
    if reverse:
        qkv_ref, glr_ref, gw_ref, gb_ref, of_ref, r_ref, ng_ref, out_ref, st_ref, la_ref = refs
    else:
        qkv_ref, glr_ref, gw_ref, gb_ref, out_ref, st_ref, la_ref = refs
    H, DK, DV, L = GLA_HEADS, GLA_DK, GLA_DV, chunk
    scale = DK ** -0.5
    d = 1 if reverse else 0

    @pl.when(pl.program_id(1) == 0)
    def _():
        st_ref[...] = jnp.zeros_like(st_ref)

    glr = glr_ref[:, d * GLA_RANK:(d + 1) * GLA_RANK]
    g0, g1, _ = _split3(glr)
    pre = _dot(jnp.concatenate([g0, g1, g0], axis=1), gw_ref[d]) + gb_ref[d:d + 1, :]
    la_ref[...] = _log_sigmoid(pre) * (1.0 / GLA_TAU)

    rows = lax.broadcasted_iota(jnp.int32, (L, L), 0)
    cols = lax.broadcasted_iota(jnp.int32, (L, L), 1)
    mask = (cols >= rows) if reverse else (cols <= rows)
    last = 0 if reverse else L - 1
    mid = L // 2
    n_chunks = ts // L

    def body(i, carry):
        c = (n_chunks - 1 - i) if reverse else i
        r0 = pl.multiple_of(c * L, L)
        bsum = _cumsum_rows(la_ref[pl.ds(r0, L), :], L, reverse)
        for h in range(H):
            b = bsum[:, h * DK:(h + 1) * DK]
            qf = qkv_ref[pl.ds(r0, L), h * DK:(h + 1) * DK].astype(F32)
            kf = qkv_ref[pl.ds(r0, L), H * DK + h * DK:H * DK + (h + 1) * DK].astype(F32)
            vb = qkv_ref[pl.ds(r0, L), 2 * H * DK + h * DV:2 * H * DK + (h + 1) * DV]
            beta = b[mid:mid + 1, :]
            g = b[last:last + 1, :]
            qt = (qf * jnp.exp(b - beta)).astype(BF16)
            kt = (kf * jnp.exp(beta - b)).astype(BF16)
            amat = jnp.where(mask, _dot_nt(qt, kt) * scale, 0.0)
            qh = (qf * (jnp.exp(b) * scale)).astype(BF16)
            st = st_ref[h]
            o = _dot(amat.astype(BF16), vb) + _dot_nt(qh, st.astype(BF16))
            kh = (kf * jnp.exp(g - b)).astype(BF16)
            st_ref[h] = st * jnp.exp(g) + _dot_tn(vb, kh)
            if reverse:
                hs = of_ref[pl.ds(r0, L), h * DV:(h + 1) * DV] + o
                mu = jnp.mean(hs, axis=-1, keepdims=True)
                dd = hs - mu
                var = jnp.mean(dd * dd, axis=-1, keepdims=True)
                hn = dd * lax.rsqrt(var + LN_EPS) * ng_ref[:, h * DV:(h + 1) * DV]
                rr = r_ref[pl.ds(r0, L), h * DV:(h + 1) * DV]
                out_ref[pl.ds(r0, L), h * DV:(h + 1) * DV] = (rr * _sigmoid(rr) * hn).astype(out_ref.dtype)
            else:
                out_ref[pl.ds(r0, L), h * DV:(h + 1) * DV] = o
        return carry

    lax.fori_loop(0, n_chunks, body, 0, unroll=4)


def _gla_pass(qkv, glr, gw, gb, B, S, reverse, of=None, r=None, ng=None):
    T = B * S
    ts = min(SEQ_TILE, S)
    chunk = min(GLA_CHUNK, ts)
    nb = S // ts
    if reverse:
        blk = lambda b, j: (b * nb + nb - 1 - j, 0)
    else:
        blk = lambda b, j: (b * nb + j, 0)
    fix2 = lambda b, j: (0, 0)
    fix3 = lambda b, j: (0, 0, 0)
    in_specs = [pl.BlockSpec((ts, qkv.shape[1]), blk), pl.BlockSpec((ts, 2 * GLA_RANK), blk),
                pl.BlockSpec(gw.shape, fix3), pl.BlockSpec(gb.shape, fix2)]
    args = [qkv, glr, gw, gb]
    if reverse:
        in_specs += [pl.BlockSpec((ts, D_MODEL), blk), pl.BlockSpec((ts, D_MODEL), blk),
                     pl.BlockSpec((1, D_MODEL), fix2)]
        args += [of, r, ng.reshape(1, D_MODEL)]
    kern = functools.partial(_gla_kernel, reverse=reverse, ts=ts, chunk=chunk)
    return pl.pallas_call(
        kern, grid=(B, nb), in_specs=in_specs,
        out_specs=pl.BlockSpec((ts, D_MODEL), blk),
        out_shape=jax.ShapeDtypeStruct((T, D_MODEL), BF16 if reverse else F32),
        scratch_shapes=[pltpu.VMEM((GLA_HEADS, GLA_DV, GLA_DK), F32),
                        pltpu.VMEM((ts, GLA_HEADS * GLA_DK), F32)],
        compiler_params=_params("parallel", "arbitrary"),
        name="gla_bwd" if reverse else "gla_fwd")(*args)


def _gla_layer(x, xb, B, S, w_in, gate_w, gate_b, norm_g, w_out, ln_g, ln_b, w_router):
    del xb
    qk = GLA_HEADS * GLA_DK
    w = w_in.astype(BF16)
    w_qkv = w[:, :2 * qk + D_MODEL]
    w_r = w[:, 2 * qk + D_MODEL:2 * qk + 2 * D_MODEL]
    w_glr = w[:, 2 * qk + 2 * D_MODEL:]
    qkv, r, glr = _dense(x, [w_qkv, w_r, w_glr], [BF16, F32, F32])
    gw0 = gate_w.astype(BF16)
    gw1 = (gate_w.astype(F32) - gw0.astype(F32)).astype(BF16)
    gw = jnp.concatenate([gw0, gw0, gw1], axis=1)
    gb = gate_b.astype(F32)
    of = _gla_pass(qkv, glr, gw, gb, B, S, False)
    a = _gla_pass(qkv, glr, gw, gb, B, S, True, of, r, norm_g.astype(F32))
    return _outproj_ln(a, w_out.astype(BF16), x, ln_g, ln_b, w_router)


def _gelu_tanh(x):
    return 0.5 * x * (1.0 + jnp.tanh(math.sqrt(2.0 / math.pi) * (x + 0.044715 * (x * x * x))))


def _lru_kernel(*refs, reverse, ts):
    if reverse:
        (u_ref, up_ref, un_ref, cw_ref, cb_ref, wg_ref, bg_ref, lam_ref, hf_ref, gate_ref,
         out_ref, a_ref, g_ref, h_ref, hs_ref) = refs
    else:
        (u_ref, up_ref, un_ref, cw_ref, cb_ref, wg_ref, bg_ref, lam_ref,
         out_ref, a_ref, g_ref, h_ref) = refs
        hs_ref = out_ref
    W = LRU_WIDTH
    j = pl.program_id(1)
    nb = pl.num_programs(1)
    jj = (nb - 1 - j) if reverse else j

    @pl.when(j == 0)
    def _():
        h_ref[...] = jnp.zeros_like(h_ref)

    z = u_ref[...]
    prev = jnp.where(jj > 0, up_ref[...], 0.0)
    nxt = jnp.where(jj < nb - 1, un_ref[...], 0.0)
    row = lax.broadcasted_iota(jnp.int32, (ts, W), 0)
    zm1 = jnp.where(row == 0, prev[7:8, :], pltpu.roll(z, 1, axis=0))
    zm2 = pltpu.roll(z, 2, axis=0)
    zm2 = jnp.where(row == 0, prev[6:7, :], jnp.where(row == 1, prev[7:8, :], zm2))
    zp1 = jnp.where(row == ts - 1, nxt[0:1, :], pltpu.roll(z, ts - 1, axis=0))
    u = cw_ref[0:1, :] * zm2 + cw_ref[1:2, :] * zm1 + cw_ref[2:3, :] * z + cw_ref[3:4, :] * zp1 + cb_ref[...]

    ls = LRU_C * _log_sigmoid(lam_ref[...])
    ub = u.astype(BF16)
    for n in range(LRU_BLOCKS):
        sl = slice(n * LRU_BW, (n + 1) * LRU_BW)
        pre = _dot(ub[:, sl], wg_ref[n]) + bg_ref[n]
        r = _sigmoid(pre[:, :LRU_BW])
        ig = _sigmoid(pre[:, LRU_BW:])
        log_a = r * ls[:, sl]
        a = jnp.exp(log_a)
        a_ref[:, sl] = a
        om = 1.0 - a * a
        g_ref[:, sl] = jnp.where(om > 0.0, om * lax.rsqrt(om), 0.0) * (ig * u[:, sl])

    n_tiles = ts // SUBLANES

    srow = lax.broadcasted_iota(jnp.int32, (SUBLANES, W), 0)
    carry_row = 0 if reverse else SUBLANES - 1

    def body(i, h):
        t = (n_tiles - 1 - i) if reverse else i
        r0 = pl.multiple_of(t * SUBLANES, SUBLANES)
        a8 = a_ref[pl.ds(r0, SUBLANES), :]
        g8 = g_ref[pl.ds(r0, SUBLANES), :]
        d = 1
        while d < SUBLANES:
            shift = SUBLANES - d if reverse else d
            keep = (srow < SUBLANES - d) if reverse else (srow >= d)
            g8 = g8 + a8 * jnp.where(keep, pltpu.roll(g8, shift, axis=0), 0.0)
            a8 = a8 * jnp.where(keep, pltpu.roll(a8, shift, axis=0), 1.0)
            d *= 2
        hs = g8 + a8 * h
        hs_ref[pl.ds(r0, SUBLANES), :] = hs
        return hs[carry_row:carry_row + 1, :]

    h_ref[...] = lax.fori_loop(0, n_tiles, body, h_ref[...], unroll=8)
    if reverse:
        out_ref[...] = (_gelu_tanh(gate_ref[...]) * (hf_ref[...] + hs_ref[...])).astype(out_ref.dtype)


def _lru_pass(u, cw, cb, wg, bg, lam, B, S, reverse, hf=None, gate=None):
    T = B * S
    W = LRU_WIDTH
    ts = min(SEQ_TILE, S)
    nb = S // ts
    tpb = ts // SUBLANES
    n8 = T // SUBLANES
    if reverse:
        seq = lambda b, j: b * nb + nb - 1 - j
    else:
        seq = lambda b, j: b * nb + j
    blk = lambda b, j: (seq(b, j), 0)
    blk_prev = lambda b, j: (jnp.maximum(seq(b, j) * tpb - 1, 0), 0)
    blk_next = lambda b, j: (jnp.minimum((seq(b, j) + 1) * tpb, n8 - 1), 0)
    fix2 = lambda b, j: (0, 0)
    fix3 = lambda b, j: (0, 0, 0)
    in_specs = [pl.BlockSpec((ts, W), blk), pl.BlockSpec((SUBLANES, W), blk_prev),
                pl.BlockSpec((SUBLANES, W), blk_next), pl.BlockSpec(cw.shape, fix2),
                pl.BlockSpec(cb.shape, fix2), pl.BlockSpec(wg.shape, fix3), pl.BlockSpec(bg.shape, fix3),
                pl.BlockSpec(lam.shape, fix2)]
    args = [u, u, u, cw, cb, wg, bg, lam]
    scratch = [pltpu.VMEM((ts, W), F32), pltpu.VMEM((ts, W), F32), pltpu.VMEM((1, W), F32)]
    if reverse:
        in_specs += [pl.BlockSpec((ts, W), blk), pl.BlockSpec((ts, W), blk)]
        args += [hf, gate]
        scratch.append(pltpu.VMEM((ts, W), F32))
    kern = functools.partial(_lru_kernel, reverse=reverse, ts=ts)
    return pl.pallas_call(
        kern, grid=(B, nb), in_specs=in_specs,
        out_specs=pl.BlockSpec((ts, W), blk),
        out_shape=jax.ShapeDtypeStruct((T, W), BF16 if reverse else F32),
        scratch_shapes=scratch,
        compiler_params=_params("parallel", "arbitrary"),
        name="lru_bwd" if reverse else "lru_fwd")(*args)


def _lru_layer(x, xb, B, S, w_in, conv_w, conv_b, gate_a_w, gate_a_b, gate_x_w, gate_x_b, lam, w_out,
               ln_g, ln_b, w_router):
    del xb
    W = LRU_WIDTH
    w = w_in.astype(BF16)
    gate, u = _dense(x, [w[:, :W], w[:, W:]], [F32, F32])
    cw = conv_w.astype(F32)
    cb = conv_b.astype(F32).reshape(1, W)
    passes = []
    for d in range(2):
        wg = jnp.concatenate([gate_a_w[d], gate_x_w[d]], axis=-1).astype(BF16)
        bg = jnp.concatenate([gate_a_b[d].reshape(LRU_BLOCKS, 1, LRU_BW),
                              gate_x_b[d].reshape(LRU_BLOCKS, 1, LRU_BW)], axis=-1).astype(F32)
        passes.append((wg, bg, lam[d].astype(F32).reshape(1, W)))
    hf = _lru_pass(u, cw, cb, *passes[0], B, S, False)
    a = _lru_pass(u, cw, cb, *passes[1], B, S, True, hf, gate)
    return _outproj_ln(a, w_out.astype(BF16), x, ln_g, ln_b, w_router)


MLA_HW = MLA_NOPE + LANES


def _mla_proj_kernel(x_ref, pos_ref, win_ref, qg_ref, kg_ref, wqn_ref, wqr_ref, wqs_ref, wkv_ref,
                     fr_ref, sg_ref, q_ref, k_ref, v_ref):
    H = MLA_HEADS
    scale = (MLA_NOPE + MLA_ROPE) ** -0.5 * math.log2(math.e)
    xb = x_ref[...].astype(BF16)
    z = _dot(xb, win_ref[...])
    cq = z[:, :MLA_Q_RANK]
    ckv = z[:, MLA_Q_RANK:MLA_Q_RANK + MLA_KV_RANK]
    kr = z[:, MLA_Q_RANK + MLA_KV_RANK:MLA_Q_RANK + MLA_KV_RANK + LANES]
    krs = z[:, MLA_Q_RANK + MLA_KV_RANK + LANES:]
    qn = (cq * lax.rsqrt(jnp.mean(cq * cq, axis=-1, keepdims=True) + LN_EPS) * qg_ref[...]).astype(BF16)
    kvn = (ckv * lax.rsqrt(jnp.mean(ckv * ckv, axis=-1, keepdims=True) + LN_EPS) * kg_ref[...]).astype(BF16)
    ang = pos_ref[...].astype(F32) * fr_ref[...]
    cosv = jnp.cos(ang)
    lane = lax.broadcasted_iota(jnp.int32, ang.shape, 1)
    cosv = jnp.where(lane < MLA_ROPE, cosv, 0.0)
    sinv = jnp.sin(ang) * sg_ref[...]
    k_rope = kr * cosv + krs * sinv
    kv = _dot(kvn, wkv_ref[...])
    q_nope = _dot(qn, wqn_ref[...])
    q_rope = _dot(qn, wqr_ref[...])
    q_swap = _dot(qn, wqs_ref[...])
    ones_col = jnp.where(lane == 0, 1.0, 0.0).astype(BF16)
    for h in range(H):
        a0 = h * MLA_HW
        q_ref[:, a0:a0 + MLA_NOPE] = (q_nope[:, h * MLA_NOPE:(h + 1) * MLA_NOPE] * scale).astype(BF16)
        qr = q_rope[:, h * LANES:(h + 1) * LANES] * cosv + q_swap[:, h * LANES:(h + 1) * LANES] * sinv
        q_ref[:, a0 + MLA_NOPE:a0 + MLA_HW] = (qr * scale).astype(BF16)
        k_ref[:, a0:a0 + MLA_NOPE] = kv[:, h * 2 * MLA_NOPE:h * 2 * MLA_NOPE + MLA_NOPE].astype(BF16)
        k_ref[:, a0 + MLA_NOPE:a0 + MLA_HW] = k_rope.astype(BF16)
        v_ref[:, 2 * h * MLA_DV:(2 * h + 1) * MLA_DV] = kv[:, h * 2 * MLA_NOPE + MLA_NOPE:(h + 1) * 2 * MLA_NOPE].astype(BF16)
        v_ref[:, (2 * h + 1) * MLA_DV:(2 * h + 2) * MLA_DV] = ones_col


def _attn_kernel(q_ref, k_ref, v_ref, o_ref, *, n_sub):
    k = k_ref[...]
    v = v_ref[...]
    rows = q_ref.shape[0] // n_sub
    for i in range(n_sub):
        s = _dot_nt(q_ref[i * rows:(i + 1) * rows, :], k)
        p = jnp.exp2(s - jnp.max(s, axis=-1, keepdims=True))
        acc = _dot(p.astype(BF16), v)
        o_ref[i * rows:(i + 1) * rows, :] = (acc[:, :MLA_DV] / acc[:, MLA_DV:MLA_DV + 1]).astype(o_ref.dtype)


def _pad_rope_cols(w, swap):
    half = MLA_ROPE // 2
    if swap:
        w = jnp.concatenate([w[..., half:], w[..., :half]], axis=-1)
    w = jnp.concatenate([w, jnp.zeros_like(w)], axis=-1)
    return w.reshape(w.shape[0], -1)


def _mla_layer(x, xb, B, S, positions, w_in, q_norm_g, kv_norm_g, w_uq, w_ukv, w_out, ln_g, ln_b, w_router):
    del xb
    T = B * S
    H = MLA_HEADS
    half = MLA_ROPE // 2
    w_kr = w_in[:, MLA_Q_RANK + MLA_KV_RANK:].reshape(D_MODEL, 1, MLA_ROPE)
    win = jnp.concatenate([w_in[:, :MLA_Q_RANK + MLA_KV_RANK], _pad_rope_cols(w_kr, False),
                           _pad_rope_cols(w_kr, True)], axis=1).astype(BF16)
    wq = w_uq.reshape(MLA_Q_RANK, H, MLA_NOPE + MLA_ROPE)
    wqn = wq[:, :, :MLA_NOPE].reshape(MLA_Q_RANK, H * MLA_NOPE).astype(BF16)
    wqr = _pad_rope_cols(wq[:, :, MLA_NOPE:], False).astype(BF16)
    wqs = _pad_rope_cols(wq[:, :, MLA_NOPE:], True).astype(BF16)
    freq = ROPE_THETA ** (-jnp.arange(half, dtype=F32) / half)
    zeros = jnp.zeros((LANES - MLA_ROPE,), F32)
    fr = jnp.concatenate([freq, freq, zeros]).reshape(1, LANES)
    sg = jnp.concatenate([-jnp.ones((half,), F32), jnp.ones((half,), F32), zeros]).reshape(1, LANES)
    tm = min(ROW_TILE, T)
    row = lambda i: (i, 0)
    fix = lambda i: (0, 0)
    ins = [x, positions.reshape(T, 1), win, q_norm_g.astype(F32).reshape(1, -1),
           kv_norm_g.astype(F32).reshape(1, -1), wqn, wqr, wqs, w_ukv.astype(BF16), fr, sg]
    in_specs = [pl.BlockSpec((tm, D_MODEL), row), pl.BlockSpec((tm, 1), row)]
    in_specs += [pl.BlockSpec(a.shape, fix) for a in ins[2:]]
    q, k, v = pl.pallas_call(
        _mla_proj_kernel, grid=(T // tm,), in_specs=in_specs,
        out_specs=[pl.BlockSpec((tm, H * MLA_HW), row), pl.BlockSpec((tm, H * MLA_HW), row),
                   pl.BlockSpec((tm, 2 * H * MLA_DV), row)],
        out_shape=[jax.ShapeDtypeStruct((T, H * MLA_HW), BF16), jax.ShapeDtypeStruct((T, H * MLA_HW), BF16),
                   jax.ShapeDtypeStruct((T, 2 * H * MLA_DV), BF16)],
        compiler_params=_params("parallel"), name="mla_proj")(*ins)
    tq = min(Q_TILE, S)
    nq = S // tq
    att = pl.pallas_call(
        functools.partial(_attn_kernel, n_sub=tq // min(Q_SUB, tq)), grid=(B, H, nq),
        in_specs=[pl.BlockSpec((tq, MLA_HW), lambda b, h, i: (b * nq + i, h)),
                  pl.BlockSpec((S, MLA_HW), lambda b, h, i: (b, h)),
                  pl.BlockSpec((S, 2 * MLA_DV), lambda b, h, i: (b, h))],
        out_specs=pl.BlockSpec((tq, MLA_DV), lambda b, h, i: (b * nq + i, h)),
        out_shape=jax.ShapeDtypeStruct((T, H * MLA_DV), BF16),
        compiler_params=_params("parallel", "parallel", "arbitrary"), name="mla_attn")(q, k, v)
    return _outproj_ln(att, w_out.astype(BF16), x, ln_g, ln_b, w_router)


def _router_kernel(lg_ref, pos_ref, gate_ref, idx_ref, *, cap):
    logits = lg_ref[...]
    E, S = logits.shape
    mx = jnp.max(logits, axis=0, keepdims=True)
    ex = jnp.exp(logits - mx)
    aff = ex / jnp.sum(ex, axis=0, keepdims=True)
    bits = pltpu.bitcast(aff, jnp.int32)

    def count_ge(cand):
        return jnp.sum(jnp.where(bits >= cand, 1.0, 0.0), axis=1, keepdims=True)

    def bit_triple(i, thr):
        sh = 28 - 3 * i
        best = thr
        for k in range(1, 8):
            cand = thr | jnp.left_shift(jnp.int32(k), sh)
            best = jnp.where(count_ge(cand) >= cap, cand, best)
        return best

    thr = lax.fori_loop(0, 10, bit_triple, jnp.zeros((E, 1), jnp.int32))
    c0 = thr | 1
    thr = jnp.where(count_ge(c0) >= cap, c0, thr)
    gt = bits > thr
    eq = bits == thr
    need = cap - jnp.sum(jnp.where(gt, 1.0, 0.0), axis=1, keepdims=True)
    r = lax.broadcasted_iota(jnp.int32, (LANES, LANES), 0)
    c = lax.broadcasted_iota(jnp.int32, (LANES, LANES), 1)
    upper = jnp.where(r < c, 1.0, 0.0).astype(BF16)
    off = jnp.zeros((2 * E, 1), F32)
    for blk in range(S // LANES):
        sl = slice(blk * LANES, (blk + 1) * LANES)
        ind = jnp.concatenate([jnp.where(gt[:, sl], 1.0, 0.0), jnp.where(eq[:, sl], 1.0, 0.0)], axis=0)
        pre = _dot(ind.astype(BF16), upper) + off
        off = off + jnp.sum(ind, axis=1, keepdims=True)
        pg, pe = pre[:E], pre[E:]
        sel = gt[:, sl] | (eq[:, sl] & (pe < need))
        slot = pg + jnp.minimum(pe, need)
        pos_ref[:, sl] = jnp.where(sel, slot, -1.0).astype(jnp.int32)

    slots = lax.broadcasted_iota(jnp.int32, (cap, S), 0)
    token = lax.broadcasted_iota(jnp.int32, (1, S), 1)
    tok_hi = (token >> 6).astype(F32)
    tok_lo = (token & 63).astype(F32)
    pad = jnp.zeros((3, S), F32)
    for e in range(E):
        onehot = jnp.where(slots == pos_ref[e:e + 1, :], 1.0, 0.0).astype(BF16)
        a = aff[e:e + 1, :]
        a0 = a.astype(BF16).astype(F32)
        a1 = (a - a0).astype(BF16).astype(F32)
        a2 = a - a0 - a1
        vals = jnp.concatenate([tok_hi, tok_lo, a0, a1, a2, pad], axis=0).astype(BF16)
        res = _dot_nt(vals, onehot)
        idx_ref[e:e + 1, :] = (res[0:1, :] * 64.0 + res[1:2, :]).astype(jnp.int32)
        gate_ref[e:e + 1, :] = res[2:3, :] + res[3:4, :] + res[4:5, :]


def _gather_kernel(idx_ref, xp_ref, xin_ref, *, n_rows):
    def gather(t, carry):
        j0 = pl.multiple_of(t * SUBLANES, SUBLANES)
        rows = [xp_ref[pl.ds(idx_ref[0, 0, j0 + k], 1), :] for k in range(SUBLANES)]
        xin_ref[pl.ds(j0, SUBLANES), :] = jnp.concatenate(rows, axis=0)
        return carry

    lax.fori_loop(0, n_rows // SUBLANES, gather, 0, unroll=2)


def _ffn_kernel(xin_ref, gate_ref, wg_ref, wu_ref, wd_ref, out_ref, wgb_ref, wub_ref, wdb_ref):
    G, _, cap, half = xin_ref.shape

    @pl.when(pl.program_id(1) == 0)
    def _():
        wgb_ref[...] = wg_ref[0, 0].astype(BF16)
        wub_ref[...] = wu_ref[0, 0].astype(BF16)
        wdb_ref[...] = wd_ref[0, 0].astype(BF16)

    words = xin_ref[...].reshape(G * cap, half)
    x_lo = pltpu.bitcast(words << 16, F32).astype(BF16)
    x_hi = pltpu.bitcast(words & jnp.uint32(0xFFFF0000), F32).astype(BF16)
    hg = _dot(x_lo, wgb_ref[:half, :]) + _dot(x_hi, wgb_ref[half:, :])
    hu = _dot(x_lo, wub_ref[:half, :]) + _dot(x_hi, wub_ref[half:, :])
    hmid = (hg * _sigmoid(hg) * hu).astype(BF16)
    out = _dot(hmid, wdb_ref[...])
    eye = (lax.broadcasted_iota(jnp.int32, (cap, cap), 0) == lax.broadcasted_iota(jnp.int32, (cap, cap), 1))
    for g in range(G):
        gate = jnp.sum(jnp.where(eye, gate_ref[g, 0], 0.0), axis=1, keepdims=True)
        out_ref[g, 0] = (out[g * cap:(g + 1) * cap, :] * gate).astype(out_ref.dtype)


def _combine_kernel(outs_ref, pos_ref, x_ref, g_ref, b_ref, o_ref, *, cap):
    E, ts = pos_ref.shape
    posf = pos_ref[...].astype(F32)
    pos_t = jnp.concatenate([posf, jnp.full((LANES - E, ts), -1.0, F32)], axis=0).T
    lane = lax.broadcasted_iota(jnp.int32, (ts, cap), 1).astype(F32)
    onehot = jnp.concatenate(
        [jnp.where(pos_t[:, e:e + 1] == lane, 1.0, 0.0).astype(BF16) for e in range(E)], axis=1)
    y = _dot(onehot, outs_ref[...])
    o_ref[...] = _layer_norm(ALPHA * x_ref[...] + y, g_ref[...], b_ref[...])


def _moe_layer(x, xb, logits, B, S, layer, w_gate, w_up, w_down, ln_g, ln_b):
    T = B * S
    E = N_EXPERTS
    D = D_MODEL
    cap = CAPACITY_FACTOR * S // E
    pos, gate, idx = pl.pallas_call(
        functools.partial(_router_kernel, cap=cap), grid=(B,),
        in_specs=[pl.BlockSpec((E, S), lambda b: (0, b))],
        out_specs=[pl.BlockSpec((E, S), lambda b: (b, 0)), pl.BlockSpec((E, cap), lambda b: (b, 0)),
                   pl.BlockSpec((E, cap), lambda b: (b, 0))],
        out_shape=[jax.ShapeDtypeStruct((B * E, S), jnp.int32), jax.ShapeDtypeStruct((B * E, cap), F32),
                   jax.ShapeDtypeStruct((B * E, cap), jnp.int32)],
        compiler_params=_params("parallel"), name="moe_router")(logits)
    xin = pl.pallas_call(
        functools.partial(_gather_kernel, n_rows=E * cap), grid=(B,),
        in_specs=[pl.BlockSpec((1, 1, E * cap), lambda b: (b, 0, 0), memory_space=pltpu.SMEM),
                  pl.BlockSpec((S, D // 2), lambda b: (b, 0))],
        out_specs=pl.BlockSpec((E * cap, D // 2), lambda b: (b, 0)),
        out_shape=jax.ShapeDtypeStruct((B * E * cap, D // 2), jnp.uint32),
        compiler_params=_params("parallel"), name="moe_gather")(idx.reshape(B, 1, E * cap), xb)
    ff = w_gate.shape[-1]
    G = math.gcd(B, FFN_GROUP)
    outs = pl.pallas_call(
        _ffn_kernel, grid=(E, B // G),
        in_specs=[pl.BlockSpec((G, 1, cap, D // 2), lambda e, b: (b, e, 0, 0)),
                  pl.BlockSpec((G, 1, 1, cap), lambda e, b: (b, e, 0, 0)),
                  pl.BlockSpec((1, 1, D, ff), lambda e, b: (layer, e, 0, 0)),
                  pl.BlockSpec((1, 1, D, ff), lambda e, b: (layer, e, 0, 0)),
                  pl.BlockSpec((1, 1, ff, D), lambda e, b: (layer, e, 0, 0))],
        out_specs=pl.BlockSpec((G, 1, cap, D), lambda e, b: (b, e, 0, 0)),
        out_shape=jax.ShapeDtypeStruct((B, E, cap, D), BF16),
        scratch_shapes=[pltpu.VMEM((D, ff), BF16), pltpu.VMEM((D, ff), BF16), pltpu.VMEM((ff, D), BF16)],
        compiler_params=_params("arbitrary", "arbitrary"), name="moe_ffn")(
            xin.reshape(B, E, cap, D // 2), gate.reshape(B, E, 1, cap), w_gate, w_up, w_down)
    ts = min(COMBINE_TILE, S)
    nb = S // ts
    return pl.pallas_call(
        functools.partial(_combine_kernel, cap=cap), grid=(B, nb),
        in_specs=[pl.BlockSpec((E * cap, D), lambda b, j: (b, 0)),
                  pl.BlockSpec((E, ts), lambda b, j: (b, j)),
                  pl.BlockSpec((ts, D), lambda b, j: (b * nb + j, 0)),
                  pl.BlockSpec((1, D), lambda b, j: (0, 0)), pl.BlockSpec((1, D), lambda b, j: (0, 0))],
        out_specs=pl.BlockSpec((ts, D), lambda b, j: (b * nb + j, 0)),
        out_shape=jax.ShapeDtypeStruct((T, D), F32),
        compiler_params=_params("parallel", "arbitrary"), name="moe_combine")(
            outs.reshape(B * E * cap, D), pos, x, ln_g.reshape(1, D), ln_b.reshape(1, D))


def kernel(x, positions, mlstm_w_in, mlstm_gate_b, mlstm_norm_g, mlstm_w_out, gla_w_in, gla_gate_w, gla_gate_b, gla_norm_g, gla_w_out, lru_w_in, lru_conv_w, lru_conv_b, lru_gate_a_w, lru_gate_a_b, lru_gate_x_w, lru_gate_x_b, lru_lambda, lru_w_out, mla_w_in, mla_q_norm_g, mla_kv_norm_g, mla_w_uq, mla_w_ukv, mla_w_out, moe_router, moe_w_gate, moe_w_up, moe_w_down, ln_g, ln_b):
    B, S, D = x.shape
    xf = x.reshape(B * S, D)
    xb = None
    for i in range(DEPTH):
        m = i % N_MIXERS
        j = i // N_MIXERS
        g0, b0, wr = ln_g[i, 0], ln_b[i, 0], moe_router[i]
        if m == 0:
            xf, xb, lg = _mlstm_layer(xf, xb, B, S, mlstm_w_in[j], mlstm_gate_b[j], mlstm_norm_g[j],
                                      mlstm_w_out[j], g0, b0, wr)
        elif m == 1:
            xf, xb, lg = _gla_layer(xf, xb, B, S, gla_w_in[j], gla_gate_w[j], gla_gate_b[j], gla_norm_g[j],
                                    gla_w_out[j], g0, b0, wr)
        elif m == 2:
            xf, xb, lg = _lru_layer(xf, xb, B, S, lru_w_in[j], lru_conv_w[j], lru_conv_b[j], lru_gate_a_w[j],
                                    lru_gate_a_b[j], lru_gate_x_w[j], lru_gate_x_b[j], lru_lambda[j],
                                    lru_w_out[j], g0, b0, wr)
        else:
            xf, xb, lg = _mla_layer(xf, xb, B, S, positions, mla_w_in[j], mla_q_norm_g[j], mla_kv_norm_g[j],
                                    mla_w_uq[j], mla_w_ukv[j], mla_w_out[j], g0, b0, wr)
        xf = _moe_layer(xf, xb, lg, B, S, i, moe_w_gate, moe_w_up, moe_w_down, ln_g[i, 1], ln_b[i, 1])
    return xf.reshape(B, S, D)
```

```python
import functools
import math

import jax
import jax.numpy as jnp
from jax import lax
from jax.experimental import pallas as pl
from jax.experimental.pallas import tpu as pltpu

F32 = jnp.float32
BF16 = jnp.bfloat16

D_MODEL = 1024
DEPTH = 4
N_MIXERS = 4
ALPHA = (2 * DEPTH) ** 0.25
LN_EPS = 1e-5

ML_HEADS = 4
ML_DV = D_MODEL // ML_HEADS
ML_DK = ML_DV // 2

GLA_HEADS = 4
GLA_DK = D_MODEL // 2 // GLA_HEADS
GLA_DV = D_MODEL // GLA_HEADS
GLA_RANK = 16
GLA_TAU = 16.0

LRU_WIDTH = D_MODEL
LRU_BLOCKS = 4
LRU_BW = LRU_WIDTH // LRU_BLOCKS
CONV_WIDTH = 4
LRU_C = 8.0

MLA_HEADS = 8
MLA_NOPE = 128
MLA_ROPE = 64
MLA_DV = 128
MLA_Q_RANK = 384
MLA_KV_RANK = 256
ROPE_THETA = 10000.0

N_EXPERTS = 16
CAPACITY_FACTOR = 2

V7X_VMEM_BYTES = 64 * 1024 * 1024
VMEM_LIMIT = V7X_VMEM_BYTES - 8 * 1024 * 1024
LANES = 128
SUBLANES = 8

ROW_TILE = 1024
SEQ_TILE = 1024
ML_CHUNK = 512
GLA_CHUNK = 128
Q_TILE = 2048
Q_SUB = 256
N_CHUNK = 512
FFN_GROUP = 4
COMBINE_TILE = 1024
COMBINE_CHAINS = 4
FF_CHUNK = 512


def _params(*sem):
    return pltpu.CompilerParams(dimension_semantics=sem, vmem_limit_bytes=VMEM_LIMIT)


def _log_sigmoid(x):
    return jnp.minimum(x, 0.0) - jnp.log(1.0 + jnp.exp(-jnp.abs(x)))


def _sigmoid(x):
    return 0.5 * jnp.tanh(0.5 * x) + 0.5


def _layer_norm(v, g, b):
    mu = jnp.mean(v, axis=-1, keepdims=True)
    d = v - mu
    var = jnp.mean(d * d, axis=-1, keepdims=True)
    return d * lax.rsqrt(var + LN_EPS) * g + b


def _dot(a, b):
    return jnp.dot(a, b, preferred_element_type=F32)


def _dot_nt(a, b):
    return lax.dot_general(a, b, (((1,), (1,)), ((), ())), preferred_element_type=F32)


def _split3(a):
    a0 = a.astype(BF16)
    r1 = a - a0.astype(F32)
    a1 = r1.astype(BF16)
    a2 = (r1 - a1.astype(F32)).astype(BF16)
    return a0, a1, a2


def _dot_tn(a, b):
    return lax.dot_general(a, b, (((0,), (0,)), ((), ())), preferred_element_type=F32)


def _dense_kernel(*refs, n_w, n_t, has_bias):
    x_ref = refs[0]
    pos = 1
    w_refs = refs[pos:pos + n_w]
    pos += n_w
    b_refs = []
    for hb in has_bias:
        if hb:
            b_refs.append(refs[pos])
            pos += 1
        else:
            b_refs.append(None)
    t_refs = refs[pos:pos + 2 * n_t]
    pos += 2 * n_t
    o_refs = refs[pos:pos + n_w]
    pos += n_w
    ot_refs = refs[pos:pos + n_t]

    xb = x_ref[...].astype(BF16)
    for w_ref, b_ref, o_ref in zip(w_refs, b_refs, o_refs):
        n = w_ref.shape[1]
        for j0 in range(0, n, N_CHUNK):
            j1 = min(n, j0 + N_CHUNK)
            acc = _dot(xb, w_ref[:, j0:j1])
            if b_ref is not None:
                acc = acc + b_ref[:, j0:j1]
            o_ref[:, j0:j1] = acc.astype(o_ref.dtype)
    for i in range(n_t):
        wt_ref, bt_ref = t_refs[2 * i], t_refs[2 * i + 1]
        ot_refs[i][...] = (_dot_nt(wt_ref[...], xb) + bt_ref[...]).astype(ot_refs[i].dtype)


def _dense(x, ws, dtypes, biases=None, transposed=()):
    T, K = x.shape
    tm = min(ROW_TILE, T)
    if biases is None:
        biases = [None] * len(ws)
    has_bias = tuple(b is not None for b in biases)
    args = [x] + list(ws) + [b for b in biases if b is not None]
    in_specs = [pl.BlockSpec((tm, K), lambda i: (i, 0))]
    in_specs += [pl.BlockSpec(w.shape, lambda i: (0, 0)) for w in ws]
    in_specs += [pl.BlockSpec(b.shape, lambda i: (0, 0)) for b in biases if b is not None]
    for wt, bt, _ in transposed:
        args += [wt, bt]
        in_specs += [pl.BlockSpec(wt.shape, lambda i: (0, 0)), pl.BlockSpec(bt.shape, lambda i: (0, 0))]
    out_shape = [jax.ShapeDtypeStruct((T, w.shape[1]), dt) for w, dt in zip(ws, dtypes)]
    out_specs = [pl.BlockSpec((tm, w.shape[1]), lambda i: (i, 0)) for w in ws]
    for wt, _, dt in transposed:
        out_shape.append(jax.ShapeDtypeStruct((wt.shape[0], T), dt))
        out_specs.append(pl.BlockSpec((wt.shape[0], tm), lambda i: (0, i)))
    kern = functools.partial(_dense_kernel, n_w=len(ws), n_t=len(transposed), has_bias=has_bias)
    return pl.pallas_call(
        kern, grid=(T // tm,), in_specs=in_specs, out_specs=out_specs, out_shape=out_shape,
        compiler_params=_params("parallel"), name="dense")(*args)


def _outproj_ln_kernel(a_ref, w_ref, x_ref, g_ref, b_ref, wr_ref, o_ref, ob_ref, lg_ref):
    y = _dot(a_ref[...], w_ref[...])
    v = _layer_norm(ALPHA * x_ref[...] + y, g_ref[...], b_ref[...])
    o_ref[...] = v
    E = lg_ref.shape[0]
    vh = v.astype(BF16)
    vl = (v - vh.astype(F32)).astype(BF16)
    both = _dot_nt(wr_ref[...], vh)
    lg_ref[...] = both[:E] + (both[E:] + _dot_nt(wr_ref[:E, :], vl))
    half = v.shape[1] // 2
    bits = pltpu.bitcast(v.astype(BF16).astype(F32), jnp.uint32)
    ob_ref[...] = (bits[:, :half] >> 16) | bits[:, half:]


def _outproj_ln(a, w, x, g, b, w_router):
    T, K = a.shape
    E = w_router.shape[1]
    wr = w_router.astype(F32).T
    wh = wr.astype(BF16)
    wr2 = jnp.concatenate([wh, (wr - wh.astype(F32)).astype(BF16)], axis=0)
    D = w.shape[1]
    tm = min(ROW_TILE, T)
    row = lambda i: (i, 0)
    fix = lambda i: (0, 0)
    return pl.pallas_call(
        _outproj_ln_kernel, grid=(T // tm,),
        in_specs=[pl.BlockSpec((tm, K), row), pl.BlockSpec((K, D), fix), pl.BlockSpec((tm, D), row),
                  pl.BlockSpec((1, D), fix), pl.BlockSpec((1, D), fix), pl.BlockSpec((2 * E, D), fix)],
        out_specs=[pl.BlockSpec((tm, D), row), pl.BlockSpec((tm, D // 2), row),
                   pl.BlockSpec((E, tm), lambda i: (0, i))],
        out_shape=[jax.ShapeDtypeStruct((T, D), F32), jax.ShapeDtypeStruct((T, D // 2), jnp.uint32),
                   jax.ShapeDtypeStruct((E, T), F32)],
        compiler_params=_params("parallel"), name="outproj_ln")(a, w, x, g.reshape(1, D), b.reshape(1, D), wr2)


def _mlstm_kernel(*refs, reverse, ts, chunk):
    if reverse:
        qv_ref, kt_ref, gr_ref, hf_ref, o_ref, ng_ref, out_ref, c_ref, m_ref = refs
    else:
        qv_ref, kt_ref, gr_ref, out_ref, c_ref, m_ref = refs
    H, DK, DV, L = ML_HEADS, ML_DK, ML_DV, chunk
    scale = DK ** -0.5

    @pl.when(pl.program_id(1) == 0)
    def _():
        c_ref[...] = jnp.zeros_like(c_ref)
        m_ref[...] = jnp.zeros_like(m_ref)

    rows = lax.broadcasted_iota(jnp.int32, (L, L), 0)
    cols = lax.broadcasted_iota(jnp.int32, (L, L), 1)
    mask = (cols >= rows) if reverse else (cols <= rows)
    eye = rows == cols
    tri = jnp.where((rows >= cols) if reverse else (rows <= cols), 1.0, 0.0).astype(BF16)
    ones_col = jnp.where(lax.broadcasted_iota(jnp.int32, (L, LANES), 1) == 0, 1.0, 0.0).astype(BF16)
    d0 = 8 if reverse else 0
    last = 0 if reverse else L - 1
    n_chunks = ts // L
    order = range(n_chunks - 1, -1, -1) if reverse else range(n_chunks)
    neg_inf = -jnp.inf

    for c in order:
        r0 = c * L
        g8 = gr_ref[d0:d0 + 8, r0:r0 + L]
        lf8 = _log_sigmoid(g8)
        pieces = _dot(jnp.concatenate(_split3(lf8), axis=0), tri)
        b8 = pieces[0:8] + pieces[8:16] + pieces[16:24]
        u8 = g8[0:4, :] - b8[4:8, :]
        for h in range(H):
            qb = qv_ref[r0:r0 + L, h * DK:(h + 1) * DK]
            kt = kt_ref[h * DK:(h + 1) * DK, r0:r0 + L]
            vx = jnp.concatenate([qv_ref[r0:r0 + L, H * DK + h * DV:H * DK + (h + 1) * DV], ones_col], axis=1)
            u_r = u8[h:h + 1, :]
            b_r = b8[4 + h:5 + h, :]
            m_prev = m_ref[h:h + 1, 0:1]
            b_c = jnp.sum(jnp.where(eye, b_r, 0.0), axis=1, keepdims=True)
            um = jnp.where(mask, u_r, neg_inf)
            a_c = jnp.maximum(m_prev, jnp.max(um, axis=1, keepdims=True))
            s = _dot(qb, kt) * (scale * jnp.exp(um - a_c))
            w_int = jnp.exp(m_prev - a_c) * scale
            c_old = c_ref[h]
            acc = _dot(s.astype(BF16), vx) + w_int * _dot(qb, c_old.astype(BF16))
            den = acc[:, DV:DV + 1]
            hh = acc[:, :DV] / jnp.maximum(jnp.abs(den), jnp.exp(-(a_c + b_c)))
            a_last = jnp.maximum(m_prev, jnp.max(u_r, axis=1, keepdims=True))
            g_tot = b_r[:, last:last + 1]
            wc = jnp.exp(m_prev - a_last)
            kw = (kt.astype(F32) * jnp.exp(u_r - a_last)).astype(BF16)
            c_ref[h] = wc * c_old + _dot(kw, vx)
            m_ref[h:h + 1, :] = jnp.broadcast_to(g_tot + a_last, (1, LANES))
            if reverse:
                hs = hf_ref[r0:r0 + L, h * DV:(h + 1) * DV] + hh
                mu = jnp.mean(hs, axis=-1, keepdims=True)
                dd = hs - mu
                var = jnp.mean(dd * dd, axis=-1, keepdims=True)
                hn = dd * lax.rsqrt(var + LN_EPS) * ng_ref[:, h * DV:(h + 1) * DV]
                og = _sigmoid(o_ref[r0:r0 + L, h * DV:(h + 1) * DV])
                out_ref[r0:r0 + L, h * DV:(h + 1) * DV] = (og * hn).astype(out_ref.dtype)
            else:
                out_ref[r0:r0 + L, h * DV:(h + 1) * DV] = hh


def _mlstm_pass(qv, kt, gr, B, S, reverse, hf=None, o=None, ng=None):
    T = B * S
    ts = min(SEQ_TILE, S)
    chunk = min(ML_CHUNK, ts)
    nb = S // ts
    if reverse:
        blk = lambda b, j: (b * nb + nb - 1 - j, 0)
        blk_t = lambda b, j: (0, b * nb + nb - 1 - j)
    else:
        blk = lambda b, j: (b * nb + j, 0)
        blk_t = lambda b, j: (0, b * nb + j)
    in_specs = [pl.BlockSpec((ts, qv.shape[1]), blk), pl.BlockSpec((kt.shape[0], ts), blk_t),
                pl.BlockSpec((16, ts), blk_t)]
    args = [qv, kt, gr]
    if reverse:
        in_specs += [pl.BlockSpec((ts, D_MODEL), blk), pl.BlockSpec((ts, D_MODEL), blk),
                     pl.BlockSpec((1, D_MODEL), lambda b, j: (0, 0))]
        args += [hf, o, ng.reshape(1, D_MODEL)]
    out_dtype = BF16 if reverse else F32
    kern = functools.partial(_mlstm_kernel, reverse=reverse, ts=ts, chunk=chunk)
    return pl.pallas_call(
        kern, grid=(B, nb), in_specs=in_specs,
        out_specs=pl.BlockSpec((ts, D_MODEL), blk),
        out_shape=jax.ShapeDtypeStruct((T, D_MODEL), out_dtype),
        scratch_shapes=[pltpu.VMEM((ML_HEADS, ML_DK, ML_DV + LANES), F32),
                        pltpu.VMEM((SUBLANES, LANES), F32)],
        compiler_params=_params("parallel", "arbitrary"),
        name="mlstm_bwd" if reverse else "mlstm_fwd")(*args)


def _mlstm_layer(x, xb, B, S, w_in, gate_b, norm_g, w_out, ln_g, ln_b, w_router):
    del xb
    qk = ML_HEADS * ML_DK
    w = w_in.astype(BF16)
    w_qv = jnp.concatenate([w[:, :qk], w[:, 2 * qk:2 * qk + D_MODEL]], axis=1)
    w_k_t = w[:, qk:2 * qk].T
    w_o = w[:, 2 * qk + D_MODEL:2 * qk + 2 * D_MODEL]
    w_g_t = w[:, 2 * qk + 2 * D_MODEL:].T
    b_g = gate_b.astype(F32).reshape(16, 1)
    qv, o, kt, gr = _dense(x, [w_qv, w_o], [BF16, F32],
                           transposed=[(w_k_t, jnp.zeros((qk, 1), F32), BF16), (w_g_t, b_g, F32)])
    hf = _mlstm_pass(qv, kt, gr, B, S, False)
    a = _mlstm_pass(qv, kt, gr, B, S, True, hf, o, norm_g.astype(F32))
    return _outproj_ln(a, w_out.astype(BF16), x, ln_g, ln_b, w_router)


def _cumsum_rows(x, n, reverse):
    row = lax.broadcasted_iota(jnp.int32, x.shape, 0)
    sh = 1
    while sh < n:
        if reverse:
            x = x + jnp.where(row < n - sh, pltpu.roll(x, n - sh, axis=0), 0.0)
        else:
            x = x + jnp.where(row >= sh, pltpu.roll(x, sh, axis=0), 0.0)
        sh *= 2
    return x


def _gla_kernel(*refs, reverse, ts, chunk):
    if reverse:
        qkv_ref, glr_ref, gw_ref, gb_ref, of_ref, r_ref, ng_ref, out_ref, st_ref, la_ref = refs
    else:
        qkv_ref, glr_ref, gw_ref, gb_ref, out_ref, st_ref, la_ref = refs
    H, DK, DV, L = GLA_HEADS, GLA_DK, GLA_DV, chunk
    scale = DK ** -0.5
    d = 1 if reverse else 0

    @pl.when(pl.program_id(1) == 0)
    def _():
        st_ref[...] = jnp.zeros_like(st_ref)

    glr = glr_ref[:, d * GLA_RANK:(d + 1) * GLA_RANK]
    g0, g1, _ = _split3(glr)
    pre = _dot(jnp.concatenate([g0, g1, g0], axis=1), gw_ref[d]) + gb_ref[d:d + 1, :]
    la_ref[...] = _log_sigmoid(pre) * (1.0 / GLA_TAU)

    rows = lax.broadcasted_iota(jnp.int32, (L, L), 0)
    cols = lax.broadcasted_iota(jnp.int32, (L, L), 1)
    mask = (cols >= rows) if reverse else (cols <= rows)
    last = 0 if reverse else L - 1
    mid = L // 2
    n_chunks = ts // L

    def body(i, carry):
        c = (n_chunks - 1 - i) if reverse else i
        r0 = pl.multiple_of(c * L, L)
        bsum = _cumsum_rows(la_ref[pl.ds(r0, L), :], L, reverse)
        for h in range(H):
            b = bsum[:, h * DK:(h + 1) * DK]
            qf = qkv_ref[pl.ds(r0, L), h * DK:(h + 1) * DK].astype(F32)
            kf = qkv_ref[pl.ds(r0, L), H * DK + h * DK:H * DK + (h + 1) * DK].astype(F32)
            vb = qkv_ref[pl.ds(r0, L), 2 * H * DK + h * DV:2 * H * DK + (h + 1) * DV]
            beta = b[mid:mid + 1, :]
            g = b[last:last + 1, :]
            qt = (qf * jnp.exp(b - beta)).astype(BF16)
            kt = (kf * jnp.exp(beta - b)).astype(BF16)
            amat = jnp.where(mask, _dot_nt(qt, kt) * scale, 0.0)
            qh = (qf * (jnp.exp(b) * scale)).astype(BF16)
            st = st_ref[h]
            o = _dot(amat.astype(BF16), vb) + _dot_nt(qh, st.astype(BF16))
            kh = (kf * jnp.exp(g - b)).astype(BF16)
            st_ref[h] = st * jnp.exp(g) + _dot_tn(vb, kh)
            if reverse:
                hs = of_ref[pl.ds(r0, L), h * DV:(h + 1) * DV] + o
                mu = jnp.mean(hs, axis=-1, keepdims=True)
                dd = hs - mu
                var = jnp.mean(dd * dd, axis=-1, keepdims=True)
                hn = dd * lax.rsqrt(var + LN_EPS) * ng_ref[:, h * DV:(h + 1) * DV]
                rr = r_ref[pl.ds(r0, L), h * DV:(h + 1) * DV]
                out_ref[pl.ds(r0, L), h * DV:(h + 1) * DV] = (rr * _sigmoid(rr) * hn).astype(out_ref.dtype)
            else:
                out_ref[pl.ds(r0, L), h * DV:(h + 1) * DV] = o
        return carry

    lax.fori_loop(0, n_chunks, body, 0, unroll=4)


def _gla_pass(qkv, glr, gw, gb, B, S, reverse, of=None, r=None, ng=None):
    T = B * S
    ts = min(SEQ_TILE, S)
    chunk = min(GLA_CHUNK, ts)
    nb = S // ts
    if reverse:
        blk = lambda b, j: (b * nb + nb - 1 - j, 0)
    else:
        blk = lambda b, j: (b * nb + j, 0)
    fix2 = lambda b, j: (0, 0)
    fix3 = lambda b, j: (0, 0, 0)
    in_specs = [pl.BlockSpec((ts, qkv.shape[1]), blk), pl.BlockSpec((ts, 2 * GLA_RANK), blk),
                pl.BlockSpec(gw.shape, fix3), pl.BlockSpec(gb.shape, fix2)]
    args = [qkv, glr, gw, gb]
    if reverse:
        in_specs += [pl.BlockSpec((ts, D_MODEL), blk), pl.BlockSpec((ts, D_MODEL), blk),
                     pl.BlockSpec((1, D_MODEL), fix2)]
        args += [of, r, ng.reshape(1, D_MODEL)]
    kern = functools.partial(_gla_kernel, reverse=reverse, ts=ts, chunk=chunk)
    return pl.pallas_call(
        kern, grid=(B, nb), in_specs=in_specs,
        out_specs=pl.BlockSpec((ts, D_MODEL), blk),
        out_shape=jax.ShapeDtypeStruct((T, D_MODEL), BF16 if reverse else F32),
        scratch_shapes=[pltpu.VMEM((GLA_HEADS, GLA_DV, GLA_DK), F32),
                        pltpu.VMEM((ts, GLA_HEADS * GLA_DK), F32)],
        compiler_params=_params("parallel", "arbitrary"),
        name="gla_bwd" if reverse else "gla_fwd")(*args)


def _gla_layer(x, xb, B, S, w_in, gate_w, gate_b, norm_g, w_out, ln_g, ln_b, w_router):
    del xb
    qk = GLA_HEADS * GLA_DK
    w = w_in.astype(BF16)
    w_qkv = w[:, :2 * qk + D_MODEL]
    w_r = w[:, 2 * qk + D_MODEL:2 * qk + 2 * D_MODEL]
    w_glr = w[:, 2 * qk + 2 * D_MODEL:]
    qkv, r, glr = _dense(x, [w_qkv, w_r, w_glr], [BF16, F32, F32])
    gw0 = gate_w.astype(BF16)
    gw1 = (gate_w.astype(F32) - gw0.astype(F32)).astype(BF16)
    gw = jnp.concatenate([gw0, gw0, gw1], axis=1)
    gb = gate_b.astype(F32)
    of = _gla_pass(qkv, glr, gw, gb, B, S, False)
    a = _gla_pass(qkv, glr, gw, gb, B, S, True, of, r, norm_g.astype(F32))
    return _outproj_ln(a, w_out.astype(BF16), x, ln_g, ln_b, w_router)


def _gelu_tanh(x):
    return 0.5 * x * (1.0 + jnp.tanh(math.sqrt(2.0 / math.pi) * (x + 0.044715 * (x * x * x))))


def _lru_kernel(*refs, reverse, ts):
    if reverse:
        (u_ref, up_ref, un_ref, cw_ref, cb_ref, wg_ref, bg_ref, lam_ref, hf_ref, gate_ref,
         out_ref, a_ref, g_ref, h_ref, hs_ref) = refs
    else:
        (u_ref, up_ref, un_ref, cw_ref, cb_ref, wg_ref, bg_ref, lam_ref,
         out_ref, a_ref, g_ref, h_ref) = refs
        hs_ref = out_ref
    W = LRU_WIDTH
    j = pl.program_id(1)
    nb = pl.num_programs(1)
    jj = (nb - 1 - j) if reverse else j

    @pl.when(j == 0)
    def _():
        h_ref[...] = jnp.zeros_like(h_ref)

    z = u_ref[...]
    prev = jnp.where(jj > 0, up_ref[...], 0.0)
    nxt = jnp.where(jj < nb - 1, un_ref[...], 0.0)
    row = lax.broadcasted_iota(jnp.int32, (ts, W), 0)
    zm1 = jnp.where(row == 0, prev[7:8, :], pltpu.roll(z, 1, axis=0))
    zm2 = pltpu.roll(z, 2, axis=0)
    zm2 = jnp.where(row == 0, prev[6:7, :], jnp.where(row == 1, prev[7:8, :], zm2))
    zp1 = jnp.where(row == ts - 1, nxt[0:1, :], pltpu.roll(z, ts - 1, axis=0))
    u = cw_ref[0:1, :] * zm2 + cw_ref[1:2, :] * zm1 + cw_ref[2:3, :] * z + cw_ref[3:4, :] * zp1 + cb_ref[...]

    ls = LRU_C * _log_sigmoid(lam_ref[...])
    ub = u.astype(BF16)
    for n in range(LRU_BLOCKS):
        sl = slice(n * LRU_BW, (n + 1) * LRU_BW)
        pre = _dot(ub[:, sl], wg_ref[n]) + bg_ref[n]
        r = _sigmoid(pre[:, :LRU_BW])
        ig = _sigmoid(pre[:, LRU_BW:])
        log_a = r * ls[:, sl]
        a = jnp.exp(log_a)
        a_ref[:, sl] = a
        om = 1.0 - a * a
        g_ref[:, sl] = jnp.where(om > 0.0, om * lax.rsqrt(om), 0.0) * (ig * u[:, sl])

    n_tiles = ts // SUBLANES

    srow = lax.broadcasted_iota(jnp.int32, (SUBLANES, W), 0)
    carry_row = 0 if reverse else SUBLANES - 1

    def body(i, h):
        t = (n_tiles - 1 - i) if reverse else i
        r0 = pl.multiple_of(t * SUBLANES, SUBLANES)
        a8 = a_ref[pl.ds(r0, SUBLANES), :]
        g8 = g_ref[pl.ds(r0, SUBLANES), :]
        d = 1
        while d < SUBLANES:
            shift = SUBLANES - d if reverse else d
            keep = (srow < SUBLANES - d) if reverse else (srow >= d)
            g8 = g8 + a8 * jnp.where(keep, pltpu.roll(g8, shift, axis=0), 0.0)
            a8 = a8 * jnp.where(keep, pltpu.roll(a8, shift, axis=0), 1.0)
            d *= 2
        hs = g8 + a8 * h
        hs_ref[pl.ds(r0, SUBLANES), :] = hs
        return hs[carry_row:carry_row + 1, :]

    h_ref[...] = lax.fori_loop(0, n_tiles, body, h_ref[...], unroll=8)
    if reverse:
        out_ref[...] = (_gelu_tanh(gate_ref[...]) * (hf_ref[...] + hs_ref[...])).astype(out_ref.dtype)


def _lru_pass(u, cw, cb, wg, bg, lam, B, S, reverse, hf=None, gate=None):
    T = B * S
    W = LRU_WIDTH
    ts = min(SEQ_TILE, S)
    nb = S // ts
    tpb = ts // SUBLANES
    n8 = T // SUBLANES
    if reverse:
        seq = lambda b, j: b * nb + nb - 1 - j
    else:
        seq = lambda b, j: b * nb + j
    blk = lambda b, j: (seq(b, j), 0)
    blk_prev = lambda b, j: (jnp.maximum(seq(b, j) * tpb - 1, 0), 0)
    blk_next = lambda b, j: (jnp.minimum((seq(b, j) + 1) * tpb, n8 - 1), 0)
    fix2 = lambda b, j: (0, 0)
    fix3 = lambda b, j: (0, 0, 0)
    in_specs = [pl.BlockSpec((ts, W), blk), pl.BlockSpec((SUBLANES, W), blk_prev),
                pl.BlockSpec((SUBLANES, W), blk_next), pl.BlockSpec(cw.shape, fix2),
                pl.BlockSpec(cb.shape, fix2), pl.BlockSpec(wg.shape, fix3), pl.BlockSpec(bg.shape, fix3),
                pl.BlockSpec(lam.shape, fix2)]
    args = [u, u, u, cw, cb, wg, bg, lam]
    scratch = [pltpu.VMEM((ts, W), F32), pltpu.VMEM((ts, W), F32), pltpu.VMEM((1, W), F32)]
    if reverse:
        in_specs += [pl.BlockSpec((ts, W), blk), pl.BlockSpec((ts, W), blk)]
        args += [hf, gate]
        scratch.append(pltpu.VMEM((ts, W), F32))
    kern = functools.partial(_lru_kernel, reverse=reverse, ts=ts)
    return pl.pallas_call(
        kern, grid=(B, nb), in_specs=in_specs,
        out_specs=pl.BlockSpec((ts, W), blk),
        out_shape=jax.ShapeDtypeStruct((T, W), BF16 if reverse else F32),
        scratch_shapes=scratch,
        compiler_params=_params("parallel", "arbitrary"),
        name="lru_bwd" if reverse else "lru_fwd")(*args)


def _lru_layer(x, xb, B, S, w_in, conv_w, conv_b, gate_a_w, gate_a_b, gate_x_w, gate_x_b, lam, w_out,
               ln_g, ln_b, w_router):
    del xb
    W = LRU_WIDTH
    w = w_in.astype(BF16)
    gate, u = _dense(x, [w[:, :W], w[:, W:]], [F32, F32])
    cw = conv_w.astype(F32)
    cb = conv_b.astype(F32).reshape(1, W)
    passes = []
    for d in range(2):
        wg = jnp.concatenate([gate_a_w[d], gate_x_w[d]], axis=-1).astype(BF16)
        bg = jnp.concatenate([gate_a_b[d].reshape(LRU_BLOCKS, 1, LRU_BW),
                              gate_x_b[d].reshape(LRU_BLOCKS, 1, LRU_BW)], axis=-1).astype(F32)
        passes.append((wg, bg, lam[d].astype(F32).reshape(1, W)))
    hf = _lru_pass(u, cw, cb, *passes[0], B, S, False)
    a = _lru_pass(u, cw, cb, *passes[1], B, S, True, hf, gate)
    return _outproj_ln(a, w_out.astype(BF16), x, ln_g, ln_b, w_router)


MLA_HW = MLA_NOPE + LANES


def _mla_proj_kernel(x_ref, pos_ref, win_ref, qg_ref, kg_ref, wqn_ref, wqr_ref, wqs_ref, wkv_ref,
                     fr_ref, sg_ref, q_ref, k_ref, v_ref):
    H = MLA_HEADS
    scale = (MLA_NOPE + MLA_ROPE) ** -0.5 * math.log2(math.e)
    xb = x_ref[...].astype(BF16)
    z = _dot(xb, win_ref[...])
    cq = z[:, :MLA_Q_RANK]
    ckv = z[:, MLA_Q_RANK:MLA_Q_RANK + MLA_KV_RANK]
    kr = z[:, MLA_Q_RANK + MLA_KV_RANK:MLA_Q_RANK + MLA_KV_RANK + LANES]
    krs = z[:, MLA_Q_RANK + MLA_KV_RANK + LANES:]
    qn = (cq * lax.rsqrt(jnp.mean(cq * cq, axis=-1, keepdims=True) + LN_EPS) * qg_ref[...]).astype(BF16)
    kvn = (ckv * lax.rsqrt(jnp.mean(ckv * ckv, axis=-1, keepdims=True) + LN_EPS) * kg_ref[...]).astype(BF16)
    ang = pos_ref[...].astype(F32) * fr_ref[...]
    cosv = jnp.cos(ang)
    lane = lax.broadcasted_iota(jnp.int32, ang.shape, 1)
    cosv = jnp.where(lane < MLA_ROPE, cosv, 0.0)
    sinv = jnp.sin(ang) * sg_ref[...]
    k_rope = kr * cosv + krs * sinv
    kv = _dot(kvn, wkv_ref[...])
    q_nope = _dot(qn, wqn_ref[...])
    q_rope = _dot(qn, wqr_ref[...])
    q_swap = _dot(qn, wqs_ref[...])
    ones_col = jnp.where(lane == 0, 1.0, 0.0).astype(BF16)
    for h in range(H):
        a0 = h * MLA_HW
        q_ref[:, a0:a0 + MLA_NOPE] = (q_nope[:, h * MLA_NOPE:(h + 1) * MLA_NOPE] * scale).astype(BF16)
        qr = q_rope[:, h * LANES:(h + 1) * LANES] * cosv + q_swap[:, h * LANES:(h + 1) * LANES] * sinv
        q_ref[:, a0 + MLA_NOPE:a0 + MLA_HW] = (qr * scale).astype(BF16)
        k_ref[:, a0:a0 + MLA_NOPE] = kv[:, h * 2 * MLA_NOPE:h * 2 * MLA_NOPE + MLA_NOPE].astype(BF16)
        k_ref[:, a0 + MLA_NOPE:a0 + MLA_HW] = k_rope.astype(BF16)
        v_ref[:, 2 * h * MLA_DV:(2 * h + 1) * MLA_DV] = kv[:, h * 2 * MLA_NOPE + MLA_NOPE:(h + 1) * 2 * MLA_NOPE].astype(BF16)
        v_ref[:, (2 * h + 1) * MLA_DV:(2 * h + 2) * MLA_DV] = ones_col


def _attn_kernel(q_ref, k_ref, v_ref, o_ref, *, n_sub):
    k = k_ref[...]
    v = v_ref[...]
    rows = q_ref.shape[0] // n_sub
    for i in range(n_sub):
        s = _dot_nt(q_ref[i * rows:(i + 1) * rows, :], k)
        p = jnp.exp2(s - jnp.max(s, axis=-1, keepdims=True))
        acc = _dot(p.astype(BF16), v)
        o_ref[i * rows:(i + 1) * rows, :] = (acc[:, :MLA_DV] / acc[:, MLA_DV:MLA_DV + 1]).astype(o_ref.dtype)


def _pad_rope_cols(w, swap):
    half = MLA_ROPE // 2
    if swap:
        w = jnp.concatenate([w[..., half:], w[..., :half]], axis=-1)
    w = jnp.concatenate([w, jnp.zeros_like(w)], axis=-1)
    return w.reshape(w.shape[0], -1)


def _mla_layer(x, xb, B, S, positions, w_in, q_norm_g, kv_norm_g, w_uq, w_ukv, w_out, ln_g, ln_b, w_router):
    del xb
    T = B * S
    H = MLA_HEADS
    half = MLA_ROPE // 2
    w_kr = w_in[:, MLA_Q_RANK + MLA_KV_RANK:].reshape(D_MODEL, 1, MLA_ROPE)
    win = jnp.concatenate([w_in[:, :MLA_Q_RANK + MLA_KV_RANK], _pad_rope_cols(w_kr, False),
                           _pad_rope_cols(w_kr, True)], axis=1).astype(BF16)
    wq = w_uq.reshape(MLA_Q_RANK, H, MLA_NOPE + MLA_ROPE)
    wqn = wq[:, :, :MLA_NOPE].reshape(MLA_Q_RANK, H * MLA_NOPE).astype(BF16)
    wqr = _pad_rope_cols(wq[:, :, MLA_NOPE:], False).astype(BF16)
    wqs = _pad_rope_cols(wq[:, :, MLA_NOPE:], True).astype(BF16)
    freq = ROPE_THETA ** (-jnp.arange(half, dtype=F32) / half)
    zeros = jnp.zeros((LANES - MLA_ROPE,), F32)
    fr = jnp.concatenate([freq, freq, zeros]).reshape(1, LANES)
    sg = jnp.concatenate([-jnp.ones((half,), F32), jnp.ones((half,), F32), zeros]).reshape(1, LANES)
    tm = min(ROW_TILE, T)
    row = lambda i: (i, 0)
    fix = lambda i: (0, 0)
    ins = [x, positions.reshape(T, 1), win, q_norm_g.astype(F32).reshape(1, -1),
           kv_norm_g.astype(F32).reshape(1, -1), wqn, wqr, wqs, w_ukv.astype(BF16), fr, sg]
    in_specs = [pl.BlockSpec((tm, D_MODEL), row), pl.BlockSpec((tm, 1), row)]
    in_specs += [pl.BlockSpec(a.shape, fix) for a in ins[2:]]
    q, k, v = pl.pallas_call(
        _mla_proj_kernel, grid=(T // tm,), in_specs=in_specs,
        out_specs=[pl.BlockSpec((tm, H * MLA_HW), row), pl.BlockSpec((tm, H * MLA_HW), row),
                   pl.BlockSpec((tm, 2 * H * MLA_DV), row)],
        out_shape=[jax.ShapeDtypeStruct((T, H * MLA_HW), BF16), jax.ShapeDtypeStruct((T, H * MLA_HW), BF16),
                   jax.ShapeDtypeStruct((T, 2 * H * MLA_DV), BF16)],
        compiler_params=_params("parallel"), name="mla_proj")(*ins)
    tq = min(Q_TILE, S)
    nq = S // tq
    att = pl.pallas_call(
        functools.partial(_attn_kernel, n_sub=tq // min(Q_SUB, tq)), grid=(B, H, nq),
        in_specs=[pl.BlockSpec((tq, MLA_HW), lambda b, h, i: (b * nq + i, h)),
                  pl.BlockSpec((S, MLA_HW), lambda b, h, i: (b, h)),
                  pl.BlockSpec((S, 2 * MLA_DV), lambda b, h, i: (b, h))],
        out_specs=pl.BlockSpec((tq, MLA_DV), lambda b, h, i: (b * nq + i, h)),
        out_shape=jax.ShapeDtypeStruct((T, H * MLA_DV), BF16),
        compiler_params=_params("parallel", "parallel", "arbitrary"), name="mla_attn")(q, k, v)
    return _outproj_ln(att, w_out.astype(BF16), x, ln_g, ln_b, w_router)


def _router_kernel(lg_ref, pos_ref, gate_ref, idx_ref, *, cap):
    logits = lg_ref[...]
    E, S = logits.shape
    mx = jnp.max(logits, axis=0, keepdims=True)
    ex = jnp.exp(logits - mx)
    aff = ex / jnp.sum(ex, axis=0, keepdims=True)
    bits = pltpu.bitcast(aff, jnp.int32)

    def count_ge(cand):
        return jnp.sum(jnp.where(bits >= cand, 1.0, 0.0), axis=1, keepdims=True)

    def bit_triple(i, thr):
        sh = 28 - 3 * i
        best = thr
        for k in range(1, 8):
            cand = thr | jnp.left_shift(jnp.int32(k), sh)
            best = jnp.where(count_ge(cand) >= cap, cand, best)
        return best

    thr = lax.fori_loop(0, 10, bit_triple, jnp.zeros((E, 1), jnp.int32))
    c0 = thr | 1
    thr = jnp.where(count_ge(c0) >= cap, c0, thr)
    gt = bits > thr
    eq = bits == thr
    need = cap - jnp.sum(jnp.where(gt, 1.0, 0.0), axis=1, keepdims=True)
    r = lax.broadcasted_iota(jnp.int32, (LANES, LANES), 0)
    c = lax.broadcasted_iota(jnp.int32, (LANES, LANES), 1)
    upper = jnp.where(r < c, 1.0, 0.0).astype(BF16)
    off = jnp.zeros((2 * E, 1), F32)
    for blk in range(S // LANES):
        sl = slice(blk * LANES, (blk + 1) * LANES)
        ind = jnp.concatenate([jnp.where(gt[:, sl], 1.0, 0.0), jnp.where(eq[:, sl], 1.0, 0.0)], axis=0)
        pre = _dot(ind.astype(BF16), upper) + off
        off = off + jnp.sum(ind, axis=1, keepdims=True)
        pg, pe = pre[:E], pre[E:]
        sel = gt[:, sl] | (eq[:, sl] & (pe < need))
        slot = pg + jnp.minimum(pe, need)
        pos_ref[:, sl] = jnp.where(sel, slot, -1.0).astype(jnp.int32)

    slots = lax.broadcasted_iota(jnp.int32, (cap, S), 0)
    token = lax.broadcasted_iota(jnp.int32, (1, S), 1)
    tok_hi = (token >> 6).astype(F32)
    tok_lo = (token & 63).astype(F32)
    pad = jnp.zeros((3, S), F32)
    for e in range(E):
        onehot = jnp.where(slots == pos_ref[e:e + 1, :], 1.0, 0.0).astype(BF16)
        a = aff[e:e + 1, :]
        a0 = a.astype(BF16).astype(F32)
        a1 = (a - a0).astype(BF16).astype(F32)
        a2 = a - a0 - a1
        vals = jnp.concatenate([tok_hi, tok_lo, a0, a1, a2, pad], axis=0).astype(BF16)
        res = _dot_nt(vals, onehot)
        idx_ref[e:e + 1, :] = (res[0:1, :] * 64.0 + res[1:2, :]).astype(jnp.int32)
        gate_ref[e:e + 1, :] = res[2:3, :] + res[3:4, :] + res[4:5, :]


def _gather_kernel(idx_ref, xp_ref, xin_ref, *, n_rows):
    def gather(t, carry):
        j0 = pl.multiple_of(t * SUBLANES, SUBLANES)
        rows = [xp_ref[pl.ds(idx_ref[0, 0, j0 + k], 1), :] for k in range(SUBLANES)]
        xin_ref[pl.ds(j0, SUBLANES), :] = jnp.concatenate(rows, axis=0)
        return carry

    lax.fori_loop(0, n_rows // SUBLANES, gather, 0, unroll=2)


def _ffn_kernel(xin_ref, gate_ref, wg_ref, wu_ref, wd_ref, out_ref, wgb_ref, wub_ref, wdb_ref):
    G, _, cap, half = xin_ref.shape

    @pl.when(pl.program_id(1) == 0)
    def _():
        wgb_ref[...] = wg_ref[0, 0].astype(BF16)
        wub_ref[...] = wu_ref[0, 0].astype(BF16)
        wdb_ref[...] = wd_ref[0, 0].astype(BF16)

    words = xin_ref[...].reshape(G * cap, half)
    x_lo = pltpu.bitcast(words << 16, F32).astype(BF16)
    x_hi = pltpu.bitcast(words & jnp.uint32(0xFFFF0000), F32).astype(BF16)
    out = None
    for c0 in range(0, wgb_ref.shape[1], FF_CHUNK):
        c1 = c0 + FF_CHUNK
        hg = _dot(x_lo, wgb_ref[:half, c0:c1]) + _dot(x_hi, wgb_ref[half:, c0:c1])
        hu = _dot(x_lo, wub_ref[:half, c0:c1]) + _dot(x_hi, wub_ref[half:, c0:c1])
        hmid = (hg * _sigmoid(hg) * hu).astype(BF16)
        part = _dot(hmid, wdb_ref[c0:c1, :])
        out = part if out is None else out + part
    eye = (lax.broadcasted_iota(jnp.int32, (cap, cap), 0) == lax.broadcasted_iota(jnp.int32, (cap, cap), 1))
    for g in range(G):
        gate = jnp.sum(jnp.where(eye, gate_ref[g, 0], 0.0), axis=1, keepdims=True)
        out_ref[g, 0] = (out[g * cap:(g + 1) * cap, :] * gate).astype(out_ref.dtype)


def _combine_kernel(outs_ref, pos_ref, x_ref, g_ref, b_ref, o_ref, *, cap):
    E, ts = pos_ref.shape
    rows = ts // COMBINE_CHAINS
    lane = lax.broadcasted_iota(jnp.int32, (rows, cap), 1).astype(F32)
    outs = outs_ref[...]
    for c in range(COMBINE_CHAINS):
        posf = pos_ref[:, c * rows:(c + 1) * rows].astype(F32)
        pos_t = jnp.concatenate([posf, jnp.full((LANES - E, rows), -1.0, F32)], axis=0).T
        onehot = jnp.concatenate(
            [jnp.where(pos_t[:, e:e + 1] == lane, 1.0, 0.0).astype(BF16) for e in range(E)], axis=1)
        y = _dot(onehot, outs)
        o_ref[c * rows:(c + 1) * rows, :] = _layer_norm(ALPHA * x_ref[c * rows:(c + 1) * rows, :] + y,
                                                        g_ref[...], b_ref[...])


def _moe_layer(x, xb, logits, B, S, layer, w_gate, w_up, w_down, ln_g, ln_b):
    T = B * S
    E = N_EXPERTS
    D = D_MODEL
    cap = CAPACITY_FACTOR * S // E
    pos, gate, idx = pl.pallas_call(
        functools.partial(_router_kernel, cap=cap), grid=(B,),
        in_specs=[pl.BlockSpec((E, S), lambda b: (0, b))],
        out_specs=[pl.BlockSpec((E, S), lambda b: (b, 0)), pl.BlockSpec((E, cap), lambda b: (b, 0)),
                   pl.BlockSpec((E, cap), lambda b: (b, 0))],
        out_shape=[jax.ShapeDtypeStruct((B * E, S), jnp.int32), jax.ShapeDtypeStruct((B * E, cap), F32),
                   jax.ShapeDtypeStruct((B * E, cap), jnp.int32)],
        compiler_params=_params("parallel"), name="moe_router")(logits)
    xin = pl.pallas_call(
        functools.partial(_gather_kernel, n_rows=E * cap), grid=(B,),
        in_specs=[pl.BlockSpec((1, 1, E * cap), lambda b: (b, 0, 0), memory_space=pltpu.SMEM),
                  pl.BlockSpec((S, D // 2), lambda b: (b, 0))],
        out_specs=pl.BlockSpec((E * cap, D // 2), lambda b: (b, 0)),
        out_shape=jax.ShapeDtypeStruct((B * E * cap, D // 2), jnp.uint32),
        compiler_params=_params("parallel"), name="moe_gather")(idx.reshape(B, 1, E * cap), xb)
    ff = w_gate.shape[-1]
    G = math.gcd(B, FFN_GROUP)
    outs = pl.pallas_call(
        _ffn_kernel, grid=(E, B // G),
        in_specs=[pl.BlockSpec((G, 1, cap, D // 2), lambda e, b: (b, e, 0, 0)),
                  pl.BlockSpec((G, 1, 1, cap), lambda e, b: (b, e, 0, 0)),
                  pl.BlockSpec((1, 1, D, ff), lambda e, b: (layer, e, 0, 0)),
                  pl.BlockSpec((1, 1, D, ff), lambda e, b: (layer, e, 0, 0)),
                  pl.BlockSpec((1, 1, ff, D), lambda e, b: (layer, e, 0, 0))],
        out_specs=pl.BlockSpec((G, 1, cap, D), lambda e, b: (b, e, 0, 0)),
        out_shape=jax.ShapeDtypeStruct((B, E, cap, D), BF16),
        scratch_shapes=[pltpu.VMEM((D, ff), BF16), pltpu.VMEM((D, ff), BF16), pltpu.VMEM((ff, D), BF16)],
        compiler_params=_params("arbitrary", "arbitrary"), name="moe_ffn")(
            xin.reshape(B, E, cap, D // 2), gate.reshape(B, E, 1, cap), w_gate, w_up, w_down)
    ts = min(COMBINE_TILE, S)
    nb = S // ts
    return pl.pallas_call(
        functools.partial(_combine_kernel, cap=cap), grid=(B, nb),
        in_specs=[pl.BlockSpec((E * cap, D), lambda b, j: (b, 0)),
                  pl.BlockSpec((E, ts), lambda b, j: (b, j)),
                  pl.BlockSpec((ts, D), lambda b, j: (b * nb + j, 0)),
                  pl.BlockSpec((1, D), lambda b, j: (0, 0)), pl.BlockSpec((1, D), lambda b, j: (0, 0))],
        out_specs=pl.BlockSpec((ts, D), lambda b, j: (b * nb + j, 0)),
        out_shape=jax.ShapeDtypeStruct((T, D), F32),
        compiler_params=_params("parallel", "arbitrary"), name="moe_combine")(
            outs.reshape(B * E * cap, D), pos, x, ln_g.reshape(1, D), ln_b.reshape(1, D))


def kernel(x, positions, mlstm_w_in, mlstm_gate_b, mlstm_norm_g, mlstm_w_out, gla_w_in, gla_gate_w, gla_gate_b, gla_norm_g, gla_w_out, lru_w_in, lru_conv_w, lru_conv_b, lru_gate_a_w, lru_gate_a_b, lru_gate_x_w, lru_gate_x_b, lru_lambda, lru_w_out, mla_w_in, mla_q_norm_g, mla_kv_norm_g, mla_w_uq, mla_w_ukv, mla_w_out, moe_router, moe_w_gate, moe_w_up, moe_w_down, ln_g, ln_b):
    B, S, D = x.shape
    xf = x.reshape(B * S, D)
    xb = None
    for i in range(DEPTH):
        m = i % N_MIXERS
        j = i // N_MIXERS
        g0, b0, wr = ln_g[i, 0], ln_b[i, 0], moe_router[i]
        if m == 0:
            xf, xb, lg = _mlstm_layer(xf, xb, B, S, mlstm_w_in[j], mlstm_gate_b[j], mlstm_norm_g[j],
                                      mlstm_w_out[j], g0, b0, wr)
        elif m == 1:
            xf, xb, lg = _gla_layer(xf, xb, B, S, gla_w_in[j], gla_gate_w[j], gla_gate_b[j], gla_norm_g[j],
                                    gla_w_out[j], g0, b0, wr)
        elif m == 2:
            xf, xb, lg = _lru_layer(xf, xb, B, S, lru_w_in[j], lru_conv_w[j], lru_conv_b[j], lru_gate_a_w[j],
                                    lru_gate_a_b[j], lru_gate_x_w[j], lru_gate_x_b[j], lru_lambda[j],
                                    lru_w_out[j], g0, b0, wr)
        else:
            xf, xb, lg = _mla_layer(xf, xb, B, S, positions, mla_w_in[j], mla_q_norm_g[j], mla_kv_norm_g[j],
                                    mla_w_uq[j], mla_w_ukv[j], mla_w_out[j], g0, b0, wr)
        xf = _moe_layer(xf, xb, lg, B, S, i, moe_w_gate, moe_w_up, moe_w_down, ln_g[i, 1], ln_b[i, 1])
    return xf.reshape(B, S, D)
```

```python
import functools
import math

import jax
import jax.numpy as jnp
from jax import lax
from jax.experimental import pallas as pl
from jax.experimental.pallas import tpu as pltpu

F32 = jnp.float32
BF16 = jnp.bfloat16

D_MODEL = 1024
DEPTH = 4
N_MIXERS = 4
ALPHA = (2 * DEPTH) ** 0.25
LN_EPS = 1e-5

ML_HEADS = 4
ML_DV = D_MODEL // ML_HEADS
ML_DK = ML_DV // 2

GLA_HEADS = 4
GLA_DK = D_MODEL // 2 // GLA_HEADS
GLA_DV = D_MODEL // GLA_HEADS
GLA_RANK = 16
GLA_TAU = 16.0

LRU_WIDTH = D_MODEL
LRU_BLOCKS = 4
LRU_BW = LRU_WIDTH // LRU_BLOCKS
CONV_WIDTH = 4
LRU_C = 8.0

MLA_HEADS = 8
MLA_NOPE = 128
MLA_ROPE = 64
MLA_DV = 128
MLA_Q_RANK = 384
MLA_KV_RANK = 256
ROPE_THETA = 10000.0

N_EXPERTS = 16
CAPACITY_FACTOR = 2

V7X_VMEM_BYTES = 64 * 1024 * 1024
VMEM_LIMIT = V7X_VMEM_BYTES - 8 * 1024 * 1024
LANES = 128
SUBLANES = 8

ROW_TILE = 1024
SEQ_TILE = 1024
ML_CHUNK = 512
GLA_CHUNK = 128
Q_TILE = 2048
Q_SUB = 256
N_CHUNK = 512
FFN_GROUP = 4
COMBINE_TILE = 1024
COMBINE_CHAINS = 4
FF_CHUNK = 512


def _params(*sem):
    return pltpu.CompilerParams(dimension_semantics=sem, vmem_limit_bytes=VMEM_LIMIT)


def _log_sigmoid(x):
    return jnp.minimum(x, 0.0) - jnp.log(1.0 + jnp.exp(-jnp.abs(x)))


def _sigmoid(x):
    return 0.5 * jnp.tanh(0.5 * x) + 0.5


def _layer_norm(v, g, b):
    mu = jnp.mean(v, axis=-1, keepdims=True)
    d = v - mu
    var = jnp.mean(d * d, axis=-1, keepdims=True)
    return d * lax.rsqrt(var + LN_EPS) * g + b


def _dot(a, b):
    return jnp.dot(a, b, preferred_element_type=F32)


def _dot_nt(a, b):
    return lax.dot_general(a, b, (((1,), (1,)), ((), ())), preferred_element_type=F32)


def _split3(a):
    a0 = a.astype(BF16)
    r1 = a - a0.astype(F32)
    a1 = r1.astype(BF16)
    a2 = (r1 - a1.astype(F32)).astype(BF16)
    return a0, a1, a2


def _dot_tn(a, b):
    return lax.dot_general(a, b, (((0,), (0,)), ((), ())), preferred_element_type=F32)


def _dense_kernel(*refs, n_w, n_t, has_bias):
    x_ref = refs[0]
    pos = 1
    w_refs = refs[pos:pos + n_w]
    pos += n_w
    b_refs = []
    for hb in has_bias:
        if hb:
            b_refs.append(refs[pos])
            pos += 1
        else:
            b_refs.append(None)
    t_refs = refs[pos:pos + 2 * n_t]
    pos += 2 * n_t
    o_refs = refs[pos:pos + n_w]
    pos += n_w
    ot_refs = refs[pos:pos + n_t]

    xb = x_ref[...].astype(BF16)
    for w_ref, b_ref, o_ref in zip(w_refs, b_refs, o_refs):
        n = w_ref.shape[1]
        for j0 in range(0, n, N_CHUNK):
            j1 = min(n, j0 + N_CHUNK)
            acc = _dot(xb, w_ref[:, j0:j1])
            if b_ref is not None:
                acc = acc + b_ref[:, j0:j1]
            o_ref[:, j0:j1] = acc.astype(o_ref.dtype)
    for i in range(n_t):
        wt_ref, bt_ref = t_refs[2 * i], t_refs[2 * i + 1]
        ot_refs[i][...] = (_dot_nt(wt_ref[...], xb) + bt_ref[...]).astype(ot_refs[i].dtype)


def _dense(x, ws, dtypes, biases=None, transposed=()):
    T, K = x.shape
    tm = min(ROW_TILE, T)
    if biases is None:
        biases = [None] * len(ws)
    has_bias = tuple(b is not None for b in biases)
    args = [x] + list(ws) + [b for b in biases if b is not None]
    in_specs = [pl.BlockSpec((tm, K), lambda i: (i, 0))]
    in_specs += [pl.BlockSpec(w.shape, lambda i: (0, 0)) for w in ws]
    in_specs += [pl.BlockSpec(b.shape, lambda i: (0, 0)) for b in biases if b is not None]
    for wt, bt, _ in transposed:
        args += [wt, bt]
        in_specs += [pl.BlockSpec(wt.shape, lambda i: (0, 0)), pl.BlockSpec(bt.shape, lambda i: (0, 0))]
    out_shape = [jax.ShapeDtypeStruct((T, w.shape[1]), dt) for w, dt in zip(ws, dtypes)]
    out_specs = [pl.BlockSpec((tm, w.shape[1]), lambda i: (i, 0)) for w in ws]
    for wt, _, dt in transposed:
        out_shape.append(jax.ShapeDtypeStruct((wt.shape[0], T), dt))
        out_specs.append(pl.BlockSpec((wt.shape[0], tm), lambda i: (0, i)))
    kern = functools.partial(_dense_kernel, n_w=len(ws), n_t=len(transposed), has_bias=has_bias)
    return pl.pallas_call(
        kern, grid=(T // tm,), in_specs=in_specs, out_specs=out_specs, out_shape=out_shape,
        compiler_params=_params("parallel"), name="dense")(*args)


def _outproj_ln_kernel(a_ref, w_ref, x_ref, g_ref, b_ref, wr_ref, o_ref, ob_ref, lg_ref):
    y = _dot(a_ref[...], w_ref[...])
    v = _layer_norm(ALPHA * x_ref[...] + y, g_ref[...], b_ref[...])
    o_ref[...] = v
    E = lg_ref.shape[0]
    vh = v.astype(BF16)
    vl = (v - vh.astype(F32)).astype(BF16)
    both = _dot_nt(wr_ref[...], vh)
    lg_ref[...] = both[:E] + (both[E:] + _dot_nt(wr_ref[:E, :], vl))
    half = v.shape[1] // 2
    bits = pltpu.bitcast(v.astype(BF16).astype(F32), jnp.uint32)
    ob_ref[...] = (bits[:, :half] >> 16) | bits[:, half:]


def _outproj_ln(a, w, x, g, b, w_router):
    T, K = a.shape
    E = w_router.shape[1]
    wr = w_router.astype(F32).T
    wh = wr.astype(BF16)
    wr2 = jnp.concatenate([wh, (wr - wh.astype(F32)).astype(BF16)], axis=0)
    D = w.shape[1]
    tm = min(ROW_TILE, T)
    row = lambda i: (i, 0)
    fix = lambda i: (0, 0)
    return pl.pallas_call(
        _outproj_ln_kernel, grid=(T // tm,),
        in_specs=[pl.BlockSpec((tm, K), row), pl.BlockSpec((K, D), fix), pl.BlockSpec((tm, D), row),
                  pl.BlockSpec((1, D), fix), pl.BlockSpec((1, D), fix), pl.BlockSpec((2 * E, D), fix)],
        out_specs=[pl.BlockSpec((tm, D), row), pl.BlockSpec((tm, D // 2), row),
                   pl.BlockSpec((E, tm), lambda i: (0, i))],
        out_shape=[jax.ShapeDtypeStruct((T, D), F32), jax.ShapeDtypeStruct((T, D // 2), jnp.uint32),
                   jax.ShapeDtypeStruct((E, T), F32)],
        compiler_params=_params("parallel"), name="outproj_ln")(a, w, x, g.reshape(1, D), b.reshape(1, D), wr2)


def _mlstm_kernel(*refs, reverse, ts, chunk):
    if reverse:
        qv_ref, kt_ref, gr_ref, hf_ref, o_ref, ng_ref, out_ref, c_ref, m_ref = refs
    else:
        qv_ref, kt_ref, gr_ref, out_ref, c_ref, m_ref = refs
    H, DK, DV, L = ML_HEADS, ML_DK, ML_DV, chunk
    scale = DK ** -0.5

    @pl.when(pl.program_id(1) == 0)
    def _():
        c_ref[...] = jnp.zeros_like(c_ref)
        m_ref[...] = jnp.zeros_like(m_ref)

    rows = lax.broadcasted_iota(jnp.int32, (L, L), 0)
    cols = lax.broadcasted_iota(jnp.int32, (L, L), 1)
    mask = (cols >= rows) if reverse else (cols <= rows)
    eye = rows == cols
    tri = jnp.where((rows >= cols) if reverse else (rows <= cols), 1.0, 0.0).astype(BF16)
    ones_col = jnp.where(lax.broadcasted_iota(jnp.int32, (L, LANES), 1) == 0, 1.0, 0.0).astype(BF16)
    d0 = 8 if reverse else 0
    last = 0 if reverse else L - 1
    n_chunks = ts // L
    order = range(n_chunks - 1, -1, -1) if reverse else range(n_chunks)
    neg_inf = -jnp.inf

    for c in order:
        r0 = c * L
        g8 = gr_ref[d0:d0 + 8, r0:r0 + L]
        lf8 = _log_sigmoid(g8)
        pieces = _dot(jnp.concatenate(_split3(lf8), axis=0), tri)
        b8 = pieces[0:8] + pieces[8:16] + pieces[16:24]
        u8 = g8[0:4, :] - b8[4:8, :]
        for h in range(H):
            qb = qv_ref[r0:r0 + L, h * DK:(h + 1) * DK]
            kt = kt_ref[h * DK:(h + 1) * DK, r0:r0 + L]
            vx = jnp.concatenate([qv_ref[r0:r0 + L, H * DK + h * DV:H * DK + (h + 1) * DV], ones_col], axis=1)
            u_r = u8[h:h + 1, :]
            b_r = b8[4 + h:5 + h, :]
            m_prev = m_ref[h:h + 1, 0:1]
            b_c = jnp.sum(jnp.where(eye, b_r, 0.0), axis=1, keepdims=True)
            um = jnp.where(mask, u_r, neg_inf)
            a_c = jnp.maximum(m_prev, jnp.max(um, axis=1, keepdims=True))
            s = _dot(qb, kt) * (scale * jnp.exp(um - a_c))
            w_int = jnp.exp(m_prev - a_c) * scale
            c_old = c_ref[h]
            acc = _dot(s.astype(BF16), vx) + w_int * _dot(qb, c_old.astype(BF16))
            den = acc[:, DV:DV + 1]
            hh = acc[:, :DV] / jnp.maximum(jnp.abs(den), jnp.exp(-(a_c + b_c)))
            a_last = jnp.maximum(m_prev, jnp.max(u_r, axis=1, keepdims=True))
            g_tot = b_r[:, last:last + 1]
            wc = jnp.exp(m_prev - a_last)
            kw = (kt.astype(F32) * jnp.exp(u_r - a_last)).astype(BF16)
            c_ref[h] = wc * c_old + _dot(kw, vx)
            m_ref[h:h + 1, :] = jnp.broadcast_to(g_tot + a_last, (1, LANES))
            if reverse:
                hs = hf_ref[r0:r0 + L, h * DV:(h + 1) * DV] + hh
                mu = jnp.mean(hs, axis=-1, keepdims=True)
                dd = hs - mu
                var = jnp.mean(dd * dd, axis=-1, keepdims=True)
                hn = dd * lax.rsqrt(var + LN_EPS) * ng_ref[:, h * DV:(h + 1) * DV]
                og = _sigmoid(o_ref[r0:r0 + L, h * DV:(h + 1) * DV])
                out_ref[r0:r0 + L, h * DV:(h + 1) * DV] = (og * hn).astype(out_ref.dtype)
            else:
                out_ref[r0:r0 + L, h * DV:(h + 1) * DV] = hh


def _mlstm_pass(qv, kt, gr, B, S, reverse, hf=None, o=None, ng=None):
    T = B * S
    ts = min(SEQ_TILE, S)
    chunk = min(ML_CHUNK, ts)
    nb = S // ts
    if reverse:
        blk = lambda b, j: (b * nb + nb - 1 - j, 0)
        blk_t = lambda b, j: (0, b * nb + nb - 1 - j)
    else:
        blk = lambda b, j: (b * nb + j, 0)
        blk_t = lambda b, j: (0, b * nb + j)
    in_specs = [pl.BlockSpec((ts, qv.shape[1]), blk), pl.BlockSpec((kt.shape[0], ts), blk_t),
                pl.BlockSpec((16, ts), blk_t)]
    args = [qv, kt, gr]
    if reverse:
        in_specs += [pl.BlockSpec((ts, D_MODEL), blk), pl.BlockSpec((ts, D_MODEL), blk),
                     pl.BlockSpec((1, D_MODEL), lambda b, j: (0, 0))]
        args += [hf, o, ng.reshape(1, D_MODEL)]
    out_dtype = BF16 if reverse else F32
    kern = functools.partial(_mlstm_kernel, reverse=reverse, ts=ts, chunk=chunk)
    return pl.pallas_call(
        kern, grid=(B, nb), in_specs=in_specs,
        out_specs=pl.BlockSpec((ts, D_MODEL), blk),
        out_shape=jax.ShapeDtypeStruct((T, D_MODEL), out_dtype),
        scratch_shapes=[pltpu.VMEM((ML_HEADS, ML_DK, ML_DV + LANES), F32),
                        pltpu.VMEM((SUBLANES, LANES), F32)],
        compiler_params=_params("parallel", "arbitrary"),
        name="mlstm_bwd" if reverse else "mlstm_fwd")(*args)


def _mlstm_layer(x, B, S, w_in, gate_b, norm_g, w_out, ln_g, ln_b, w_router):
    qk = ML_HEADS * ML_DK
    w = w_in.astype(BF16)
    w_qv = jnp.concatenate([w[:, :qk], w[:, 2 * qk:2 * qk + D_MODEL]], axis=1)
    w_k_t = w[:, qk:2 * qk].T
    w_o = w[:, 2 * qk + D_MODEL:2 * qk + 2 * D_MODEL]
    w_g_t = w[:, 2 * qk + 2 * D_MODEL:].T
    b_g = gate_b.astype(F32).reshape(16, 1)
    qv, o, kt, gr = _dense(x, [w_qv, w_o], [BF16, F32],
                           transposed=[(w_k_t, jnp.zeros((qk, 1), F32), BF16), (w_g_t, b_g, F32)])
    hf = _mlstm_pass(qv, kt, gr, B, S, False)
    a = _mlstm_pass(qv, kt, gr, B, S, True, hf, o, norm_g.astype(F32))
    return _outproj_ln(a, w_out.astype(BF16), x, ln_g, ln_b, w_router)


def _cumsum_rows(x, n, reverse):
    row = lax.broadcasted_iota(jnp.int32, x.shape, 0)
    sh = 1
    while sh < n:
        if reverse:
            x = x + jnp.where(row < n - sh, pltpu.roll(x, n - sh, axis=0), 0.0)
        else:
            x = x + jnp.where(row >= sh, pltpu.roll(x, sh, axis=0), 0.0)
        sh *= 2
    return x


def _gla_kernel(*refs, reverse, ts, chunk):
    if reverse:
        qkv_ref, glr_ref, gw_ref, gb_ref, of_ref, r_ref, ng_ref, out_ref, st_ref, la_ref = refs
    else:
        qkv_ref, glr_ref, gw_ref, gb_ref, out_ref, st_ref, la_ref = refs
    H, DK, DV, L = GLA_HEADS, GLA_DK, GLA_DV, chunk
    scale = DK ** -0.5
    d = 1 if reverse else 0

    @pl.when(pl.program_id(1) == 0)
    def _():
        st_ref[...] = jnp.zeros_like(st_ref)

    glr = glr_ref[:, d * GLA_RANK:(d + 1) * GLA_RANK]
    g0, g1, _ = _split3(glr)
    pre = _dot(jnp.concatenate([g0, g1, g0], axis=1), gw_ref[d]) + gb_ref[d:d + 1, :]
    la_ref[...] = _log_sigmoid(pre) * (1.0 / GLA_TAU)

    rows = lax.broadcasted_iota(jnp.int32, (L, L), 0)
    cols = lax.broadcasted_iota(jnp.int32, (L, L), 1)
    mask = (cols >= rows) if reverse else (cols <= rows)
    last = 0 if reverse else L - 1
    mid = L // 2
    n_chunks = ts // L

    def body(i, carry):
        c = (n_chunks - 1 - i) if reverse else i
        r0 = pl.multiple_of(c * L, L)
        bsum = _cumsum_rows(la_ref[pl.ds(r0, L), :], L, reverse)
        for h in range(H):
            b = bsum[:, h * DK:(h + 1) * DK]
            qf = qkv_ref[pl.ds(r0, L), h * DK:(h + 1) * DK].astype(F32)
            kf = qkv_ref[pl.ds(r0, L), H * DK + h * DK:H * DK + (h + 1) * DK].astype(F32)
            vb = qkv_ref[pl.ds(r0, L), 2 * H * DK + h * DV:2 * H * DK + (h + 1) * DV]
            beta = b[mid:mid + 1, :]
            g = b[last:last + 1, :]
            qt = (qf * jnp.exp(b - beta)).astype(BF16)
            kt = (kf * jnp.exp(beta - b)).astype(BF16)
            amat = jnp.where(mask, _dot_nt(qt, kt) * scale, 0.0)
            qh = (qf * (jnp.exp(b) * scale)).astype(BF16)
            st = st_ref[h]
            o = _dot(amat.astype(BF16), vb) + _dot_nt(qh, st.astype(BF16))
            kh = (kf * jnp.exp(g - b)).astype(BF16)
            st_ref[h] = st * jnp.exp(g) + _dot_tn(vb, kh)
            if reverse:
                hs = of_ref[pl.ds(r0, L), h * DV:(h + 1) * DV] + o
                mu = jnp.mean(hs, axis=-1, keepdims=True)
                dd = hs - mu
                var = jnp.mean(dd * dd, axis=-1, keepdims=True)
                hn = dd * lax.rsqrt(var + LN_EPS) * ng_ref[:, h * DV:(h + 1) * DV]
                rr = r_ref[pl.ds(r0, L), h * DV:(h + 1) * DV]
                out_ref[pl.ds(r0, L), h * DV:(h + 1) * DV] = (rr * _sigmoid(rr) * hn).astype(out_ref.dtype)
            else:
                out_ref[pl.ds(r0, L), h * DV:(h + 1) * DV] = o
        return carry

    lax.fori_loop(0, n_chunks, body, 0, unroll=4)


def _gla_pass(qkv, glr, gw, gb, B, S, reverse, of=None, r=None, ng=None):
    T = B * S
    ts = min(SEQ_TILE, S)
    chunk = min(GLA_CHUNK, ts)
    nb = S // ts
    if reverse:
        blk = lambda b, j: (b * nb + nb - 1 - j, 0)
    else:
        blk = lambda b, j: (b * nb + j, 0)
    fix2 = lambda b, j: (0, 0)
    fix3 = lambda b, j: (0, 0, 0)
    in_specs = [pl.BlockSpec((ts, qkv.shape[1]), blk), pl.BlockSpec((ts, 2 * GLA_RANK), blk),
                pl.BlockSpec(gw.shape, fix3), pl.BlockSpec(gb.shape, fix2)]
    args = [qkv, glr, gw, gb]
    if reverse:
        in_specs += [pl.BlockSpec((ts, D_MODEL), blk), pl.BlockSpec((ts, D_MODEL), blk),
                     pl.BlockSpec((1, D_MODEL), fix2)]
        args += [of, r, ng.reshape(1, D_MODEL)]
    kern = functools.partial(_gla_kernel, reverse=reverse, ts=ts, chunk=chunk)
    return pl.pallas_call(
        kern, grid=(B, nb), in_specs=in_specs,
        out_specs=pl.BlockSpec((ts, D_MODEL), blk),
        out_shape=jax.ShapeDtypeStruct((T, D_MODEL), BF16 if reverse else F32),
        scratch_shapes=[pltpu.VMEM((GLA_HEADS, GLA_DV, GLA_DK), F32),
                        pltpu.VMEM((ts, GLA_HEADS * GLA_DK), F32)],
        compiler_params=_params("parallel", "arbitrary"),
        name="gla_bwd" if reverse else "gla_fwd")(*args)


def _gla_layer(x, B, S, w_in, gate_w, gate_b, norm_g, w_out, ln_g, ln_b, w_router):
    qk = GLA_HEADS * GLA_DK
    w = w_in.astype(BF16)
    w_qkv = w[:, :2 * qk + D_MODEL]
    w_r = w[:, 2 * qk + D_MODEL:2 * qk + 2 * D_MODEL]
    w_glr = w[:, 2 * qk + 2 * D_MODEL:]
    qkv, r, glr = _dense(x, [w_qkv, w_r, w_glr], [BF16, F32, F32])
    gw0 = gate_w.astype(BF16)
    gw1 = (gate_w.astype(F32) - gw0.astype(F32)).astype(BF16)
    gw = jnp.concatenate([gw0, gw0, gw1], axis=1)
    gb = gate_b.astype(F32)
    of = _gla_pass(qkv, glr, gw, gb, B, S, False)
    a = _gla_pass(qkv, glr, gw, gb, B, S, True, of, r, norm_g.astype(F32))
    return _outproj_ln(a, w_out.astype(BF16), x, ln_g, ln_b, w_router)


def _gelu_tanh(x):
    return 0.5 * x * (1.0 + jnp.tanh(math.sqrt(2.0 / math.pi) * (x + 0.044715 * (x * x * x))))


def _lru_kernel(*refs, reverse, ts):
    if reverse:
        (u_ref, up_ref, un_ref, cw_ref, cb_ref, wg_ref, bg_ref, lam_ref, hf_ref, gate_ref,
         out_ref, a_ref, g_ref, h_ref, hs_ref) = refs
    else:
        (u_ref, up_ref, un_ref, cw_ref, cb_ref, wg_ref, bg_ref, lam_ref,
         out_ref, a_ref, g_ref, h_ref) = refs
        hs_ref = out_ref
    W = LRU_WIDTH
    j = pl.program_id(1)
    nb = pl.num_programs(1)
    jj = (nb - 1 - j) if reverse else j

    @pl.when(j == 0)
    def _():
        h_ref[...] = jnp.zeros_like(h_ref)

    z = u_ref[...]
    prev = jnp.where(jj > 0, up_ref[...], 0.0)
    nxt = jnp.where(jj < nb - 1, un_ref[...], 0.0)
    row = lax.broadcasted_iota(jnp.int32, (ts, W), 0)
    zm1 = jnp.where(row == 0, prev[7:8, :], pltpu.roll(z, 1, axis=0))
    zm2 = pltpu.roll(z, 2, axis=0)
    zm2 = jnp.where(row == 0, prev[6:7, :], jnp.where(row == 1, prev[7:8, :], zm2))
    zp1 = jnp.where(row == ts - 1, nxt[0:1, :], pltpu.roll(z, ts - 1, axis=0))
    u = cw_ref[0:1, :] * zm2 + cw_ref[1:2, :] * zm1 + cw_ref[2:3, :] * z + cw_ref[3:4, :] * zp1 + cb_ref[...]

    ls = LRU_C * _log_sigmoid(lam_ref[...])
    ub = u.astype(BF16)
    for n in range(LRU_BLOCKS):
        sl = slice(n * LRU_BW, (n + 1) * LRU_BW)
        pre = _dot(ub[:, sl], wg_ref[n]) + bg_ref[n]
        r = _sigmoid(pre[:, :LRU_BW])
        ig = _sigmoid(pre[:, LRU_BW:])
        log_a = r * ls[:, sl]
        a = jnp.exp(log_a)
        a_ref[:, sl] = a
        om = 1.0 - a * a
        g_ref[:, sl] = jnp.where(om > 0.0, om * lax.rsqrt(om), 0.0) * (ig * u[:, sl])

    n_tiles = ts // SUBLANES

    srow = lax.broadcasted_iota(jnp.int32, (SUBLANES, W), 0)
    carry_row = 0 if reverse else SUBLANES - 1

    def body(i, h):
        t = (n_tiles - 1 - i) if reverse else i
        r0 = pl.multiple_of(t * SUBLANES, SUBLANES)
        a8 = a_ref[pl.ds(r0, SUBLANES), :]
        g8 = g_ref[pl.ds(r0, SUBLANES), :]
        d = 1
        while d < SUBLANES:
            shift = SUBLANES - d if reverse else d
            keep = (srow < SUBLANES - d) if reverse else (srow >= d)
            g8 = g8 + a8 * jnp.where(keep, pltpu.roll(g8, shift, axis=0), 0.0)
            a8 = a8 * jnp.where(keep, pltpu.roll(a8, shift, axis=0), 1.0)
            d *= 2
        hs = g8 + a8 * h
        hs_ref[pl.ds(r0, SUBLANES), :] = hs
        return hs[carry_row:carry_row + 1, :]

    h_ref[...] = lax.fori_loop(0, n_tiles, body, h_ref[...], unroll=8)
    if reverse:
        out_ref[...] = (_gelu_tanh(gate_ref[...]) * (hf_ref[...] + hs_ref[...])).astype(out_ref.dtype)


def _lru_pass(u, cw, cb, wg, bg, lam, B, S, reverse, hf=None, gate=None):
    T = B * S
    W = LRU_WIDTH
    ts = min(SEQ_TILE, S)
    nb = S // ts
    tpb = ts // SUBLANES
    n8 = T // SUBLANES
    if reverse:
        seq = lambda b, j: b * nb + nb - 1 - j
    else:
        seq = lambda b, j: b * nb + j
    blk = lambda b, j: (seq(b, j), 0)
    blk_prev = lambda b, j: (jnp.maximum(seq(b, j) * tpb - 1, 0), 0)
    blk_next = lambda b, j: (jnp.minimum((seq(b, j) + 1) * tpb, n8 - 1), 0)
    fix2 = lambda b, j: (0, 0)
    fix3 = lambda b, j: (0, 0, 0)
    in_specs = [pl.BlockSpec((ts, W), blk), pl.BlockSpec((SUBLANES, W), blk_prev),
                pl.BlockSpec((SUBLANES, W), blk_next), pl.BlockSpec(cw.shape, fix2),
                pl.BlockSpec(cb.shape, fix2), pl.BlockSpec(wg.shape, fix3), pl.BlockSpec(bg.shape, fix3),
                pl.BlockSpec(lam.shape, fix2)]
    args = [u, u, u, cw, cb, wg, bg, lam]
    scratch = [pltpu.VMEM((ts, W), F32), pltpu.VMEM((ts, W), F32), pltpu.VMEM((1, W), F32)]
    if reverse:
        in_specs += [pl.BlockSpec((ts, W), blk), pl.BlockSpec((ts, W), blk)]
        args += [hf, gate]
        scratch.append(pltpu.VMEM((ts, W), F32))
    kern = functools.partial(_lru_kernel, reverse=reverse, ts=ts)
    return pl.pallas_call(
        kern, grid=(B, nb), in_specs=in_specs,
        out_specs=pl.BlockSpec((ts, W), blk),
        out_shape=jax.ShapeDtypeStruct((T, W), BF16 if reverse else F32),
        scratch_shapes=scratch,
        compiler_params=_params("parallel", "arbitrary"),
        name="lru_bwd" if reverse else "lru_fwd")(*args)


def _lru_layer(x, B, S, w_in, conv_w, conv_b, gate_a_w, gate_a_b, gate_x_w, gate_x_b, lam, w_out,
               ln_g, ln_b, w_router):
    W = LRU_WIDTH
    w = w_in.astype(BF16)
    gate, u = _dense(x, [w[:, :W], w[:, W:]], [F32, F32])
    cw = conv_w.astype(F32)
    cb = conv_b.astype(F32).reshape(1, W)
    passes = []
    for d in range(2):
        wg = jnp.concatenate([gate_a_w[d], gate_x_w[d]], axis=-1).astype(BF16)
        bg = jnp.concatenate([gate_a_b[d].reshape(LRU_BLOCKS, 1, LRU_BW),
                              gate_x_b[d].reshape(LRU_BLOCKS, 1, LRU_BW)], axis=-1).astype(F32)
        passes.append((wg, bg, lam[d].astype(F32).reshape(1, W)))
    hf = _lru_pass(u, cw, cb, *passes[0], B, S, False)
    a = _lru_pass(u, cw, cb, *passes[1], B, S, True, hf, gate)
    return _outproj_ln(a, w_out.astype(BF16), x, ln_g, ln_b, w_router)


MLA_HW = MLA_NOPE + LANES


def _mla_proj_kernel(x_ref, pos_ref, win_ref, qg_ref, kg_ref, wqn_ref, wqr_ref, wqs_ref, wkv_ref,
                     fr_ref, sg_ref, q_ref, k_ref, v_ref):
    H = MLA_HEADS
    scale = (MLA_NOPE + MLA_ROPE) ** -0.5 * math.log2(math.e)
    xb = x_ref[...].astype(BF16)
    z = _dot(xb, win_ref[...])
    cq = z[:, :MLA_Q_RANK]
    ckv = z[:, MLA_Q_RANK:MLA_Q_RANK + MLA_KV_RANK]
    kr = z[:, MLA_Q_RANK + MLA_KV_RANK:MLA_Q_RANK + MLA_KV_RANK + LANES]
    krs = z[:, MLA_Q_RANK + MLA_KV_RANK + LANES:]
    qn = (cq * lax.rsqrt(jnp.mean(cq * cq, axis=-1, keepdims=True) + LN_EPS) * qg_ref[...]).astype(BF16)
    kvn = (ckv * lax.rsqrt(jnp.mean(ckv * ckv, axis=-1, keepdims=True) + LN_EPS) * kg_ref[...]).astype(BF16)
    ang = pos_ref[...].astype(F32) * fr_ref[...]
    cosv = jnp.cos(ang)
    lane = lax.broadcasted_iota(jnp.int32, ang.shape, 1)
    cosv = jnp.where(lane < MLA_ROPE, cosv, 0.0)
    sinv = jnp.sin(ang) * sg_ref[...]
    k_rope = kr * cosv + krs * sinv
    kv = _dot(kvn, wkv_ref[...])
    q_nope = _dot(qn, wqn_ref[...])
    q_rope = _dot(qn, wqr_ref[...])
    q_swap = _dot(qn, wqs_ref[...])
    ones_col = jnp.where(lane == 0, 1.0, 0.0).astype(BF16)
    for h in range(H):
        a0 = h * MLA_HW
        q_ref[:, a0:a0 + MLA_NOPE] = (q_nope[:, h * MLA_NOPE:(h + 1) * MLA_NOPE] * scale).astype(BF16)
        qr = q_rope[:, h * LANES:(h + 1) * LANES] * cosv + q_swap[:, h * LANES:(h + 1) * LANES] * sinv
        q_ref[:, a0 + MLA_NOPE:a0 + MLA_HW] = (qr * scale).astype(BF16)
        k_ref[:, a0:a0 + MLA_NOPE] = kv[:, h * 2 * MLA_NOPE:h * 2 * MLA_NOPE + MLA_NOPE].astype(BF16)
        k_ref[:, a0 + MLA_NOPE:a0 + MLA_HW] = k_rope.astype(BF16)
        v_ref[:, 2 * h * MLA_DV:(2 * h + 1) * MLA_DV] = kv[:, h * 2 * MLA_NOPE + MLA_NOPE:(h + 1) * 2 * MLA_NOPE].astype(BF16)
        v_ref[:, (2 * h + 1) * MLA_DV:(2 * h + 2) * MLA_DV] = ones_col


def _attn_kernel(q_ref, k_ref, v_ref, o_ref, *, n_sub):
    k = k_ref[...]
    v = v_ref[...]
    rows = q_ref.shape[0] // n_sub
    for i in range(n_sub):
        s = _dot_nt(q_ref[i * rows:(i + 1) * rows, :], k)
        p = jnp.exp2(s - jnp.max(s, axis=-1, keepdims=True))
        acc = _dot(p.astype(BF16), v)
        o_ref[i * rows:(i + 1) * rows, :] = (acc[:, :MLA_DV] / acc[:, MLA_DV:MLA_DV + 1]).astype(o_ref.dtype)


def _pad_rope_cols(w, swap):
    half = MLA_ROPE // 2
    if swap:
        w = jnp.concatenate([w[..., half:], w[..., :half]], axis=-1)
    w = jnp.concatenate([w, jnp.zeros_like(w)], axis=-1)
    return w.reshape(w.shape[0], -1)


def _mla_layer(x, B, S, positions, w_in, q_norm_g, kv_norm_g, w_uq, w_ukv, w_out, ln_g, ln_b, w_router):
    T = B * S
    H = MLA_HEADS
    half = MLA_ROPE // 2
    w_kr = w_in[:, MLA_Q_RANK + MLA_KV_RANK:].reshape(D_MODEL, 1, MLA_ROPE)
    win = jnp.concatenate([w_in[:, :MLA_Q_RANK + MLA_KV_RANK], _pad_rope_cols(w_kr, False),
                           _pad_rope_cols(w_kr, True)], axis=1).astype(BF16)
    wq = w_uq.reshape(MLA_Q_RANK, H, MLA_NOPE + MLA_ROPE)
    wqn = wq[:, :, :MLA_NOPE].reshape(MLA_Q_RANK, H * MLA_NOPE).astype(BF16)
    wqr = _pad_rope_cols(wq[:, :, MLA_NOPE:], False).astype(BF16)
    wqs = _pad_rope_cols(wq[:, :, MLA_NOPE:], True).astype(BF16)
    freq = ROPE_THETA ** (-jnp.arange(half, dtype=F32) / half)
    zeros = jnp.zeros((LANES - MLA_ROPE,), F32)
    fr = jnp.concatenate([freq, freq, zeros]).reshape(1, LANES)
    sg = jnp.concatenate([-jnp.ones((half,), F32), jnp.ones((half,), F32), zeros]).reshape(1, LANES)
    tm = min(ROW_TILE, T)
    row = lambda i: (i, 0)
    fix = lambda i: (0, 0)
    ins = [x, positions.reshape(T, 1), win, q_norm_g.astype(F32).reshape(1, -1),
           kv_norm_g.astype(F32).reshape(1, -1), wqn, wqr, wqs, w_ukv.astype(BF16), fr, sg]
    in_specs = [pl.BlockSpec((tm, D_MODEL), row), pl.BlockSpec((tm, 1), row)]
    in_specs += [pl.BlockSpec(a.shape, fix) for a in ins[2:]]
    q, k, v = pl.pallas_call(
        _mla_proj_kernel, grid=(T // tm,), in_specs=in_specs,
        out_specs=[pl.BlockSpec((tm, H * MLA_HW), row), pl.BlockSpec((tm, H * MLA_HW), row),
                   pl.BlockSpec((tm, 2 * H * MLA_DV), row)],
        out_shape=[jax.ShapeDtypeStruct((T, H * MLA_HW), BF16), jax.ShapeDtypeStruct((T, H * MLA_HW), BF16),
                   jax.ShapeDtypeStruct((T, 2 * H * MLA_DV), BF16)],
        compiler_params=_params("parallel"), name="mla_proj")(*ins)
    tq = min(Q_TILE, S)
    nq = S // tq
    att = pl.pallas_call(
        functools.partial(_attn_kernel, n_sub=tq // min(Q_SUB, tq)), grid=(B, H, nq),
        in_specs=[pl.BlockSpec((tq, MLA_HW), lambda b, h, i: (b * nq + i, h)),
                  pl.BlockSpec((S, MLA_HW), lambda b, h, i: (b, h)),
                  pl.BlockSpec((S, 2 * MLA_DV), lambda b, h, i: (b, h))],
        out_specs=pl.BlockSpec((tq, MLA_DV), lambda b, h, i: (b * nq + i, h)),
        out_shape=jax.ShapeDtypeStruct((T, H * MLA_DV), BF16),
        compiler_params=_params("parallel", "parallel", "arbitrary"), name="mla_attn")(q, k, v)
    return _outproj_ln(att, w_out.astype(BF16), x, ln_g, ln_b, w_router)


def _router_kernel(lg_ref, pos_ref, gate_ref, idx_ref, *, cap):
    logits = lg_ref[...]
    E, S = logits.shape
    mx = jnp.max(logits, axis=0, keepdims=True)
    ex = jnp.exp(logits - mx)
    aff = ex / jnp.sum(ex, axis=0, keepdims=True)
    bits = pltpu.bitcast(aff, jnp.int32)

    def count_ge(cand):
        return jnp.sum(jnp.where(bits >= cand, 1.0, 0.0), axis=1, keepdims=True)

    def bit_triple(i, thr):
        sh = 28 - 3 * i
        best = thr
        for k in range(1, 8):
            cand = thr | jnp.left_shift(jnp.int32(k), sh)
            best = jnp.where(count_ge(cand) >= cap, cand, best)
        return best

    thr = lax.fori_loop(0, 10, bit_triple, jnp.zeros((E, 1), jnp.int32))
    c0 = thr | 1
    thr = jnp.where(count_ge(c0) >= cap, c0, thr)
    gt = bits > thr
    eq = bits == thr
    need = cap - jnp.sum(jnp.where(gt, 1.0, 0.0), axis=1, keepdims=True)
    r = lax.broadcasted_iota(jnp.int32, (LANES, LANES), 0)
    c = lax.broadcasted_iota(jnp.int32, (LANES, LANES), 1)
    upper = jnp.where(r < c, 1.0, 0.0).astype(BF16)
    off = jnp.zeros((2 * E, 1), F32)
    for blk in range(S // LANES):
        sl = slice(blk * LANES, (blk + 1) * LANES)
        ind = jnp.concatenate([jnp.where(gt[:, sl], 1.0, 0.0), jnp.where(eq[:, sl], 1.0, 0.0)], axis=0)
        pre = _dot(ind.astype(BF16), upper) + off
        off = off + jnp.sum(ind, axis=1, keepdims=True)
        pg, pe = pre[:E], pre[E:]
        sel = gt[:, sl] | (eq[:, sl] & (pe < need))
        slot = pg + jnp.minimum(pe, need)
        pos_ref[:, sl] = jnp.where(sel, slot, -1.0).astype(jnp.int32)

    slots = lax.broadcasted_iota(jnp.int32, (cap, S), 0)
    token = lax.broadcasted_iota(jnp.int32, (1, S), 1)
    tok_hi = (token >> 6).astype(F32)
    tok_lo = (token & 63).astype(F32)
    pad = jnp.zeros((3, S), F32)
    for e in range(E):
        onehot = jnp.where(slots == pos_ref[e:e + 1, :], 1.0, 0.0).astype(BF16)
        a = aff[e:e + 1, :]
        a0 = a.astype(BF16).astype(F32)
        a1 = (a - a0).astype(BF16).astype(F32)
        a2 = a - a0 - a1
        vals = jnp.concatenate([tok_hi, tok_lo, a0, a1, a2, pad], axis=0).astype(BF16)
        res = _dot_nt(vals, onehot)
        idx_ref[e:e + 1, :] = (res[0:1, :] * 64.0 + res[1:2, :]).astype(jnp.int32)
        gate_ref[e:e + 1, :] = res[2:3, :] + res[3:4, :] + res[4:5, :]


def _gather_kernel(idx_ref, xp_ref, xin_ref, *, n_rows):
    def gather(t, carry):
        j0 = pl.multiple_of(t * SUBLANES, SUBLANES)
        rows = [xp_ref[pl.ds(idx_ref[0, 0, j0 + k], 1), :] for k in range(SUBLANES)]
        xin_ref[pl.ds(j0, SUBLANES), :] = jnp.concatenate(rows, axis=0)
        return carry

    lax.fori_loop(0, n_rows // SUBLANES, gather, 0, unroll=8)


def _ffn_kernel(xin_ref, gate_ref, wg_ref, wu_ref, wd_ref, out_ref, wgb_ref, wub_ref, wdb_ref):
    G, _, cap, half = xin_ref.shape

    @pl.when(pl.program_id(1) == 0)
    def _():
        wgb_ref[...] = wg_ref[0, 0].astype(BF16)
        wub_ref[...] = wu_ref[0, 0].astype(BF16)
        wdb_ref[...] = wd_ref[0, 0].astype(BF16)

    words = xin_ref[...].reshape(G * cap, half)
    x_lo = pltpu.bitcast(words << 16, F32).astype(BF16)
    x_hi = pltpu.bitcast(words & jnp.uint32(0xFFFF0000), F32).astype(BF16)
    out = None
    for c0 in range(0, wgb_ref.shape[1], FF_CHUNK):
        c1 = c0 + FF_CHUNK
        hg = _dot(x_lo, wgb_ref[:half, c0:c1]) + _dot(x_hi, wgb_ref[half:, c0:c1])
        hu = _dot(x_lo, wub_ref[:half, c0:c1]) + _dot(x_hi, wub_ref[half:, c0:c1])
        hmid = (hg * _sigmoid(hg) * hu).astype(BF16)
        part = _dot(hmid, wdb_ref[c0:c1, :])
        out = part if out is None else out + part
    eye = (lax.broadcasted_iota(jnp.int32, (cap, cap), 0) == lax.broadcasted_iota(jnp.int32, (cap, cap), 1))
    for g in range(G):
        gate = jnp.sum(jnp.where(eye, gate_ref[g, 0], 0.0), axis=1, keepdims=True)
        out_ref[g, 0] = (out[g * cap:(g + 1) * cap, :] * gate).astype(out_ref.dtype)


def _combine_kernel(outs_ref, pos_ref, x_ref, g_ref, b_ref, o_ref, *, cap):
    E, ts = pos_ref.shape
    rows = ts // COMBINE_CHAINS
    lane = lax.broadcasted_iota(jnp.int32, (rows, cap), 1).astype(F32)
    outs = outs_ref[...]
    for c in range(COMBINE_CHAINS):
        posf = pos_ref[:, c * rows:(c + 1) * rows].astype(F32)
        pos_t = jnp.concatenate([posf, jnp.full((LANES - E, rows), -1.0, F32)], axis=0).T
        onehot = jnp.concatenate(
            [jnp.where(pos_t[:, e:e + 1] == lane, 1.0, 0.0).astype(BF16) for e in range(E)], axis=1)
        y = _dot(onehot, outs)
        o_ref[c * rows:(c + 1) * rows, :] = _layer_norm(ALPHA * x_ref[c * rows:(c + 1) * rows, :] + y,
                                                        g_ref[...], b_ref[...])


def _moe_layer(x, xb, logits, B, S, layer, w_gate, w_up, w_down, ln_g, ln_b):
    T = B * S
    E = N_EXPERTS
    D = D_MODEL
    cap = CAPACITY_FACTOR * S // E
    pos, gate, idx = pl.pallas_call(
        functools.partial(_router_kernel, cap=cap), grid=(B,),
        in_specs=[pl.BlockSpec((E, S), lambda b: (0, b))],
        out_specs=[pl.BlockSpec((E, S), lambda b: (b, 0)), pl.BlockSpec((E, cap), lambda b: (b, 0)),
                   pl.BlockSpec((E, cap), lambda b: (b, 0))],
        out_shape=[jax.ShapeDtypeStruct((B * E, S), jnp.int32), jax.ShapeDtypeStruct((B * E, cap), F32),
                   jax.ShapeDtypeStruct((B * E, cap), jnp.int32)],
        compiler_params=_params("parallel"), name="moe_router")(logits)
    xin = pl.pallas_call(
        functools.partial(_gather_kernel, n_rows=E * cap), grid=(B,),
        in_specs=[pl.BlockSpec((1, 1, E * cap), lambda b: (b, 0, 0), memory_space=pltpu.SMEM),
                  pl.BlockSpec((S, D // 2), lambda b: (b, 0))],
        out_specs=pl.BlockSpec((E * cap, D // 2), lambda b: (b, 0)),
        out_shape=jax.ShapeDtypeStruct((B * E * cap, D // 2), jnp.uint32),
        compiler_params=_params("parallel"), name="moe_gather")(idx.reshape(B, 1, E * cap), xb)
    ff = w_gate.shape[-1]
    G = math.gcd(B, FFN_GROUP)
    outs = pl.pallas_call(
        _ffn_kernel, grid=(E, B // G),
        in_specs=[pl.BlockSpec((G, 1, cap, D // 2), lambda e, b: (b, e, 0, 0)),
                  pl.BlockSpec((G, 1, 1, cap), lambda e, b: (b, e, 0, 0)),
                  pl.BlockSpec((1, 1, D, ff), lambda e, b: (layer, e, 0, 0)),
                  pl.BlockSpec((1, 1, D, ff), lambda e, b: (layer, e, 0, 0)),
                  pl.BlockSpec((1, 1, ff, D), lambda e, b: (layer, e, 0, 0))],
        out_specs=pl.BlockSpec((G, 1, cap, D), lambda e, b: (b, e, 0, 0)),
        out_shape=jax.ShapeDtypeStruct((B, E, cap, D), BF16),
        scratch_shapes=[pltpu.VMEM((D, ff), BF16), pltpu.VMEM((D, ff), BF16), pltpu.VMEM((ff, D), BF16)],
        compiler_params=_params("arbitrary", "arbitrary"), name="moe_ffn")(
            xin.reshape(B, E, cap, D // 2), gate.reshape(B, E, 1, cap), w_gate, w_up, w_down)
    ts = min(COMBINE_TILE, S)
    nb = S // ts
    return pl.pallas_call(
        functools.partial(_combine_kernel, cap=cap), grid=(B, nb),
        in_specs=[pl.BlockSpec((E * cap, D), lambda b, j: (b, 0)),
                  pl.BlockSpec((E, ts), lambda b, j: (b, j)),
                  pl.BlockSpec((ts, D), lambda b, j: (b * nb + j, 0)),
                  pl.BlockSpec((1, D), lambda b, j: (0, 0)), pl.BlockSpec((1, D), lambda b, j: (0, 0))],
        out_specs=pl.BlockSpec((ts, D), lambda b, j: (b * nb + j, 0)),
        out_shape=jax.ShapeDtypeStruct((T, D), F32),
        compiler_params=_params("parallel", "arbitrary"), name="moe_combine")(
            outs.reshape(B * E * cap, D), pos, x, ln_g.reshape(1, D), ln_b.reshape(1, D))


def kernel(x, positions, mlstm_w_in, mlstm_gate_b, mlstm_norm_g, mlstm_w_out, gla_w_in, gla_gate_w, gla_gate_b, gla_norm_g, gla_w_out, lru_w_in, lru_conv_w, lru_conv_b, lru_gate_a_w, lru_gate_a_b, lru_gate_x_w, lru_gate_x_b, lru_lambda, lru_w_out, mla_w_in, mla_q_norm_g, mla_kv_norm_g, mla_w_uq, mla_w_ukv, mla_w_out, moe_router, moe_w_gate, moe_w_up, moe_w_down, ln_g, ln_b):
    B, S, D = x.shape
    xf = x.reshape(B * S, D)
    for i in range(DEPTH):
        m = i % N_MIXERS
        j = i // N_MIXERS
        g0, b0, wr = ln_g[i, 0], ln_b[i, 0], moe_router[i]
        if m == 0:
            xf, xb, lg = _mlstm_layer(xf, B, S, mlstm_w_in[j], mlstm_gate_b[j], mlstm_norm_g[j],
                                      mlstm_w_out[j], g0, b0, wr)
        elif m == 1:
            xf, xb, lg = _gla_layer(xf, B, S, gla_w_in[j], gla_gate_w[j], gla_gate_b[j], gla_norm_g[j],
                                    gla_w_out[j], g0, b0, wr)
        elif m == 2:
            xf, xb, lg = _lru_layer(xf, B, S, lru_w_in[j], lru_conv_w[j], lru_conv_b[j], lru_gate_a_w[j],
                                    lru_gate_a_b[j], lru_gate_x_w[j], lru_gate_x_b[j], lru_lambda[j],
                                    lru_w_out[j], g0, b0, wr)
        else:
            xf, xb, lg = _mla_layer(xf, B, S, positions, mla_w_in[j], mla_q_norm_g[j], mla_kv_norm_g[j],
                                    mla_w_uq[j], mla_w_ukv[j], mla_w_out[j], g0, b0, wr)
        xf = _moe_layer(xf, xb, lg, B, S, i, moe_w_gate, moe_w_up, moe_w_down, ln_g[i, 1], ln_b[i, 1])
    return xf.reshape(B, S, D)
```

```python
import functools
import math

import jax
import jax.numpy as jnp
from jax import lax
from jax.experimental import pallas as pl
from jax.experimental.pallas import tpu as pltpu

F32 = jnp.float32
BF16 = jnp.bfloat16

D_MODEL = 1024
DEPTH = 4
N_MIXERS = 4
ALPHA = (2 * DEPTH) ** 0.25
LN_EPS = 1e-5

ML_HEADS = 4
ML_DV = D_MODEL // ML_HEADS
ML_DK = ML_DV // 2

GLA_HEADS = 4
GLA_DK = D_MODEL // 2 // GLA_HEADS
GLA_DV = D_MODEL // GLA_HEADS
GLA_RANK = 16
GLA_TAU = 16.0

LRU_WIDTH = D_MODEL
LRU_BLOCKS = 4
LRU_BW = LRU_WIDTH // LRU_BLOCKS
CONV_WIDTH = 4
LRU_C = 8.0

MLA_HEADS = 8
MLA_NOPE = 128
MLA_ROPE = 64
MLA_DV = 128
MLA_Q_RANK = 384
MLA_KV_RANK = 256
ROPE_THETA = 10000.0

N_EXPERTS = 16
CAPACITY_FACTOR = 2

V7X_VMEM_BYTES = 64 * 1024 * 1024
VMEM_LIMIT = V7X_VMEM_BYTES - 8 * 1024 * 1024
LANES = 128
SUBLANES = 8

ROW_TILE = 1024
SEQ_TILE = 1024
ML_CHUNK = 512
GLA_CHUNK = 128
Q_TILE = 2048
Q_SUB = 256
ATTN_HEADS_PER_STEP = 4
N_CHUNK = 512
FFN_GROUP = 4
COMBINE_TILE = 1024
COMBINE_CHAINS = 4
FF_CHUNK = 512


def _params(*sem):
    return pltpu.CompilerParams(dimension_semantics=sem, vmem_limit_bytes=VMEM_LIMIT)


def _log_sigmoid(x):
    return jnp.minimum(x, 0.0) - jnp.log(1.0 + jnp.exp(-jnp.abs(x)))


def _sigmoid(x):
    return 0.5 * jnp.tanh(0.5 * x) + 0.5


def _layer_norm(v, g, b):
    mu = jnp.mean(v, axis=-1, keepdims=True)
    d = v - mu
    var = jnp.mean(d * d, axis=-1, keepdims=True)
    return d * lax.rsqrt(var + LN_EPS) * g + b


def _dot(a, b):
    return jnp.dot(a, b, preferred_element_type=F32)


def _dot_nt(a, b):
    return lax.dot_general(a, b, (((1,), (1,)), ((), ())), preferred_element_type=F32)


def _split3(a):
    a0 = a.astype(BF16)
    r1 = a - a0.astype(F32)
    a1 = r1.astype(BF16)
    a2 = (r1 - a1.astype(F32)).astype(BF16)
    return a0, a1, a2


def _dot_tn(a, b):
    return lax.dot_general(a, b, (((0,), (0,)), ((), ())), preferred_element_type=F32)


def _dense_kernel(*refs, n_w, n_t, has_bias):
    x_ref = refs[0]
    pos = 1
    w_refs = refs[pos:pos + n_w]
    pos += n_w
    b_refs = []
    for hb in has_bias:
        if hb:
            b_refs.append(refs[pos])
            pos += 1
        else:
            b_refs.append(None)
    t_refs = refs[pos:pos + 2 * n_t]
    pos += 2 * n_t
    o_refs = refs[pos:pos + n_w]
    pos += n_w
    ot_refs = refs[pos:pos + n_t]

    xb = x_ref[...].astype(BF16)
    for w_ref, b_ref, o_ref in zip(w_refs, b_refs, o_refs):
        n = w_ref.shape[1]
        for j0 in range(0, n, N_CHUNK):
            j1 = min(n, j0 + N_CHUNK)
            acc = _dot(xb, w_ref[:, j0:j1])
            if b_ref is not None:
                acc = acc + b_ref[:, j0:j1]
            o_ref[:, j0:j1] = acc.astype(o_ref.dtype)
    for i in range(n_t):
        wt_ref, bt_ref = t_refs[2 * i], t_refs[2 * i + 1]
        ot_refs[i][...] = (_dot_nt(wt_ref[...], xb) + bt_ref[...]).astype(ot_refs[i].dtype)


def _dense(x, ws, dtypes, biases=None, transposed=()):
    T, K = x.shape
    tm = min(ROW_TILE, T)
    if biases is None:
        biases = [None] * len(ws)
    has_bias = tuple(b is not None for b in biases)
    args = [x] + list(ws) + [b for b in biases if b is not None]
    in_specs = [pl.BlockSpec((tm, K), lambda i: (i, 0))]
    in_specs += [pl.BlockSpec(w.shape, lambda i: (0, 0)) for w in ws]
    in_specs += [pl.BlockSpec(b.shape, lambda i: (0, 0)) for b in biases if b is not None]
    for wt, bt, _ in transposed:
        args += [wt, bt]
        in_specs += [pl.BlockSpec(wt.shape, lambda i: (0, 0)), pl.BlockSpec(bt.shape, lambda i: (0, 0))]
    out_shape = [jax.ShapeDtypeStruct((T, w.shape[1]), dt) for w, dt in zip(ws, dtypes)]
    out_specs = [pl.BlockSpec((tm, w.shape[1]), lambda i: (i, 0)) for w in ws]
    for wt, _, dt in transposed:
        out_shape.append(jax.ShapeDtypeStruct((wt.shape[0], T), dt))
        out_specs.append(pl.BlockSpec((wt.shape[0], tm), lambda i: (0, i)))
    kern = functools.partial(_dense_kernel, n_w=len(ws), n_t=len(transposed), has_bias=has_bias)
    return pl.pallas_call(
        kern, grid=(T // tm,), in_specs=in_specs, out_specs=out_specs, out_shape=out_shape,
        compiler_params=_params("parallel"), name="dense")(*args)


def _outproj_ln_kernel(a_ref, w_ref, x_ref, g_ref, b_ref, wr_ref, o_ref, ob_ref, lg_ref):
    y = _dot(a_ref[...], w_ref[...])
    v = _layer_norm(ALPHA * x_ref[...] + y, g_ref[...], b_ref[...])
    o_ref[...] = v
    E = lg_ref.shape[0]
    vh = v.astype(BF16)
    vl = (v - vh.astype(F32)).astype(BF16)
    both = _dot_nt(wr_ref[...], vh)
    lg_ref[...] = both[:E] + (both[E:] + _dot_nt(wr_ref[:E, :], vl))
    half = v.shape[1] // 2
    bits = pltpu.bitcast(v.astype(BF16).astype(F32), jnp.uint32)
    ob_ref[...] = (bits[:, :half] >> 16) | bits[:, half:]


def _outproj_ln(a, w, x, g, b, w_router):
    T, K = a.shape
    E = w_router.shape[1]
    wr = w_router.astype(F32).T
    wh = wr.astype(BF16)
    wr2 = jnp.concatenate([wh, (wr - wh.astype(F32)).astype(BF16)], axis=0)
    D = w.shape[1]
    tm = min(ROW_TILE, T)
    row = lambda i: (i, 0)
    fix = lambda i: (0, 0)
    return pl.pallas_call(
        _outproj_ln_kernel, grid=(T // tm,),
        in_specs=[pl.BlockSpec((tm, K), row), pl.BlockSpec((K, D), fix), pl.BlockSpec((tm, D), row),
                  pl.BlockSpec((1, D), fix), pl.BlockSpec((1, D), fix), pl.BlockSpec((2 * E, D), fix)],
        out_specs=[pl.BlockSpec((tm, D), row), pl.BlockSpec((tm, D // 2), row),
                   pl.BlockSpec((E, tm), lambda i: (0, i))],
        out_shape=[jax.ShapeDtypeStruct((T, D), F32), jax.ShapeDtypeStruct((T, D // 2), jnp.uint32),
                   jax.ShapeDtypeStruct((E, T), F32)],
        compiler_params=_params("parallel"), name="outproj_ln")(a, w, x, g.reshape(1, D), b.reshape(1, D), wr2)


def _mlstm_kernel(*refs, reverse, ts, chunk):
    if reverse:
        qv_ref, kt_ref, gr_ref, hf_ref, o_ref, ng_ref, out_ref, c_ref, m_ref = refs
    else:
        qv_ref, kt_ref, gr_ref, out_ref, c_ref, m_ref = refs
    H, DK, DV, L = ML_HEADS, ML_DK, ML_DV, chunk
    scale = DK ** -0.5

    @pl.when(pl.program_id(1) == 0)
    def _():
        c_ref[...] = jnp.zeros_like(c_ref)
        m_ref[...] = jnp.zeros_like(m_ref)

    rows = lax.broadcasted_iota(jnp.int32, (L, L), 0)
    cols = lax.broadcasted_iota(jnp.int32, (L, L), 1)
    mask = (cols >= rows) if reverse else (cols <= rows)
    eye = rows == cols
    tri = jnp.where((rows >= cols) if reverse else (rows <= cols), 1.0, 0.0).astype(BF16)
    ones_col = jnp.where(lax.broadcasted_iota(jnp.int32, (L, LANES), 1) == 0, 1.0, 0.0).astype(BF16)
    d0 = 8 if reverse else 0
    last = 0 if reverse else L - 1
    n_chunks = ts // L
    order = range(n_chunks - 1, -1, -1) if reverse else range(n_chunks)
    neg_inf = -jnp.inf

    for c in order:
        r0 = c * L
        g8 = gr_ref[d0:d0 + 8, r0:r0 + L]
        lf8 = _log_sigmoid(g8)
        pieces = _dot(jnp.concatenate(_split3(lf8), axis=0), tri)
        b8 = pieces[0:8] + pieces[8:16] + pieces[16:24]
        u8 = g8[0:4, :] - b8[4:8, :]
        for h in range(H):
            qb = qv_ref[r0:r0 + L, h * DK:(h + 1) * DK]
            kt = kt_ref[h * DK:(h + 1) * DK, r0:r0 + L]
            vx = jnp.concatenate([qv_ref[r0:r0 + L, H * DK + h * DV:H * DK + (h + 1) * DV], ones_col], axis=1)
            u_r = u8[h:h + 1, :]
            b_r = b8[4 + h:5 + h, :]
            m_prev = m_ref[h:h + 1, 0:1]
            b_c = jnp.sum(jnp.where(eye, b_r, 0.0), axis=1, keepdims=True)
            um = jnp.where(mask, u_r, neg_inf)
            a_c = jnp.maximum(m_prev, jnp.max(um, axis=1, keepdims=True))
            s = _dot(qb, kt) * (scale * jnp.exp(um - a_c))
            w_int = jnp.exp(m_prev - a_c) * scale
            c_old = c_ref[h]
            acc = _dot(s.astype(BF16), vx) + w_int * _dot(qb, c_old.astype(BF16))
            den = acc[:, DV:DV + 1]
            hh = acc[:, :DV] / jnp.maximum(jnp.abs(den), jnp.exp(-(a_c + b_c)))
            a_last = jnp.maximum(m_prev, jnp.max(u_r, axis=1, keepdims=True))
            g_tot = b_r[:, last:last + 1]
            wc = jnp.exp(m_prev - a_last)
            kw = (kt.astype(F32) * jnp.exp(u_r - a_last)).astype(BF16)
            c_ref[h] = wc * c_old + _dot(kw, vx)
            m_ref[h:h + 1, :] = jnp.broadcast_to(g_tot + a_last, (1, LANES))
            if reverse:
                hs = hf_ref[r0:r0 + L, h * DV:(h + 1) * DV] + hh
                mu = jnp.mean(hs, axis=-1, keepdims=True)
                dd = hs - mu
                var = jnp.mean(dd * dd, axis=-1, keepdims=True)
                hn = dd * lax.rsqrt(var + LN_EPS) * ng_ref[:, h * DV:(h + 1) * DV]
                og = _sigmoid(o_ref[r0:r0 + L, h * DV:(h + 1) * DV])
                out_ref[r0:r0 + L, h * DV:(h + 1) * DV] = (og * hn).astype(out_ref.dtype)
            else:
                out_ref[r0:r0 + L, h * DV:(h + 1) * DV] = hh


def _mlstm_pass(qv, kt, gr, B, S, reverse, hf=None, o=None, ng=None):
    T = B * S
    ts = min(SEQ_TILE, S)
    chunk = min(ML_CHUNK, ts)
    nb = S // ts
    if reverse:
        blk = lambda b, j: (b * nb + nb - 1 - j, 0)
        blk_t = lambda b, j: (0, b * nb + nb - 1 - j)
    else:
        blk = lambda b, j: (b * nb + j, 0)
        blk_t = lambda b, j: (0, b * nb + j)
    in_specs = [pl.BlockSpec((ts, qv.shape[1]), blk), pl.BlockSpec((kt.shape[0], ts), blk_t),
                pl.BlockSpec((16, ts), blk_t)]
    args = [qv, kt, gr]
    if reverse:
        in_specs += [pl.BlockSpec((ts, D_MODEL), blk), pl.BlockSpec((ts, D_MODEL), blk),
                     pl.BlockSpec((1, D_MODEL), lambda b, j: (0, 0))]
        args += [hf, o, ng.reshape(1, D_MODEL)]
    out_dtype = BF16 if reverse else F32
    kern = functools.partial(_mlstm_kernel, reverse=reverse, ts=ts, chunk=chunk)
    return pl.pallas_call(
        kern, grid=(B, nb), in_specs=in_specs,
        out_specs=pl.BlockSpec((ts, D_MODEL), blk),
        out_shape=jax.ShapeDtypeStruct((T, D_MODEL), out_dtype),
        scratch_shapes=[pltpu.VMEM((ML_HEADS, ML_DK, ML_DV + LANES), F32),
                        pltpu.VMEM((SUBLANES, LANES), F32)],
        compiler_params=_params("parallel", "arbitrary"),
        name="mlstm_bwd" if reverse else "mlstm_fwd")(*args)


def _mlstm_layer(x, B, S, w_in, gate_b, norm_g, w_out, ln_g, ln_b, w_router):
    qk = ML_HEADS * ML_DK
    w = w_in.astype(BF16)
    w_qv = jnp.concatenate([w[:, :qk], w[:, 2 * qk:2 * qk + D_MODEL]], axis=1)
    w_k_t = w[:, qk:2 * qk].T
    w_o = w[:, 2 * qk + D_MODEL:2 * qk + 2 * D_MODEL]
    w_g_t = w[:, 2 * qk + 2 * D_MODEL:].T
    b_g = gate_b.astype(F32).reshape(16, 1)
    qv, o, kt, gr = _dense(x, [w_qv, w_o], [BF16, F32],
                           transposed=[(w_k_t, jnp.zeros((qk, 1), F32), BF16), (w_g_t, b_g, F32)])
    hf = _mlstm_pass(qv, kt, gr, B, S, False)
    a = _mlstm_pass(qv, kt, gr, B, S, True, hf, o, norm_g.astype(F32))
    return _outproj_ln(a, w_out.astype(BF16), x, ln_g, ln_b, w_router)


def _cumsum_rows(x, n, reverse):
    row = lax.broadcasted_iota(jnp.int32, x.shape, 0)
    sh = 1
    while sh < n:
        if reverse:
            x = x + jnp.where(row < n - sh, pltpu.roll(x, n - sh, axis=0), 0.0)
        else:
            x = x + jnp.where(row >= sh, pltpu.roll(x, sh, axis=0), 0.0)
        sh *= 2
    return x


def _gla_kernel(*refs, reverse, ts, chunk):
    if reverse:
        qkv_ref, glr_ref, gw_ref, gb_ref, of_ref, r_ref, ng_ref, out_ref, st_ref, la_ref = refs
    else:
        qkv_ref, glr_ref, gw_ref, gb_ref, out_ref, st_ref, la_ref = refs
    H, DK, DV, L = GLA_HEADS, GLA_DK, GLA_DV, chunk
    scale = DK ** -0.5
    d = 1 if reverse else 0

    @pl.when(pl.program_id(1) == 0)
    def _():
        st_ref[...] = jnp.zeros_like(st_ref)

    glr = glr_ref[:, d * GLA_RANK:(d + 1) * GLA_RANK]
    g0, g1, _ = _split3(glr)
    pre = _dot(jnp.concatenate([g0, g1, g0], axis=1), gw_ref[d]) + gb_ref[d:d + 1, :]
    la_ref[...] = _log_sigmoid(pre) * (1.0 / GLA_TAU)

    rows = lax.broadcasted_iota(jnp.int32, (L, L), 0)
    cols = lax.broadcasted_iota(jnp.int32, (L, L), 1)
    mask = (cols >= rows) if reverse else (cols <= rows)
    last = 0 if reverse else L - 1
    mid = L // 2
    n_chunks = ts // L

    def body(i, carry):
        c = (n_chunks - 1 - i) if reverse else i
        r0 = pl.multiple_of(c * L, L)
        bsum = _cumsum_rows(la_ref[pl.ds(r0, L), :], L, reverse)
        for h in range(H):
            b = bsum[:, h * DK:(h + 1) * DK]
            qf = qkv_ref[pl.ds(r0, L), h * DK:(h + 1) * DK].astype(F32)
            kf = qkv_ref[pl.ds(r0, L), H * DK + h * DK:H * DK + (h + 1) * DK].astype(F32)
            vb = qkv_ref[pl.ds(r0, L), 2 * H * DK + h * DV:2 * H * DK + (h + 1) * DV]
            beta = b[mid:mid + 1, :]
            g = b[last:last + 1, :]
            qt = (qf * jnp.exp(b - beta)).astype(BF16)
            kt = (kf * jnp.exp(beta - b)).astype(BF16)
            amat = jnp.where(mask, _dot_nt(qt, kt) * scale, 0.0)
            qh = (qf * (jnp.exp(b) * scale)).astype(BF16)
            st = st_ref[h]
            o = _dot(amat.astype(BF16), vb) + _dot_nt(qh, st.astype(BF16))
            kh = (kf * jnp.exp(g - b)).astype(BF16)
            st_ref[h] = st * jnp.exp(g) + _dot_tn(vb, kh)
            if reverse:
                hs = of_ref[pl.ds(r0, L), h * DV:(h + 1) * DV] + o
                mu = jnp.mean(hs, axis=-1, keepdims=True)
                dd = hs - mu
                var = jnp.mean(dd * dd, axis=-1, keepdims=True)
                hn = dd * lax.rsqrt(var + LN_EPS) * ng_ref[:, h * DV:(h + 1) * DV]
                rr = r_ref[pl.ds(r0, L), h * DV:(h + 1) * DV]
                out_ref[pl.ds(r0, L), h * DV:(h + 1) * DV] = (rr * _sigmoid(rr) * hn).astype(out_ref.dtype)
            else:
                out_ref[pl.ds(r0, L), h * DV:(h + 1) * DV] = o
        return carry

    lax.fori_loop(0, n_chunks, body, 0, unroll=4)


def _gla_pass(qkv, glr, gw, gb, B, S, reverse, of=None, r=None, ng=None):
    T = B * S
    ts = min(SEQ_TILE, S)
    chunk = min(GLA_CHUNK, ts)
    nb = S // ts
    if reverse:
        blk = lambda b, j: (b * nb + nb - 1 - j, 0)
    else:
        blk = lambda b, j: (b * nb + j, 0)
    fix2 = lambda b, j: (0, 0)
    fix3 = lambda b, j: (0, 0, 0)
    in_specs = [pl.BlockSpec((ts, qkv.shape[1]), blk), pl.BlockSpec((ts, 2 * GLA_RANK), blk),
                pl.BlockSpec(gw.shape, fix3), pl.BlockSpec(gb.shape, fix2)]
    args = [qkv, glr, gw, gb]
    if reverse:
        in_specs += [pl.BlockSpec((ts, D_MODEL), blk), pl.BlockSpec((ts, D_MODEL), blk),
                     pl.BlockSpec((1, D_MODEL), fix2)]
        args += [of, r, ng.reshape(1, D_MODEL)]
    kern = functools.partial(_gla_kernel, reverse=reverse, ts=ts, chunk=chunk)
    return pl.pallas_call(
        kern, grid=(B, nb), in_specs=in_specs,
        out_specs=pl.BlockSpec((ts, D_MODEL), blk),
        out_shape=jax.ShapeDtypeStruct((T, D_MODEL), BF16 if reverse else F32),
        scratch_shapes=[pltpu.VMEM((GLA_HEADS, GLA_DV, GLA_DK), F32),
                        pltpu.VMEM((ts, GLA_HEADS * GLA_DK), F32)],
        compiler_params=_params("parallel", "arbitrary"),
        name="gla_bwd" if reverse else "gla_fwd")(*args)


def _gla_layer(x, B, S, w_in, gate_w, gate_b, norm_g, w_out, ln_g, ln_b, w_router):
    qk = GLA_HEADS * GLA_DK
    w = w_in.astype(BF16)
    w_qkv = w[:, :2 * qk + D_MODEL]
    w_r = w[:, 2 * qk + D_MODEL:2 * qk + 2 * D_MODEL]
    w_glr = w[:, 2 * qk + 2 * D_MODEL:]
    qkv, r, glr = _dense(x, [w_qkv, w_r, w_glr], [BF16, F32, F32])
    gw0 = gate_w.astype(BF16)
    gw1 = (gate_w.astype(F32) - gw0.astype(F32)).astype(BF16)
    gw = jnp.concatenate([gw0, gw0, gw1], axis=1)
    gb = gate_b.astype(F32)
    of = _gla_pass(qkv, glr, gw, gb, B, S, False)
    a = _gla_pass(qkv, glr, gw, gb, B, S, True, of, r, norm_g.astype(F32))
    return _outproj_ln(a, w_out.astype(BF16), x, ln_g, ln_b, w_router)


def _gelu_tanh(x):
    return 0.5 * x * (1.0 + jnp.tanh(math.sqrt(2.0 / math.pi) * (x + 0.044715 * (x * x * x))))


def _lru_kernel(*refs, reverse, ts):
    if reverse:
        (u_ref, up_ref, un_ref, cw_ref, cb_ref, wg_ref, bg_ref, lam_ref, hf_ref, gate_ref,
         out_ref, a_ref, g_ref, h_ref, hs_ref) = refs
    else:
        (u_ref, up_ref, un_ref, cw_ref, cb_ref, wg_ref, bg_ref, lam_ref,
         out_ref, a_ref, g_ref, h_ref) = refs
        hs_ref = out_ref
    W = LRU_WIDTH
    j = pl.program_id(1)
    nb = pl.num_programs(1)
    jj = (nb - 1 - j) if reverse else j

    @pl.when(j == 0)
    def _():
        h_ref[...] = jnp.zeros_like(h_ref)

    z = u_ref[...]
    prev = jnp.where(jj > 0, up_ref[...], 0.0)
    nxt = jnp.where(jj < nb - 1, un_ref[...], 0.0)
    row = lax.broadcasted_iota(jnp.int32, (ts, W), 0)
    zm1 = jnp.where(row == 0, prev[7:8, :], pltpu.roll(z, 1, axis=0))
    zm2 = pltpu.roll(z, 2, axis=0)
    zm2 = jnp.where(row == 0, prev[6:7, :], jnp.where(row == 1, prev[7:8, :], zm2))
    zp1 = jnp.where(row == ts - 1, nxt[0:1, :], pltpu.roll(z, ts - 1, axis=0))
    u = cw_ref[0:1, :] * zm2 + cw_ref[1:2, :] * zm1 + cw_ref[2:3, :] * z + cw_ref[3:4, :] * zp1 + cb_ref[...]

    ls = LRU_C * _log_sigmoid(lam_ref[...])
    ub = u.astype(BF16)
    for n in range(LRU_BLOCKS):
        sl = slice(n * LRU_BW, (n + 1) * LRU_BW)
        pre = _dot(ub[:, sl], wg_ref[n]) + bg_ref[n]
        r = _sigmoid(pre[:, :LRU_BW])
        ig = _sigmoid(pre[:, LRU_BW:])
        log_a = r * ls[:, sl]
        a = jnp.exp(log_a)
        a_ref[:, sl] = a
        om = 1.0 - a * a
        g_ref[:, sl] = jnp.where(om > 0.0, om * lax.rsqrt(om), 0.0) * (ig * u[:, sl])

    n_tiles = ts // SUBLANES

    srow = lax.broadcasted_iota(jnp.int32, (SUBLANES, W), 0)
    carry_row = 0 if reverse else SUBLANES - 1

    def body(i, h):
        t = (n_tiles - 1 - i) if reverse else i
        r0 = pl.multiple_of(t * SUBLANES, SUBLANES)
        a8 = a_ref[pl.ds(r0, SUBLANES), :]
        g8 = g_ref[pl.ds(r0, SUBLANES), :]
        d = 1
        while d < SUBLANES:
            shift = SUBLANES - d if reverse else d
            keep = (srow < SUBLANES - d) if reverse else (srow >= d)
            g8 = g8 + a8 * jnp.where(keep, pltpu.roll(g8, shift, axis=0), 0.0)
            a8 = a8 * jnp.where(keep, pltpu.roll(a8, shift, axis=0), 1.0)
            d *= 2
        hs = g8 + a8 * h
        hs_ref[pl.ds(r0, SUBLANES), :] = hs
        return hs[carry_row:carry_row + 1, :]

    h_ref[...] = lax.fori_loop(0, n_tiles, body, h_ref[...], unroll=8)
    if reverse:
        out_ref[...] = (_gelu_tanh(gate_ref[...]) * (hf_ref[...] + hs_ref[...])).astype(out_ref.dtype)


def _lru_pass(u, cw, cb, wg, bg, lam, B, S, reverse, hf=None, gate=None):
    T = B * S
    W = LRU_WIDTH
    ts = min(SEQ_TILE, S)
    nb = S // ts
    tpb = ts // SUBLANES
    n8 = T // SUBLANES
    if reverse:
        seq = lambda b, j: b * nb + nb - 1 - j
    else:
        seq = lambda b, j: b * nb + j
    blk = lambda b, j: (seq(b, j), 0)
    blk_prev = lambda b, j: (jnp.maximum(seq(b, j) * tpb - 1, 0), 0)
    blk_next = lambda b, j: (jnp.minimum((seq(b, j) + 1) * tpb, n8 - 1), 0)
    fix2 = lambda b, j: (0, 0)
    fix3 = lambda b, j: (0, 0, 0)
    in_specs = [pl.BlockSpec((ts, W), blk), pl.BlockSpec((SUBLANES, W), blk_prev),
                pl.BlockSpec((SUBLANES, W), blk_next), pl.BlockSpec(cw.shape, fix2),
                pl.BlockSpec(cb.shape, fix2), pl.BlockSpec(wg.shape, fix3), pl.BlockSpec(bg.shape, fix3),
                pl.BlockSpec(lam.shape, fix2)]
    args = [u, u, u, cw, cb, wg, bg, lam]
    scratch = [pltpu.VMEM((ts, W), F32), pltpu.VMEM((ts, W), F32), pltpu.VMEM((1, W), F32)]
    if reverse:
        in_specs += [pl.BlockSpec((ts, W), blk), pl.BlockSpec((ts, W), blk)]
        args += [hf, gate]
        scratch.append(pltpu.VMEM((ts, W), F32))
    kern = functools.partial(_lru_kernel, reverse=reverse, ts=ts)
    return pl.pallas_call(
        kern, grid=(B, nb), in_specs=in_specs,
        out_specs=pl.BlockSpec((ts, W), blk),
        out_shape=jax.ShapeDtypeStruct((T, W), BF16 if reverse else F32),
        scratch_shapes=scratch,
        compiler_params=_params("parallel", "arbitrary"),
        name="lru_bwd" if reverse else "lru_fwd")(*args)


def _lru_layer(x, B, S, w_in, conv_w, conv_b, gate_a_w, gate_a_b, gate_x_w, gate_x_b, lam, w_out,
               ln_g, ln_b, w_router):
    W = LRU_WIDTH
    w = w_in.astype(BF16)
    gate, u = _dense(x, [w[:, :W], w[:, W:]], [F32, F32])
    cw = conv_w.astype(F32)
    cb = conv_b.astype(F32).reshape(1, W)
    passes = []
    for d in range(2):
        wg = jnp.concatenate([gate_a_w[d], gate_x_w[d]], axis=-1).astype(BF16)
        bg = jnp.concatenate([gate_a_b[d].reshape(LRU_BLOCKS, 1, LRU_BW),
                              gate_x_b[d].reshape(LRU_BLOCKS, 1, LRU_BW)], axis=-1).astype(F32)
        passes.append((wg, bg, lam[d].astype(F32).reshape(1, W)))
    hf = _lru_pass(u, cw, cb, *passes[0], B, S, False)
    a = _lru_pass(u, cw, cb, *passes[1], B, S, True, hf, gate)
    return _outproj_ln(a, w_out.astype(BF16), x, ln_g, ln_b, w_router)


MLA_HW = MLA_NOPE + LANES


def _mla_proj_kernel(x_ref, pos_ref, win_ref, qg_ref, kg_ref, wqn_ref, wqr_ref, wqs_ref, wkv_ref,
                     fr_ref, sg_ref, q_ref, k_ref, v_ref):
    H = MLA_HEADS
    scale = (MLA_NOPE + MLA_ROPE) ** -0.5 * math.log2(math.e)
    xb = x_ref[...].astype(BF16)
    z = _dot(xb, win_ref[...])
    cq = z[:, :MLA_Q_RANK]
    ckv = z[:, MLA_Q_RANK:MLA_Q_RANK + MLA_KV_RANK]
    kr = z[:, MLA_Q_RANK + MLA_KV_RANK:MLA_Q_RANK + MLA_KV_RANK + LANES]
    krs = z[:, MLA_Q_RANK + MLA_KV_RANK + LANES:]
    qn = (cq * lax.rsqrt(jnp.mean(cq * cq, axis=-1, keepdims=True) + LN_EPS) * qg_ref[...]).astype(BF16)
    kvn = (ckv * lax.rsqrt(jnp.mean(ckv * ckv, axis=-1, keepdims=True) + LN_EPS) * kg_ref[...]).astype(BF16)
    ang = pos_ref[...].astype(F32) * fr_ref[...]
    cosv = jnp.cos(ang)
    lane = lax.broadcasted_iota(jnp.int32, ang.shape, 1)
    cosv = jnp.where(lane < MLA_ROPE, cosv, 0.0)
    sinv = jnp.sin(ang) * sg_ref[...]
    k_rope = kr * cosv + krs * sinv
    kv = _dot(kvn, wkv_ref[...])
    q_nope = _dot(qn, wqn_ref[...])
    q_rope = _dot(qn, wqr_ref[...])
    q_swap = _dot(qn, wqs_ref[...])
    ones_col = jnp.where(lane == 0, 1.0, 0.0).astype(BF16)
    for h in range(H):
        a0 = h * MLA_HW
        q_ref[:, a0:a0 + MLA_NOPE] = (q_nope[:, h * MLA_NOPE:(h + 1) * MLA_NOPE] * scale).astype(BF16)
        qr = q_rope[:, h * LANES:(h + 1) * LANES] * cosv + q_swap[:, h * LANES:(h + 1) * LANES] * sinv
        q_ref[:, a0 + MLA_NOPE:a0 + MLA_HW] = (qr * scale).astype(BF16)
        k_ref[:, a0:a0 + MLA_NOPE] = kv[:, h * 2 * MLA_NOPE:h * 2 * MLA_NOPE + MLA_NOPE].astype(BF16)
        k_ref[:, a0 + MLA_NOPE:a0 + MLA_HW] = k_rope.astype(BF16)
        v_ref[:, 2 * h * MLA_DV:(2 * h + 1) * MLA_DV] = kv[:, h * 2 * MLA_NOPE + MLA_NOPE:(h + 1) * 2 * MLA_NOPE].astype(BF16)
        v_ref[:, (2 * h + 1) * MLA_DV:(2 * h + 2) * MLA_DV] = ones_col


def _attn_kernel(q_ref, k_ref, v_ref, o_ref, *, n_sub):
    rows = q_ref.shape[0] // n_sub
    for hh in range(ATTN_HEADS_PER_STEP):
        k = k_ref[:, hh * MLA_HW:(hh + 1) * MLA_HW]
        v = v_ref[:, hh * 2 * MLA_DV:(hh + 1) * 2 * MLA_DV]
        for i in range(n_sub):
            s = _dot_nt(q_ref[i * rows:(i + 1) * rows, hh * MLA_HW:(hh + 1) * MLA_HW], k)
            p = jnp.exp2(s - jnp.max(s, axis=-1, keepdims=True))
            acc = _dot(p.astype(BF16), v)
            o_ref[i * rows:(i + 1) * rows, hh * MLA_DV:(hh + 1) * MLA_DV] = (
                acc[:, :MLA_DV] / acc[:, MLA_DV:MLA_DV + 1]).astype(o_ref.dtype)


def _pad_rope_cols(w, swap):
    half = MLA_ROPE // 2
    if swap:
        w = jnp.concatenate([w[..., half:], w[..., :half]], axis=-1)
    w = jnp.concatenate([w, jnp.zeros_like(w)], axis=-1)
    return w.reshape(w.shape[0], -1)


def _mla_layer(x, B, S, positions, w_in, q_norm_g, kv_norm_g, w_uq, w_ukv, w_out, ln_g, ln_b, w_router):
    T = B * S
    H = MLA_HEADS
    half = MLA_ROPE // 2
    w_kr = w_in[:, MLA_Q_RANK + MLA_KV_RANK:].reshape(D_MODEL, 1, MLA_ROPE)
    win = jnp.concatenate([w_in[:, :MLA_Q_RANK + MLA_KV_RANK], _pad_rope_cols(w_kr, False),
                           _pad_rope_cols(w_kr, True)], axis=1).astype(BF16)
    wq = w_uq.reshape(MLA_Q_RANK, H, MLA_NOPE + MLA_ROPE)
    wqn = wq[:, :, :MLA_NOPE].reshape(MLA_Q_RANK, H * MLA_NOPE).astype(BF16)
    wqr = _pad_rope_cols(wq[:, :, MLA_NOPE:], False).astype(BF16)
    wqs = _pad_rope_cols(wq[:, :, MLA_NOPE:], True).astype(BF16)
    freq = ROPE_THETA ** (-jnp.arange(half, dtype=F32) / half)
    zeros = jnp.zeros((LANES - MLA_ROPE,), F32)
    fr = jnp.concatenate([freq, freq, zeros]).reshape(1, LANES)
    sg = jnp.concatenate([-jnp.ones((half,), F32), jnp.ones((half,), F32), zeros]).reshape(1, LANES)
    tm = min(ROW_TILE, T)
    row = lambda i: (i, 0)
    fix = lambda i: (0, 0)
    ins = [x, positions.reshape(T, 1), win, q_norm_g.astype(F32).reshape(1, -1),
           kv_norm_g.astype(F32).reshape(1, -1), wqn, wqr, wqs, w_ukv.astype(BF16), fr, sg]
    in_specs = [pl.BlockSpec((tm, D_MODEL), row), pl.BlockSpec((tm, 1), row)]
    in_specs += [pl.BlockSpec(a.shape, fix) for a in ins[2:]]
    q, k, v = pl.pallas_call(
        _mla_proj_kernel, grid=(T // tm,), in_specs=in_specs,
        out_specs=[pl.BlockSpec((tm, H * MLA_HW), row), pl.BlockSpec((tm, H * MLA_HW), row),
                   pl.BlockSpec((tm, 2 * H * MLA_DV), row)],
        out_shape=[jax.ShapeDtypeStruct((T, H * MLA_HW), BF16), jax.ShapeDtypeStruct((T, H * MLA_HW), BF16),
                   jax.ShapeDtypeStruct((T, 2 * H * MLA_DV), BF16)],
        compiler_params=_params("parallel"), name="mla_proj")(*ins)
    tq = min(Q_TILE, S)
    nq = S // tq
    att = pl.pallas_call(
        functools.partial(_attn_kernel, n_sub=tq // min(Q_SUB, tq)), grid=(B, H // ATTN_HEADS_PER_STEP, nq),
        in_specs=[pl.BlockSpec((tq, ATTN_HEADS_PER_STEP * MLA_HW), lambda b, h, i: (b * nq + i, h)),
                  pl.BlockSpec((S, ATTN_HEADS_PER_STEP * MLA_HW), lambda b, h, i: (b, h)),
                  pl.BlockSpec((S, ATTN_HEADS_PER_STEP * 2 * MLA_DV), lambda b, h, i: (b, h))],
        out_specs=pl.BlockSpec((tq, ATTN_HEADS_PER_STEP * MLA_DV), lambda b, h, i: (b * nq + i, h)),
        out_shape=jax.ShapeDtypeStruct((T, H * MLA_DV), BF16),
        compiler_params=_params("parallel", "parallel", "arbitrary"), name="mla_attn")(q, k, v)
    return _outproj_ln(att, w_out.astype(BF16), x, ln_g, ln_b, w_router)


def _router_kernel(lg_ref, pos_ref, gate_ref, idx_ref, *, cap):
    logits = lg_ref[...]
    E, S = logits.shape
    mx = jnp.max(logits, axis=0, keepdims=True)
    ex = jnp.exp(logits - mx)
    aff = ex / jnp.sum(ex, axis=0, keepdims=True)
    bits = pltpu.bitcast(aff, jnp.int32)

    def count_ge(cand):
        return jnp.sum(jnp.where(bits >= cand, 1.0, 0.0), axis=1, keepdims=True)

    def bit_triple(i, thr):
        sh = 28 - 3 * i
        best = thr
        for k in range(1, 8):
            cand = thr | jnp.left_shift(jnp.int32(k), sh)
            best = jnp.where(count_ge(cand) >= cap, cand, best)
        return best

    thr = lax.fori_loop(0, 10, bit_triple, jnp.zeros((E, 1), jnp.int32))
    c0 = thr | 1
    thr = jnp.where(count_ge(c0) >= cap, c0, thr)
    gt = bits > thr
    eq = bits == thr
    need = cap - jnp.sum(jnp.where(gt, 1.0, 0.0), axis=1, keepdims=True)
    r = lax.broadcasted_iota(jnp.int32, (LANES, LANES), 0)
    c = lax.broadcasted_iota(jnp.int32, (LANES, LANES), 1)
    upper = jnp.where(r < c, 1.0, 0.0).astype(BF16)
    off = jnp.zeros((2 * E, 1), F32)
    for blk in range(S // LANES):
        sl = slice(blk * LANES, (blk + 1) * LANES)
        ind = jnp.concatenate([jnp.where(gt[:, sl], 1.0, 0.0), jnp.where(eq[:, sl], 1.0, 0.0)], axis=0)
        pre = _dot(ind.astype(BF16), upper) + off
        off = off + jnp.sum(ind, axis=1, keepdims=True)
        pg, pe = pre[:E], pre[E:]
        sel = gt[:, sl] | (eq[:, sl] & (pe < need))
        slot = pg + jnp.minimum(pe, need)
        pos_ref[:, sl] = jnp.where(sel, slot, -1.0).astype(jnp.int32)

    slots = lax.broadcasted_iota(jnp.int32, (cap, S), 0)
    token = lax.broadcasted_iota(jnp.int32, (1, S), 1)
    tok_hi = (token >> 6).astype(F32)
    tok_lo = (token & 63).astype(F32)
    pad = jnp.zeros((3, S), F32)
    for e in range(E):
        onehot = jnp.where(slots == pos_ref[e:e + 1, :], 1.0, 0.0).astype(BF16)
        a = aff[e:e + 1, :]
        a0 = a.astype(BF16).astype(F32)
        a1 = (a - a0).astype(BF16).astype(F32)
        a2 = a - a0 - a1
        vals = jnp.concatenate([tok_hi, tok_lo, a0, a1, a2, pad], axis=0).astype(BF16)
        res = _dot_nt(vals, onehot)
        idx_ref[e:e + 1, :] = (res[0:1, :] * 64.0 + res[1:2, :]).astype(jnp.int32)
        gate_ref[e:e + 1, :] = res[2:3, :] + res[3:4, :] + res[4:5, :]


def _gather_kernel(idx_ref, xp_ref, xin_ref, *, n_rows):
    def gather(t, carry):
        j0 = pl.multiple_of(t * SUBLANES, SUBLANES)
        rows = [xp_ref[pl.ds(idx_ref[0, 0, j0 + k], 1), :] for k in range(SUBLANES)]
        xin_ref[pl.ds(j0, SUBLANES), :] = jnp.concatenate(rows, axis=0)
        return carry

    lax.fori_loop(0, n_rows // SUBLANES, gather, 0, unroll=8)


def _ffn_kernel(xin_ref, gate_ref, wg_ref, wu_ref, wd_ref, out_ref, wgb_ref, wub_ref, wdb_ref):
    G, _, cap, half = xin_ref.shape

    @pl.when(pl.program_id(1) == 0)
    def _():
        wgb_ref[...] = wg_ref[0, 0].astype(BF16)
        wub_ref[...] = wu_ref[0, 0].astype(BF16)
        wdb_ref[...] = wd_ref[0, 0].astype(BF16)

    words = xin_ref[...].reshape(G * cap, half)
    x_lo = pltpu.bitcast(words << 16, F32).astype(BF16)
    x_hi = pltpu.bitcast(words & jnp.uint32(0xFFFF0000), F32).astype(BF16)
    out = None
    for c0 in range(0, wgb_ref.shape[1], FF_CHUNK):
        c1 = c0 + FF_CHUNK
        hg = _dot(x_lo, wgb_ref[:half, c0:c1]) + _dot(x_hi, wgb_ref[half:, c0:c1])
        hu = _dot(x_lo, wub_ref[:half, c0:c1]) + _dot(x_hi, wub_ref[half:, c0:c1])
        hmid = (hg * _sigmoid(hg) * hu).astype(BF16)
        part = _dot(hmid, wdb_ref[c0:c1, :])
        out = part if out is None else out + part
    eye = (lax.broadcasted_iota(jnp.int32, (cap, cap), 0) == lax.broadcasted_iota(jnp.int32, (cap, cap), 1))
    for g in range(G):
        gate = jnp.sum(jnp.where(eye, gate_ref[g, 0], 0.0), axis=1, keepdims=True)
        out_ref[g, 0] = (out[g * cap:(g + 1) * cap, :] * gate).astype(out_ref.dtype)


def _combine_kernel(outs_ref, pos_ref, x_ref, g_ref, b_ref, o_ref, *, cap):
    E, ts = pos_ref.shape
    rows = ts // COMBINE_CHAINS
    lane = lax.broadcasted_iota(jnp.int32, (rows, cap), 1).astype(F32)
    outs = outs_ref[...]
    for c in range(COMBINE_CHAINS):
        posf = pos_ref[:, c * rows:(c + 1) * rows].astype(F32)
        pos_t = jnp.concatenate([posf, jnp.full((LANES - E, rows), -1.0, F32)], axis=0).T
        onehot = jnp.concatenate(
            [jnp.where(pos_t[:, e:e + 1] == lane, 1.0, 0.0).astype(BF16) for e in range(E)], axis=1)
        y = _dot(onehot, outs)
        o_ref[c * rows:(c + 1) * rows, :] = _layer_norm(ALPHA * x_ref[c * rows:(c + 1) * rows, :] + y,
                                                        g_ref[...], b_ref[...])


def _moe_layer(x, xb, logits, B, S, layer, w_gate, w_up, w_down, ln_g, ln_b):
    T = B * S
    E = N_EXPERTS
    D = D_MODEL
    cap = CAPACITY_FACTOR * S // E
    pos, gate, idx = pl.pallas_call(
        functools.partial(_router_kernel, cap=cap), grid=(B,),
        in_specs=[pl.BlockSpec((E, S), lambda b: (0, b))],
        out_specs=[pl.BlockSpec((E, S), lambda b: (b, 0)), pl.BlockSpec((E, cap), lambda b: (b, 0)),
                   pl.BlockSpec((E, cap), lambda b: (b, 0))],
        out_shape=[jax.ShapeDtypeStruct((B * E, S), jnp.int32), jax.ShapeDtypeStruct((B * E, cap), F32),
                   jax.ShapeDtypeStruct((B * E, cap), jnp.int32)],
        compiler_params=_params("parallel"), name="moe_router")(logits)
    xin = pl.pallas_call(
        functools.partial(_gather_kernel, n_rows=E * cap), grid=(B,),
        in_specs=[pl.BlockSpec((1, 1, E * cap), lambda b: (b, 0, 0), memory_space=pltpu.SMEM),
                  pl.BlockSpec((S, D // 2), lambda b: (b, 0))],
        out_specs=pl.BlockSpec((E * cap, D // 2), lambda b: (b, 0)),
        out_shape=jax.ShapeDtypeStruct((B * E * cap, D // 2), jnp.uint32),
        compiler_params=_params("parallel"), name="moe_gather")(idx.reshape(B, 1, E * cap), xb)
    ff = w_gate.shape[-1]
    G = math.gcd(B, FFN_GROUP)
    outs = pl.pallas_call(
        _ffn_kernel, grid=(E, B // G),
        in_specs=[pl.BlockSpec((G, 1, cap, D // 2), lambda e, b: (b, e, 0, 0)),
                  pl.BlockSpec((G, 1, 1, cap), lambda e, b: (b, e, 0, 0)),
                  pl.BlockSpec((1, 1, D, ff), lambda e, b: (layer, e, 0, 0)),
                  pl.BlockSpec((1, 1, D, ff), lambda e, b: (layer, e, 0, 0)),
                  pl.BlockSpec((1, 1, ff, D), lambda e, b: (layer, e, 0, 0))],
        out_specs=pl.BlockSpec((G, 1, cap, D), lambda e, b: (b, e, 0, 0)),
        out_shape=jax.ShapeDtypeStruct((B, E, cap, D), BF16),
        scratch_shapes=[pltpu.VMEM((D, ff), BF16), pltpu.VMEM((D, ff), BF16), pltpu.VMEM((ff, D), BF16)],
        compiler_params=_params("arbitrary", "arbitrary"), name="moe_ffn")(
            xin.reshape(B, E, cap, D // 2), gate.reshape(B, E, 1, cap), w_gate, w_up, w_down)
    ts = min(COMBINE_TILE, S)
    nb = S // ts
    return pl.pallas_call(
        functools.partial(_combine_kernel, cap=cap), grid=(B, nb),
        in_specs=[pl.BlockSpec((E * cap, D), lambda b, j: (b, 0)),
                  pl.BlockSpec((E, ts), lambda b, j: (b, j)),
                  pl.BlockSpec((ts, D), lambda b, j: (b * nb + j, 0)),
                  pl.BlockSpec((1, D), lambda b, j: (0, 0)), pl.BlockSpec((1, D), lambda b, j: (0, 0))],
        out_specs=pl.BlockSpec((ts, D), lambda b, j: (b * nb + j, 0)),
        out_shape=jax.ShapeDtypeStruct((T, D), F32),
        compiler_params=_params("parallel", "arbitrary"), name="moe_combine")(
            outs.reshape(B * E * cap, D), pos, x, ln_g.reshape(1, D), ln_b.reshape(1, D))


def kernel(x, positions, mlstm_w_in, mlstm_gate_b, mlstm_norm_g, mlstm_w_out, gla_w_in, gla_gate_w, gla_gate_b, gla_norm_g, gla_w_out, lru_w_in, lru_conv_w, lru_conv_b, lru_gate_a_w, lru_gate_a_b, lru_gate_x_w, lru_gate_x_b, lru_lambda, lru_w_out, mla_w_in, mla_q_norm_g, mla_kv_norm_g, mla_w_uq, mla_w_ukv, mla_w_out, moe_router, moe_w_gate, moe_w_up, moe_w_down, ln_g, ln_b):
    B, S, D = x.shape
    xf = x.reshape(B * S, D)
    for i in range(DEPTH):
        m = i % N_MIXERS
        j = i // N_MIXERS
        g0, b0, wr = ln_g[i, 0], ln_b[i, 0], moe_router[i]
        if m == 0:
            xf, xb, lg = _mlstm_layer(xf, B, S, mlstm_w_in[j], mlstm_gate_b[j], mlstm_norm_g[j],
                                      mlstm_w_out[j], g0, b0, wr)
        elif m == 1:
            xf, xb, lg = _gla_layer(xf, B, S, gla_w_in[j], gla_gate_w[j], gla_gate_b[j], gla_norm_g[j],
                                    gla_w_out[j], g0, b0, wr)
        elif m == 2:
            xf, xb, lg = _lru_layer(xf, B, S, lru_w_in[j], lru_conv_w[j], lru_conv_b[j], lru_gate_a_w[j],
                                    lru_gate_a_b[j], lru_gate_x_w[j], lru_gate_x_b[j], lru_lambda[j],
                                    lru_w_out[j], g0, b0, wr)
        else:
            xf, xb, lg = _mla_layer(xf, B, S, positions, mla_w_in[j], mla_q_norm_g[j], mla_kv_norm_g[j],
                                    mla_w_uq[j], mla_w_ukv[j], mla_w_out[j], g0, b0, wr)
        xf = _moe_layer(xf, xb, lg, B, S, i, moe_w_gate, moe_w_up, moe_w_down, ln_g[i, 1], ln_b[i, 1])
    return xf.reshape(B, S, D)
```
